```python
import math
import jax, jax.numpy as jnp
from jax import lax
import numpy as np

D_MODEL = 1024
BATCH = 16
SEQ = 4096
DEPTH = 1
DEC_BATCH = 128
DEC_SEQ = 4
PAST_LEN = 8192
PAGE_SIZE = 128

H_M = 4
DH_M = D_MODEL // 2 // H_M
D_M = H_M * DH_M
H_A = 4
DH_A = D_MODEL // 2 // H_A
D_A = H_A * DH_A
H_IDX = 8
D_IDX = 64
TOPK_MAX = 256
D_FF = -(-8 * D_MODEL // (3 * 256)) * 256
CHUNK = 64
Q_BLOCK = 128
EPS = 1e-6
INDEX_SCALE = D_IDX ** -0.5 * H_IDX ** -0.5
IN_SPLITS = [D_M, D_M, D_M, D_M, H_M, H_M, D_A, DH_A, DH_A, H_IDX * D_IDX, D_IDX, H_IDX]
D_IN = sum(IN_SPLITS)

kernel_name = "hymba_mlstm_dsa_adaln_decode_step"


def rms_norm(x, g):
    xf = x.astype(jnp.float32)
    y = xf * lax.rsqrt(jnp.mean(xf * xf, axis=-1, keepdims=True) + EPS)
    return (y * g.astype(jnp.float32)).astype(x.dtype)


def modulate(h, shift, scale):
    return h * (1.0 + scale) + shift


def alibi_slopes(n):
    return jnp.asarray(2.0 ** (-8.0 * np.arange(1, n + 1) / n), dtype=jnp.float32)


def mixer_input(x, c, w_ada, b_ada, g_norm1, w_in):
    mod = jax.nn.silu(c) @ w_ada + b_ada
    mods = [m[:, None, :] for m in jnp.split(mod, 6, axis=-1)]
    h = modulate(rms_norm(x, g_norm1), mods[0], mods[1])
    p = h @ w_in
    pieces = jnp.split(p, [int(v) for v in np.cumsum(IN_SPLITS)[:-1]], axis=-1)
    return mods, pieces


def split_heads(pieces, b_igate, b_fgate, q_norm_g, k_norm_g):
    mq, mk, mv, mo, mi, mf, aq, ak, av, iq, ik, iw = pieces
    B, S = mq.shape[:2]
    mq = mq.reshape(B, S, H_M, DH_M)
    mk = mk.reshape(B, S, H_M, DH_M) * (DH_M ** -0.5)
    mv = mv.reshape(B, S, H_M, DH_M)
    ig = mi.astype(jnp.float32) + b_igate.astype(jnp.float32)
    lf = jax.nn.log_sigmoid(mf.astype(jnp.float32) + b_fgate.astype(jnp.float32))
    aq = rms_norm(aq.reshape(B, S, H_A, DH_A), q_norm_g)
    ak = rms_norm(ak, k_norm_g)
    iq = iq.reshape(B, S, H_IDX, D_IDX)
    return mq, mk, mv, mo, ig, lf, aq, ak, av, iq, ik, iw


def mlstm_chunk(carry, inp):
    C, n, m = carry
    q, k, v, ig, lf = inp
    L = q.shape[2]
    b = jnp.cumsum(lf, axis=-1)
    causal = jnp.tril(jnp.ones((L, L), dtype=bool))
    dmat = jnp.where(causal, b[..., :, None] - b[..., None, :] + ig[..., None, :], -jnp.inf)
    inter = b + m[..., None]
    m_t = jnp.maximum(inter, jnp.max(dmat, axis=-1))
    w = jnp.exp(dmat - m_t[..., None])
    g = jnp.exp(inter - m_t)
    a = w * jnp.einsum('bhtd,bhsd->bhts', q, k)
    num = g[..., None] * jnp.einsum('bhvk,bhtk->bhtv', C, q) + jnp.einsum('bhts,bhsv->bhtv', a, v)
    den = g * jnp.einsum('bhk,bhtk->bht', n, q) + jnp.sum(a, axis=-1)
    h = num / jnp.maximum(jnp.abs(den), jnp.exp(-m_t))[..., None]
    m_new = m_t[..., -1]
    wend = jnp.exp(b[..., -1:] - b + ig - m_new[..., None])
    gend = jnp.exp(b[..., -1] + m - m_new)
    C_new = gend[..., None, None] * C + jnp.einsum('bhs,bhsv,bhsk->bhvk', wend, v, k)
    n_new = gend[..., None] * n + jnp.einsum('bhs,bhsk->bhk', wend, k)
    return (C_new, n_new, m_new), h


def _chunks(a, nc, chunk):
    B, S, H = a.shape[:3]
    a = a.reshape((B, nc, chunk, H) + a.shape[3:])
    return a.transpose((1, 0, 3, 2) + tuple(range(4, a.ndim)))


def mlstm_group(q, k, v, ig, lf, o, C0, n0, m0, chunk, norm_g):
    B, S, H, d = q.shape
    nc = S // chunk
    f32 = jnp.float32
    xs = (_chunks(q.astype(f32), nc, chunk), _chunks(k.astype(f32), nc, chunk),
          _chunks(v.astype(f32), nc, chunk), _chunks(ig, nc, chunk), _chunks(lf, nc, chunk))
    (C, n, m), h = lax.scan(mlstm_chunk, (C0.astype(f32), n0.astype(f32), m0.astype(f32)), xs)
    h = h.transpose(1, 0, 3, 2, 4).reshape(B, S, H, d)
    h = rms_norm(h, norm_g.reshape(H, d))
    h = h * jax.nn.sigmoid(o.reshape(B, S, H, d).astype(f32))
    return h.reshape(B, S, H * d).astype(o.dtype), C, n, m


def indexer_scores(iq, iw, ik):
    s = jax.nn.relu(jnp.einsum('bqhd,bld->bqhl', iq, ik).astype(jnp.float32))
    return jnp.einsum('bqhl,bqh->bql', s, iw.astype(jnp.float32)) * INDEX_SCALE


def sparse_attend(q, kg, vg, qpos, kpos, valid, slopes):
    logits = jnp.einsum('bqhd,bqkd->bqhk', q, kg).astype(jnp.float32) * (DH_A ** -0.5)
    dist = (qpos[None, :, None] - kpos).astype(jnp.float32)
    logits = logits - slopes[None, None, :, None] * dist[:, :, None, :]
    logits = jnp.where(valid[:, :, None, :], logits, -jnp.inf)
    p = jax.nn.softmax(logits, axis=-1).astype(vg.dtype)
    return jnp.einsum('bqhk,bqkd->bqhd', p, vg)


def _gather_rows(a, idx):
    return jax.vmap(lambda ab, ib: ab[ib])(a, idx)


def dsa_prompt(q, k, v, iq, ik, iw, slopes):
    B, S = q.shape[:2]
    topk = min(TOPK_MAX, S // 4)
    qb = min(Q_BLOCK, S)
    nb = S // qb
    key_pos = jnp.arange(S)

    def to_blocks(a):
        return jnp.moveaxis(a.reshape((B, nb, qb) + a.shape[2:]), 1, 0)

    def block(args):
        q_b, iq_b, iw_b, start = args
        qpos = start + jnp.arange(qb)
        sc = indexer_scores(iq_b, iw_b, ik)
        sc = jnp.where((key_pos[None, :] <= qpos[:, None])[None], sc, -jnp.inf)
        _, idx = lax.top_k(sc, topk)
        valid = idx <= qpos[None, :, None]
        return sparse_attend(q_b, _gather_rows(k, idx), _gather_rows(v, idx), qpos, idx, valid, slopes)

    out = lax.map(block, (to_blocks(q), to_blocks(iq), to_blocks(iw), jnp.arange(nb) * qb))
    return jnp.moveaxis(out, 0, 1).reshape(B, S, H_A, DH_A)


def dsa_sample(q, k_new, v_new, iq, ik_new, iw, cache_k, cache_v, cache_kidx, page_table, slopes):
    DB, T = q.shape[:2]
    n_pages = page_table.shape[1]
    P = n_pages * PAGE_SIZE
    L = P + T
    topk = min(TOPK_MAX, L // 4)
    ik_past = cache_kidx[page_table].reshape(DB, P, D_IDX).astype(ik_new.dtype)
    ik_all = jnp.concatenate([ik_past, ik_new], axis=1)
    qpos = P + jnp.arange(T)
    sc = indexer_scores(iq, iw, ik_all)
    sc = jnp.where((jnp.arange(L)[None, :] <= qpos[:, None])[None], sc, -jnp.inf)
    _, idx = lax.top_k(sc, topk)
    valid = idx <= qpos[None, :, None]
    pidx = jnp.minimum(idx, P - 1)
    phys = jax.vmap(lambda pt, i: pt[i])(page_table, pidx // PAGE_SIZE)
    off = pidx % PAGE_SIZE
    nidx = jnp.clip(idx - P, 0, T - 1)
    is_new = (idx >= P)[..., None]
    kg = jnp.where(is_new, _gather_rows(k_new, nidx), cache_k[phys, off].astype(k_new.dtype))
    vg = jnp.where(is_new, _gather_rows(v_new, nidx), cache_v[phys, off].astype(v_new.dtype))
    return sparse_attend(q, kg, vg, qpos, idx, valid, slopes)


def block_output(x, mods, h_m, h_a, w_out, g_norm2, w_gate, w_up, w_down):
    gate1, shift2, scale2, gate2 = mods[2], mods[3], mods[4], mods[5]
    B, S = x.shape[:2]
    mix = jnp.concatenate([h_m, h_a.reshape(B, S, D_A)], axis=-1) @ w_out
    x = x + gate1 * mix
    h = modulate(rms_norm(x, g_norm2), shift2, scale2)
    ff = (jax.nn.silu(h @ w_gate) * (h @ w_up)) @ w_down
    return x + gate2 * ff


def setup_inputs(seed: int = 0) -> dict:
    key = jax.random.key(seed)
    ks = jax.random.split(key, 32)
    nrm = jax.random.normal
    n_pages = PAST_LEN // PAGE_SIZE
    n_used = DEC_BATCH * n_pages
    n_phys = (n_used * 5) // 4
    perm = jax.random.permutation(ks[0], n_phys)[:n_used]
    page_table = perm.reshape(DEC_BATCH, n_pages).astype(jnp.int32)
    s = D_MODEL ** -0.5
    return {
        "x_prompt": nrm(ks[1], (BATCH, SEQ, D_MODEL), jnp.float32),
        "x_sample": nrm(ks[2], (DEC_BATCH, DEC_SEQ, D_MODEL), jnp.float32),
        "cache_k": nrm(ks[3], (n_phys, PAGE_SIZE, DH_A), jnp.float32),
        "cache_v": nrm(ks[4], (n_phys, PAGE_SIZE, DH_A), jnp.float32),
        "cache_kidx": nrm(ks[5], (n_phys, PAGE_SIZE, D_IDX), jnp.float32),
        "state_C": 0.1 * nrm(ks[6], (DEC_BATCH, H_M, DH_M, DH_M), jnp.float32),
        "state_n": 0.1 * nrm(ks[7], (DEC_BATCH, H_M, DH_M), jnp.float32),
        "state_m": 0.5 * nrm(ks[8], (DEC_BATCH, H_M), jnp.float32),
        "page_table": page_table,
        "c_prompt": nrm(ks[9], (BATCH, D_MODEL), jnp.float32),
        "c_sample": nrm(ks[10], (DEC_BATCH, D_MODEL), jnp.float32),
        "w_ada": 0.5 * s * nrm(ks[11], (D_MODEL, 6 * D_MODEL), jnp.float32),
        "b_ada": 0.01 * nrm(ks[12], (6 * D_MODEL,), jnp.float32),
        "g_norm1": 1.0 + 0.01 * nrm(ks[13], (D_MODEL,), jnp.float32),
        "w_in": s * nrm(ks[14], (D_MODEL, D_IN), jnp.float32),
        "b_igate": 0.1 * nrm(ks[15], (H_M,), jnp.float32),
        "b_fgate": jnp.linspace(3.0, 6.0, H_M, dtype=jnp.float32) + 0.1 * nrm(ks[16], (H_M,), jnp.float32),
        "mlstm_norm_g": 1.0 + 0.01 * nrm(ks[17], (D_M,), jnp.float32),
        "q_norm_g": 1.0 + 0.01 * nrm(ks[18], (DH_A,), jnp.float32),
        "k_norm_g": 1.0 + 0.01 * nrm(ks[19], (DH_A,), jnp.float32),
        "w_out": (D_M + D_A) ** -0.5 * nrm(ks[20], (D_M + D_A, D_MODEL), jnp.float32),
        "g_norm2": 1.0 + 0.01 * nrm(ks[21], (D_MODEL,), jnp.float32),
        "w_gate": s * nrm(ks[22], (D_MODEL, D_FF), jnp.float32),
        "w_up": s * nrm(ks[23], (D_MODEL, D_FF), jnp.float32),
        "w_down": D_FF ** -0.5 * nrm(ks[24], (D_FF, D_MODEL), jnp.float32),
    }


def reference(x_prompt, x_sample, cache_k, cache_v, cache_kidx, state_C, state_n, state_m, page_table,
              c_prompt, c_sample, w_ada, b_ada, g_norm1, w_in, b_igate, b_fgate, mlstm_norm_g,
              q_norm_g, k_norm_g, w_out, g_norm2, w_gate, w_up, w_down):
    slopes = alibi_slopes(H_A)
    yp, ys = x_prompt, x_sample
    for _ in range(DEPTH):
        mods_p, pieces_p = mixer_input(yp, c_prompt, w_ada, b_ada, g_norm1, w_in)
        mq, mk, mv, mo, ig, lf, aq, ak, av, iq, ik, iw = split_heads(pieces_p, b_igate, b_fgate, q_norm_g, k_norm_g)
        bp, sp = yp.shape[:2]
        zC = jnp.zeros((bp, H_M, DH_M, DH_M), jnp.float32)
        zn = jnp.zeros((bp, H_M, DH_M), jnp.float32)
        zm = jnp.zeros((bp, H_M), jnp.float32)
        hm_p, C_p, n_p, m_p = mlstm_group(mq, mk, mv, ig, lf, mo, zC, zn, zm, min(CHUNK, sp), mlstm_norm_g)
        ha_p = dsa_prompt(aq, ak, av, iq, ik, iw, slopes)
        yp = block_output(yp, mods_p, hm_p, ha_p, w_out, g_norm2, w_gate, w_up, w_down)
        k_p, v_p, kidx_p = ak, av, ik
        mods_s, pieces_s = mixer_input(ys, c_sample, w_ada, b_ada, g_norm1, w_in)
        mq, mk, mv, mo, ig, lf, aq, ak, av, iq, ik, iw = split_heads(pieces_s, b_igate, b_fgate, q_norm_g, k_norm_g)
        hm_s, C_s, n_s, m_s = mlstm_group(mq, mk, mv, ig, lf, mo, state_C, state_n, state_m, ys.shape[1], mlstm_norm_g)
        ha_s = dsa_sample(aq, ak, av, iq, ik, iw, cache_k, cache_v, cache_kidx, page_table, slopes)
        ys = block_output(ys, mods_s, hm_s, ha_s, w_out, g_norm2, w_gate, w_up, w_down)
        k_s, v_s, kidx_s = ak, av, ik
    return (yp, ys, k_p, v_p, kidx_p, C_p, n_p, m_p, k_s, v_s, kidx_s, C_s, n_s, m_s)
```

```python
import functools

import jax
import jax.numpy as jnp
import numpy as np
from jax import lax
from jax.experimental import pallas as pl
from jax.experimental.pallas import tpu as pltpu

F32 = jnp.float32
BF16 = jnp.bfloat16
I32 = jnp.int32

H_M = 4
DH_M = 128
H_A = 4
DH_A = 128
H_IDX = 8
D_IDX = 64
TOPK_MAX = 256
PAGE = 128
EPS = 1e-6
INDEX_SCALE = D_IDX ** -0.5 * H_IDX ** -0.5
ALIBI_SLOPES = tuple(float(2.0 ** (-8.0 * (h + 1) / H_A)) for h in range(H_A))

LANES = 128
VMEM_LIMIT = 56 * 1024 * 1024
NEG = -1e30
I32_MIN = -2 ** 31
NEGINF_KEY = -2139095041
INDEX_BITS = 14

C_MQ, C_MK, C_MV, C_MO, C_AQ, C_AK, C_AV, C_IQ, C_IK, C_SM, C_END = (
    0, 512, 1024, 1536, 2048, 2560, 2688, 2816, 3328, 3456, 3584)


def _cparams(sem):
    return pltpu.CompilerParams(dimension_semantics=sem, vmem_limit_bytes=VMEM_LIMIT)


def _sigmoid(x):
    return 1.0 / (1.0 + jnp.exp(-x))


def _log_sigmoid(x):
    return jnp.minimum(x, 0.0) - jnp.log1p(jnp.exp(-jnp.abs(x)))


def _dot(a, b):
    return jnp.dot(a, b, preferred_element_type=F32)


def _dot_nt(a, b):
    return lax.dot_general(a, b, (((1,), (1,)), ((), ())), preferred_element_type=F32)


def _dot_tn(a, b):
    return lax.dot_general(a, b, (((0,), (0,)), ((), ())), preferred_element_type=F32)


def _split3(x):
    hi = x.astype(BF16)
    r1 = x - hi.astype(F32)
    mid = r1.astype(BF16)
    lo = (r1 - mid.astype(F32)).astype(BF16)
    return hi, mid, lo


def _sort_key(x):
    b = pltpu.bitcast(x, I32)
    return b ^ ((b >> 31) & 0x7FFFFFFF)


def _ada_kernel(c_ref, w_ref, b_ref, o_ref):
    c = c_ref[...]
    s = c * _sigmoid(c)
    o_ref[...] = _dot(s.astype(BF16), w_ref[...]) + b_ref[...]


def _ada(c, w_bf, b):
    r, d = c.shape
    n = w_bf.shape[1]
    return pl.pallas_call(
        _ada_kernel,
        grid=(n // d,),
        in_specs=[pl.BlockSpec((r, d), lambda j: (0, 0)),
                  pl.BlockSpec((d, d), lambda j: (0, j)),
                  pl.BlockSpec((1, d), lambda j: (0, j))],
        out_specs=pl.BlockSpec((r, d), lambda j: (0, j)),
        out_shape=jax.ShapeDtypeStruct((r, n), F32),
        compiler_params=_cparams(("arbitrary",)),
        name="ada",
    )(c, w_bf, b.reshape(1, n))


def _inproj_kernel(x_ref, mod_ref, g1_ref, w_ref, wgt_ref, bsm_ref, brow_ref, qg_ref, kg_ref,
                   mqkv_ref, mo_ref, aq_ref, k_ref, v_ref, kidx_ref, kb_ref, vb_ref, ikb_ref,
                   iq_ref, small_ref, grow_ref):
    x = x_ref[...]
    xn = x * lax.rsqrt(jnp.mean(x * x, axis=-1, keepdims=True) + EPS) * g1_ref[...]
    h = xn * (1.0 + mod_ref[1, 0]) + mod_ref[0, 0]
    hb = h.astype(BF16)

    def sec(a, b):
        return _dot(hb, w_ref[:, a:b])

    mqkv_ref[:, 0:512] = sec(C_MQ, C_MK).astype(BF16)
    mqkv_ref[:, 512:1024] = (sec(C_MK, C_MV) * (DH_M ** -0.5)).astype(BF16)
    mqkv_ref[:, 1024:1536] = sec(C_MV, C_MO).astype(BF16)
    mo_ref[...] = sec(C_MO, C_AQ)

    aq = sec(C_AQ, C_AK)
    qg = qg_ref[...]
    for hh in range(H_A):
        a = aq[:, hh * DH_A:(hh + 1) * DH_A]
        a = a * lax.rsqrt(jnp.mean(a * a, axis=-1, keepdims=True) + EPS) * qg
        aq_ref[:, hh * DH_A:(hh + 1) * DH_A] = (a * (DH_A ** -0.5)).astype(BF16)

    ak = sec(C_AK, C_AV)
    ak = ak * lax.rsqrt(jnp.mean(ak * ak, axis=-1, keepdims=True) + EPS) * kg_ref[...]
    k_ref[...] = ak
    kb_ref[...] = ak.astype(BF16)
    av = sec(C_AV, C_IQ)
    v_ref[...] = av
    vb_ref[...] = av.astype(BF16)

    iq_ref[...] = sec(C_IQ, C_IK).astype(BF16)
    ik = sec(C_IK, C_SM)[:, 0:D_IDX]
    kidx_ref[...] = ik
    ikb_ref[...] = ik.astype(BF16)

    sm = sec(C_SM, C_END)[:, 0:16] + bsm_ref[...]
    col = lax.broadcasted_iota(I32, sm.shape, 1)
    small_ref[...] = jnp.where(col < H_IDX, sm * INDEX_SCALE,
                               jnp.where(col < H_IDX + H_M, sm, _log_sigmoid(sm)))

    gr = _dot_nt(wgt_ref[...], hb) + brow_ref[...]
    row = lax.broadcasted_iota(I32, gr.shape, 0)
    grow_ref[...] = jnp.where(row < H_M, gr, _log_sigmoid(gr))


def _inproj(x, mod_in, per_token_mod, rows_per_mod, g1, w_r, wgt, bsm, brow, qg, kg, tm):
    n, d = x.shape
    if per_token_mod:
        mod_spec = pl.BlockSpec((2, 1, tm, d), lambda i: (0, 0, i, 0))
    else:
        tiles = rows_per_mod // tm
        mod_spec = pl.BlockSpec((2, 1, 1, d), lambda i: (0, i // tiles, 0, 0))

    def full(a):
        return pl.BlockSpec(a.shape, lambda i: (0,) * a.ndim)

    def rows(w):
        return pl.BlockSpec((tm, w), lambda i: (i, 0))

    outs = [(1536, BF16), (512, F32), (512, BF16), (128, F32), (128, F32), (D_IDX, F32),
            (128, BF16), (128, BF16), (D_IDX, BF16), (512, BF16), (16, F32)]
    out_shape = [jax.ShapeDtypeStruct((n, w), dt) for w, dt in outs]
    out_specs = [rows(w) for w, _ in outs]
    out_shape.append(jax.ShapeDtypeStruct((8, n), F32))
    out_specs.append(pl.BlockSpec((8, tm), lambda i: (0, i)))
    return pl.pallas_call(
        _inproj_kernel,
        grid=(n // tm,),
        in_specs=[rows(d), mod_spec, full(g1), full(w_r), full(wgt), full(bsm), full(brow),
                  full(qg), full(kg)],
        out_specs=out_specs,
        out_shape=out_shape,
        compiler_params=_cparams(("parallel",)),
        name="inproj",
    )(x, mod_in, g1, w_r, wgt, bsm, brow, qg, kg)


def _mlstm_kernel(q_ref, k_ref, v_ref, small_ref, grow_ref, o_ref, ng_ref, c0_ref, n0_ref, m0_ref,
                  h_ref, c_out_ref, n_out_ref, m_out_ref, state_sc, m_sc, *, L, nc):
    c = pl.program_id(1)

    @pl.when(c == 0)
    def _():
        state_sc[...] = jnp.zeros(state_sc.shape, F32)
        for hh in range(H_M):
            state_sc[hh, 0:DH_M, :] = c0_ref[0, hh]
            state_sc[hh, DH_M:DH_M + 1, :] = n0_ref[0, hh:hh + 1, :]
        m_sc[...] = jnp.zeros(m_sc.shape, F32)
        m_sc[0:H_M, :] = m0_ref[0]

    ti = lax.broadcasted_iota(I32, (L, L), 0)
    si = lax.broadcasted_iota(I32, (L, L), 1)
    causal = si <= ti
    tri = jnp.where(causal, 1.0, 0.0).astype(BF16)
    tri_t = jnp.where(ti <= si, 1.0, 0.0).astype(BF16)

    sm = small_ref[...]
    col = lax.broadcasted_iota(I32, sm.shape, 1)
    lf_cols = jnp.where(col >= H_IDX + H_M, sm, 0.0)
    hi, mid, lo = _split3(lf_cols)
    b_cols = _dot(tri, hi) + _dot(tri, mid) + _dot(tri, lo)
    gr = grow_ref[0]
    row = lax.broadcasted_iota(I32, gr.shape, 0)
    lf_rows = jnp.where(row >= H_M, gr, 0.0)
    hi, mid, lo = _split3(lf_rows)
    b_rows = _dot(hi, tri_t) + _dot(mid, tri_t) + _dot(lo, tri_t)

    lane = lax.broadcasted_iota(I32, (L, DH_M), 1)
    ones_col = jnp.where(lane == 0, 1.0, 0.0).astype(BF16)

    for hh in range(H_M):
        q = q_ref[:, hh * DH_M:(hh + 1) * DH_M]
        k = k_ref[:, hh * DH_M:(hh + 1) * DH_M]
        v = v_ref[:, hh * DH_M:(hh + 1) * DH_M]
        ig_c = sm[:, H_IDX + hh:H_IDX + hh + 1]
        b_c = b_cols[:, H_IDX + H_M + hh:H_IDX + H_M + hh + 1]
        ig_r = gr[hh:hh + 1, :]
        b_r = b_rows[H_M + hh:H_M + hh + 1, :]
        m_prev = m_sc[hh:hh + 1, 0:1]

        dmat = jnp.where(causal, b_c - b_r + ig_r, -jnp.inf)
        inter = b_c + m_prev
        m_t = jnp.maximum(inter, jnp.max(dmat, axis=-1, keepdims=True))
        w = jnp.exp(dmat - m_t)
        g = jnp.exp(inter - m_t)
        a = w * _dot_nt(q, k)
        st = state_sc[hh]
        qs = _dot_nt(q, st.astype(BF16))
        num = g * qs[:, 0:DH_M] + _dot(a.astype(BF16), v)
        den = g * qs[:, DH_M:DH_M + 1] + jnp.sum(a, axis=-1, keepdims=True)
        hv = num / jnp.maximum(jnp.abs(den), jnp.exp(-m_t))

        m_new = m_t[L - 1:L, :]
        b_last = b_c[L - 1:L, :]
        wend = jnp.exp(b_last - b_c + ig_c - m_new)
        gend = jnp.exp(b_last + m_prev - m_new)
        kw = (k.astype(F32) * wend).astype(BF16)
        v_aug = jnp.concatenate([v, ones_col], axis=1)
        state_sc[hh] = gend * st + _dot_tn(v_aug, kw)
        m_sc[hh:hh + 1, :] = jnp.broadcast_to(m_new, (1, LANES))

        hn = hv * lax.rsqrt(jnp.mean(hv * hv, axis=-1, keepdims=True) + EPS)
        hn = hn * ng_ref[:, hh * DH_M:(hh + 1) * DH_M]
        hn = hn * _sigmoid(o_ref[:, hh * DH_M:(hh + 1) * DH_M])
        h_ref[:, hh * DH_M:(hh + 1) * DH_M] = hn.astype(BF16)

    @pl.when(c == nc - 1)
    def _():
        for hh in range(H_M):
            c_out_ref[0, hh] = state_sc[hh, 0:DH_M, :]
            n_out_ref[0, hh:hh + 1, :] = state_sc[hh, DH_M:DH_M + 1, :]
        m_out_ref[0] = m_sc[0:H_M, :]


def _mlstm(mqkv, small, grow3, mo, ng, c0, n0, m0b, nb, L):
    n = mqkv.shape[0]
    nc = n // nb // L
    d = H_M * DH_M

    def tok(w, blk=0):
        return pl.BlockSpec((L, w), lambda b, c, blk=blk: (b * nc + c, blk))

    per_b3 = pl.BlockSpec((1, H_M, LANES), lambda b, c: (b, 0, 0))
    per_b4 = pl.BlockSpec((1, H_M, DH_M, DH_M), lambda b, c: (b, 0, 0, 0))
    return pl.pallas_call(
        functools.partial(_mlstm_kernel, L=L, nc=nc),
        grid=(nb, nc),
        in_specs=[tok(d, 0), tok(d, 1), tok(d, 2), tok(16),
                  pl.BlockSpec((1, 8, L), lambda b, c: (b * nc + c, 0, 0)),
                  tok(d), pl.BlockSpec((1, d), lambda b, c: (0, 0)),
                  per_b4, per_b3, per_b3],
        out_specs=[tok(d), per_b4, per_b3, per_b3],
        out_shape=[jax.ShapeDtypeStruct((n, d), BF16),
                   jax.ShapeDtypeStruct((nb, H_M, DH_M, DH_M), F32),
                   jax.ShapeDtypeStruct((nb, H_M, DH_M), F32),
                   jax.ShapeDtypeStruct((nb, H_M, LANES), F32)],
        scratch_shapes=[pltpu.VMEM((H_M, 2 * DH_M, DH_M), F32), pltpu.VMEM((8, LANES), F32)],
        compiler_params=_cparams(("parallel", "arbitrary")),
        name="mlstm",
    )(mqkv, mqkv, mqkv, small, grow3, mo, ng, c0, n0, m0b)


def _kth_largest_key(count_ge, rows, k):
    def body(p, carry):
        t_u, cge = carry
        cand_u = t_u | jnp.left_shift(jnp.int32(1), 31 - p)
        cnt = count_ge(cand_u ^ I32_MIN)
        take = cnt >= k
        return jnp.where(take, cand_u, t_u), jnp.where(take, cnt, cge)

    t_u, cge = lax.fori_loop(0, 32, body, (jnp.zeros((rows, 1), I32), jnp.full((rows, 1), 1e9, F32)))
    return t_u ^ I32_MIN, cge


def _tie_cut_index(count_tie_below, rows, need):
    def body(p, x):
        cand = x | jnp.left_shift(jnp.int32(1), INDEX_BITS - 1 - p)
        return jnp.where(count_tie_below(cand) < need, cand, x)

    return lax.fori_loop(0, INDEX_BITS, body, jnp.zeros((rows, 1), I32))


def _dsa_prompt_kernel(iq_ref, small_ref, aq_ref, ik_ref, k_ref, v_ref, o_ref,
                       keys_sc, cand_sc, t_sc, m_sc, l_sc, acc_sc, *, tq, topk):
    i = pl.program_id(1)
    q0 = i * tq
    nchunk = i + 1
    qpos = q0 + lax.broadcasted_iota(I32, (tq, 1), 0)
    lane_t = lax.broadcasted_iota(I32, (1, tq), 1)
    lane128 = lax.broadcasted_iota(I32, (tq, LANES), 1)
    ncol = tq // LANES

    w = small_ref[:, 0:H_IDX]

    def score_body(c, _):
        ikc = ik_ref[pl.ds(pl.multiple_of(c * tq, tq), tq), :]
        acc = jnp.zeros((tq, tq), F32)
        for h in range(H_IDX):
            s = _dot_nt(iq_ref[:, h * D_IDX:(h + 1) * D_IDX], ikc)
            acc = acc + w[:, h:h + 1] * jnp.maximum(s, 0.0)
        acc = jnp.where(acc == 0.0, 0.0, acc)
        acc = jnp.where(c * tq + lane_t <= qpos, acc, -jnp.inf)
        keys_sc[c] = _sort_key(acc)
        return 0

    lax.fori_loop(0, nchunk, score_body, 0)

    def count_ge(cand):
        cand_sc[...] = jnp.broadcast_to(cand, (tq, LANES))

        def body(c, acc):
            kc = keys_sc[c]
            cb = cand_sc[...]
            for j in range(ncol):
                acc = acc + jnp.where(kc[:, j * LANES:(j + 1) * LANES] >= cb, 1.0, 0.0)
            return acc

        acc = lax.fori_loop(0, nchunk, body, jnp.zeros((tq, LANES), F32))
        return jnp.sum(acc, axis=1, keepdims=True)

    thr, cge = _kth_largest_key(count_ge, tq, topk)
    t_sc[...] = jnp.broadcast_to(thr, (tq, LANES))

    tied = jnp.logical_and(cge > topk, thr > NEGINF_KEY)
    any_tied = jnp.max(jnp.where(tied, 1.0, 0.0)) > 0.0

    @pl.when(any_tied)
    def _():
        need = topk - count_ge(thr + 1)

        def count_tie_below(xc):
            cand_sc[...] = jnp.broadcast_to(xc, (tq, LANES))

            def body(c, acc):
                kc = keys_sc[c]
                tb = t_sc[...]
                xb = cand_sc[...]
                for j in range(ncol):
                    pos = c * tq + j * LANES + lane128
                    hit = jnp.where(kc[:, j * LANES:(j + 1) * LANES] == tb, pos, 2 ** 30) < xb
                    acc = acc + jnp.where(hit, 1.0, 0.0)
                return acc

            acc = lax.fori_loop(0, nchunk, body, jnp.zeros((tq, LANES), F32))
            return jnp.sum(acc, axis=1, keepdims=True)

        cut = _tie_cut_index(count_tie_below, tq, need)
        cand_sc[...] = jnp.broadcast_to(cut, (tq, LANES))

        def drop_body(c, _):
            kc = keys_sc[c]
            tb = t_sc[...]
            xb = cand_sc[...]
            cols = []
            for j in range(ncol):
                kj = kc[:, j * LANES:(j + 1) * LANES]
                pos = c * tq + j * LANES + lane128
                late = jnp.where(kj == tb, pos, -1) > xb
                cols.append(jnp.where(late, kj - 1, kj))
            keys_sc[c] = jnp.concatenate(cols, axis=1)
            return 0

        lax.fori_loop(0, nchunk, drop_body, 0)

    t_sc[...] = jnp.broadcast_to(jnp.maximum(thr, NEGINF_KEY + 1), (tq, LANES))
    m_sc[...] = jnp.full(m_sc.shape, NEG, F32)
    l_sc[...] = jnp.zeros(l_sc.shape, F32)
    acc_sc[...] = jnp.zeros(acc_sc.shape, F32)

    def att_body(c, _):
        start = pl.multiple_of(c * tq, tq)
        kc = k_ref[pl.ds(start, tq), :]
        vc = v_ref[pl.ds(start, tq), :]
        keys = keys_sc[c]
        tb = t_sc[...]
        sel = jnp.concatenate(
            [jnp.where(keys[:, j * LANES:(j + 1) * LANES] >= tb, 0.0, NEG) for j in range(ncol)], axis=1)
        dist = (qpos - (c * tq + lane_t)).astype(F32)
        for hh in range(H_A):
            lg = _dot_nt(aq_ref[:, hh * DH_A:(hh + 1) * DH_A], kc)
            lg = (lg - ALIBI_SLOPES[hh] * dist) + sel
            m_old = m_sc[hh]
            m_new = jnp.maximum(m_old, jnp.max(lg, axis=1, keepdims=True))
            alpha = jnp.exp(m_old - m_new)
            p = jnp.exp(lg - m_new)
            l_sc[hh] = alpha * l_sc[hh] + jnp.sum(p, axis=1, keepdims=True)
            acc_sc[hh] = alpha * acc_sc[hh] + _dot(p.astype(BF16), vc)
            m_sc[hh] = m_new
        return 0

    lax.fori_loop(0, nchunk, att_body, 0)
    for hh in range(H_A):
        o_ref[:, hh * DH_A:(hh + 1) * DH_A] = (acc_sc[hh] / l_sc[hh]).astype(BF16)


def _dsa_prompt(iq, small, aq, ikb, kb, vb, nb, seq, tq):
    n = iq.shape[0]
    nq = seq // tq
    topk = min(TOPK_MAX, seq // 4)

    def tok(w):
        return pl.BlockSpec((tq, w), lambda b, i: (b * nq + i, 0))

    def per_b(w):
        return pl.BlockSpec((seq, w), lambda b, i: (b, 0))

    return pl.pallas_call(
        functools.partial(_dsa_prompt_kernel, tq=tq, topk=topk),
        grid=(nb, nq),
        in_specs=[tok(512), tok(16), tok(512), per_b(D_IDX), per_b(DH_A), per_b(DH_A)],
        out_specs=tok(512),
        out_shape=jax.ShapeDtypeStruct((n, 512), BF16),
        scratch_shapes=[pltpu.VMEM((nq, tq, tq), I32), pltpu.VMEM((tq, LANES), I32),
                        pltpu.VMEM((tq, LANES), I32), pltpu.VMEM((H_A, tq, 1), F32),
                        pltpu.VMEM((H_A, tq, 1), F32), pltpu.VMEM((H_A, tq, DH_A), F32)],
        compiler_params=_cparams(("parallel", "arbitrary")),
        name="dsa_prompt",
    )(iq, small, aq, ikb, kb, vb)


def _dsa_sample_kernel(pt_ref, iq_ref, w_ref, q_ref, iknew_ref, knew_ref, vnew_ref,
                       ckidx_hbm, ck_hbm, cv_hbm, o_ref,
                       ibuf, kbuf, vbuf, sems, keys_sc, lg_sc, *, n_pages, n_new, topk, cw):
    b = pl.program_id(0)
    nb = pl.num_programs(0)
    past = n_pages * PAGE
    total = past + LANES
    slot = b % 2
    rows_q = n_new * H_A

    def page_copies(bb, sl, j):
        pg = pt_ref[bb, j]
        dst = pl.ds(pl.multiple_of(j * PAGE, PAGE), PAGE)
        return (pltpu.make_async_copy(ckidx_hbm.at[pg], ibuf.at[sl, dst, :], sems.at[0, sl]),
                pltpu.make_async_copy(ck_hbm.at[pg], kbuf.at[sl, dst, :], sems.at[1, sl]),
                pltpu.make_async_copy(cv_hbm.at[pg], vbuf.at[sl, dst, :], sems.at[2, sl]))

    def start_all(bb, sl):
        def body(j, _):
            for cp in page_copies(bb, sl, j):
                cp.start()
            return 0
        lax.fori_loop(0, n_pages, body, 0)

    def wait_all(bb, sl):
        def body(j, _):
            for cp in page_copies(bb, sl, j):
                cp.wait()
            return 0
        lax.fori_loop(0, n_pages, body, 0)

    @pl.when(b == 0)
    def _():
        start_all(0, 0)

    @pl.when(b + 1 < nb)
    def _():
        start_all(b + 1, 1 - slot)

    wait_all(b, slot)

    iq = iq_ref[0]
    w = w_ref[0]
    keys_sc[...] = jnp.full(keys_sc.shape, NEGINF_KEY, I32)

    def scores(ik_chunk):
        s = jnp.maximum(_dot_nt(iq, ik_chunk), 0.0) * w
        s = jnp.sum(s.reshape(n_new, H_IDX, s.shape[-1]), axis=1)
        return jnp.where(s == 0.0, 0.0, s)

    for ch in range(past // cw):
        sc = scores(ibuf[slot, ch * cw:(ch + 1) * cw, :].astype(BF16))
        keys_sc[0:n_new, ch * cw:(ch + 1) * cw] = _sort_key(sc)
    sc = scores(iknew_ref[0])
    t_i = lax.broadcasted_iota(I32, (n_new, LANES), 0)
    j_i = lax.broadcasted_iota(I32, (n_new, LANES), 1)
    keys_sc[0:n_new, past:total] = _sort_key(jnp.where(j_i <= t_i, sc, -jnp.inf))

    ncols = total // LANES
    lane8 = lax.broadcasted_iota(I32, (8, LANES), 1)

    def count_ge(cand):
        cb = jnp.broadcast_to(cand, (8, LANES))
        accs = [jnp.zeros((8, LANES), F32) for _ in range(4)]
        for j in range(ncols):
            accs[j % 4] = accs[j % 4] + jnp.where(keys_sc[:, j * LANES:(j + 1) * LANES] >= cb, 1.0, 0.0)
        return jnp.sum((accs[0] + accs[1]) + (accs[2] + accs[3]), axis=1, keepdims=True)

    thr, cge = _kth_largest_key(count_ge, 8, topk)
    tied = jnp.logical_and(cge > topk, thr > NEGINF_KEY)
    any_tied = jnp.max(jnp.where(tied, 1.0, 0.0)) > 0.0

    @pl.when(any_tied)
    def _():
        need = topk - count_ge(thr + 1)
        tb = jnp.broadcast_to(thr, (8, LANES))

        def count_tie_below(xc):
            xb = jnp.broadcast_to(xc, (8, LANES))
            acc = jnp.zeros((8, LANES), F32)
            for j in range(ncols):
                pos = j * LANES + lane8
                hit = jnp.where(keys_sc[:, j * LANES:(j + 1) * LANES] == tb, pos, 2 ** 30) < xb
                acc = acc + jnp.where(hit, 1.0, 0.0)
            return jnp.sum(acc, axis=1, keepdims=True)

        cut = _tie_cut_index(count_tie_below, 8, need)
        xb = jnp.broadcast_to(cut, (8, LANES))
        for j in range(ncols):
            kj = keys_sc[:, j * LANES:(j + 1) * LANES]
            late = jnp.where(kj == tb, j * LANES + lane8, -1) > xb
            keys_sc[:, j * LANES:(j + 1) * LANES] = jnp.where(late, kj - 1, kj)

    teff = jnp.maximum(thr, NEGINF_KEY + 1)
    r_i = lax.broadcasted_iota(I32, (rows_q, 1), 0)
    r_t = r_i // H_A
    r_h = r_i % H_A
    slope = jnp.zeros((rows_q, 1), F32)
    for hh in range(H_A):
        slope = jnp.where(r_h == hh, ALIBI_SLOPES[hh], slope)
    qposf = (past + r_t).astype(F32)
    q = q_ref[0]

    def sel_bias(lo, width):
        out = jnp.full((rows_q, width), NEG, F32)
        for t in range(n_new):
            kt = keys_sc[t:t + 1, lo:lo + width]
            bias_t = jnp.where(kt >= teff[t:t + 1, :], 0.0, NEG)
            out = jnp.where(r_t == t, bias_t, out)
        return out

    def logits(k_chunk, lo, width):
        kposf = (lo + lax.broadcasted_iota(I32, (1, width), 1)).astype(F32)
        sel = sel_bias(lo, width)
        lg = _dot_nt(q, k_chunk) - slope * (qposf - kposf)
        return lg + sel

    for ch in range(past // cw):
        lg_sc[:, ch * cw:(ch + 1) * cw] = logits(kbuf[slot, ch * cw:(ch + 1) * cw, :].astype(BF16), ch * cw, cw)
    lg_sc[:, past:total] = logits(knew_ref[0], past, LANES)

    lg = lg_sc[...]
    m = jnp.max(lg, axis=1, keepdims=True)
    p = jnp.exp(lg - m)
    l = jnp.sum(p, axis=1, keepdims=True)
    pb = p.astype(BF16)
    acc = _dot(pb[:, past:total], vnew_ref[0])
    for ch in range(past // cw):
        acc = acc + _dot(pb[:, ch * cw:(ch + 1) * cw], vbuf[slot, ch * cw:(ch + 1) * cw, :].astype(BF16))
    o_ref[0] = (acc / l).astype(BF16)


def _dsa_sample(page_table, iq32, w32, q16, iknew, knew, vnew, cache_kidx, cache_k, cache_v, n_new):
    db, n_pages = page_table.shape
    past = n_pages * PAGE
    total = past + LANES
    topk = min(TOPK_MAX, (past + n_new) // 4)
    cw = 1024 if past % 1024 == 0 else PAGE
    rows_q = n_new * H_A

    def per_b(a):
        return pl.BlockSpec((1,) + a.shape[1:], lambda b, pt: (b, 0, 0))

    hbm = pl.BlockSpec(memory_space=pl.ANY)
    grid_spec = pltpu.PrefetchScalarGridSpec(
        num_scalar_prefetch=1,
        grid=(db,),
        in_specs=[per_b(iq32), per_b(w32), per_b(q16), per_b(iknew), per_b(knew), per_b(vnew), hbm, hbm, hbm],
        out_specs=pl.BlockSpec((1, rows_q, DH_A), lambda b, pt: (b, 0, 0)),
        scratch_shapes=[pltpu.VMEM((2, past, D_IDX), F32), pltpu.VMEM((2, past, DH_A), F32),
                        pltpu.VMEM((2, past, DH_A), F32), pltpu.SemaphoreType.DMA((3, 2)),
                        pltpu.VMEM((8, total), I32), pltpu.VMEM((rows_q, total), F32)],
    )
    return pl.pallas_call(
        functools.partial(_dsa_sample_kernel, n_pages=n_pages, n_new=n_new, topk=topk, cw=cw),
        grid_spec=grid_spec,
        out_shape=jax.ShapeDtypeStruct((db, rows_q, DH_A), BF16),
        compiler_params=_cparams(("arbitrary",)),
        name="dsa_sample",
    )(page_table, iq32, w32, q16, iknew, knew, vnew, cache_kidx, cache_k, cache_v)


def _outffn_kernel(x_ref, hm_ref, ha_ref, mod_ref, g2_ref, wom_ref, woa_ref, wg_ref, wu_ref, wd_ref, y_ref):
    mix = _dot(hm_ref[...], wom_ref[...]) + _dot(ha_ref[...], woa_ref[...])
    x1 = x_ref[...] + mod_ref[0, 0] * mix
    xn = x1 * lax.rsqrt(jnp.mean(x1 * x1, axis=-1, keepdims=True) + EPS) * g2_ref[...]
    hb = (xn * (1.0 + mod_ref[2, 0]) + mod_ref[1, 0]).astype(BF16)
    g = _dot(hb, wg_ref[...])
    u = _dot(hb, wu_ref[...])
    act = (g * _sigmoid(g) * u).astype(BF16)
    y_ref[...] = x1 + mod_ref[3, 0] * _dot(act, wd_ref[...])


def _outffn(x, hm, ha, mod_out, per_token_mod, rows_per_mod, g2, wom, woa, wg, wu, wd, tm):
    n, d = x.shape
    if per_token_mod:
        mod_spec = pl.BlockSpec((4, 1, tm, d), lambda i: (0, 0, i, 0))
    else:
        tiles = rows_per_mod // tm
        mod_spec = pl.BlockSpec((4, 1, 1, d), lambda i: (0, i // tiles, 0, 0))

    def full(a):
        return pl.BlockSpec(a.shape, lambda i: (0,) * a.ndim, pipeline_mode=pl.Buffered(1))

    def rows(w):
        return pl.BlockSpec((tm, w), lambda i: (i, 0))

    return pl.pallas_call(
        _outffn_kernel,
        grid=(n // tm,),
        in_specs=[rows(d), rows(512), rows(512), mod_spec, full(g2), full(wom), full(woa),
                  full(wg), full(wu), full(wd)],
        out_specs=rows(d),
        out_shape=jax.ShapeDtypeStruct((n, d), F32),
        compiler_params=_cparams(("parallel",)),
        name="outffn",
    )(x, hm, ha, mod_out, g2, wom, woa, wg, wu, wd)


def _pick_tile(n, pref):
    t = pref
    while n % t:
        t //= 2
    return t


def kernel(x_prompt, x_sample, cache_k, cache_v, cache_kidx, state_C, state_n, state_m, page_table,
           c_prompt, c_sample, w_ada, b_ada, g_norm1, w_in, b_igate, b_fgate, mlstm_norm_g,
           q_norm_g, k_norm_g, w_out, g_norm2, w_gate, w_up, w_down):
    bp, seq, d = x_prompt.shape
    db, t_new, _ = x_sample.shape
    n_p, n_s = bp * seq, db * t_new

    o = np.cumsum([0, 512, 512, 512, 512, H_M, H_M, 512, DH_A, DH_A, H_IDX * D_IDX, D_IDX, H_IDX])
    mq, mk, mv, mo, mi, mf, aq, ak, av, iq, ik, iw = [w_in[:, int(o[j]):int(o[j + 1])] for j in range(12)]
    zpad = lambda wdt: jnp.zeros((d, wdt), w_in.dtype)
    w_r = jnp.concatenate([mq, mk, mv, mo, aq, ak, av, iq, ik, zpad(LANES - D_IDX),
                           iw, mi, mf, zpad(LANES - 16)], axis=1).astype(BF16)
    wgt = jnp.concatenate([mi, mf], axis=1).T.astype(BF16)
    bsm = jnp.concatenate([jnp.zeros((H_IDX,), F32), b_igate, b_fgate]).reshape(1, 16)
    brow = jnp.concatenate([b_igate, b_fgate]).reshape(8, 1)
    g1 = g_norm1.reshape(1, d)
    g2 = g_norm2.reshape(1, d)
    qg = q_norm_g.reshape(1, DH_A)
    kg = k_norm_g.reshape(1, DH_A)
    ng = mlstm_norm_g.reshape(1, H_M * DH_M)
    wom = w_out[0:H_M * DH_M].astype(BF16)
    woa = w_out[H_M * DH_M:].astype(BF16)
    wg = w_gate.astype(BF16)
    wu = w_up.astype(BF16)
    wd = w_down.astype(BF16)

    mod = _ada(jnp.concatenate([c_prompt, c_sample], axis=0), w_ada.astype(BF16), b_ada)
    mod_p = mod[:bp].reshape(bp, 6, 1, d).transpose(1, 0, 2, 3)
    mod_s = jnp.repeat(mod[bp:].reshape(db, 6, d), t_new, axis=0)
    mod_s = mod_s.transpose(1, 0, 2).reshape(6, 1, n_s, d)

    tm_p = _pick_tile(seq, 512)
    (mqkv, mo_p, aq_p, k_p, v_p, kidx_p, kb, vb, ikb, iq_p, small_p, grow_p) = _inproj(
        x_prompt.reshape(n_p, d), mod_p[0:2], False, seq, g1, w_r, wgt, bsm, brow, qg, kg, tm_p)

    lc = _pick_tile(seq, 256)
    grow3 = grow_p.reshape(8, n_p // lc, lc).transpose(1, 0, 2)
    hm_p, C_p, n_pst, m_pb = _mlstm(
        mqkv, small_p, grow3, mo_p, ng,
        jnp.zeros((bp, H_M, DH_M, DH_M), F32), jnp.zeros((bp, H_M, DH_M), F32),
        jnp.zeros((bp, H_M, LANES), F32), bp, lc)

    tq = _pick_tile(seq, 256)
    ha_p = _dsa_prompt(iq_p, small_p, aq_p, ikb, kb, vb, bp, seq, tq)

    y_p = _outffn(x_prompt.reshape(n_p, d), hm_p, ha_p, mod_p[2:6], False, seq, g2,
                  wom, woa, wg, wu, wd, tm_p)

    tm_s = _pick_tile(n_s, 128)
    (mqkv_s, mo_s, aq_s, k_s, v_s, kidx_s, kb_s, vb_s, ikb_s, iq_s, small_s, grow_s) = _inproj(
        x_sample.reshape(n_s, d), mod_s[0:2], True, 0, g1, w_r, wgt, bsm, brow, qg, kg, tm_s)

    lp = 16
    pad_tok = lambda a: jnp.pad(a.reshape(db, t_new, a.shape[-1]),
                                ((0, 0), (0, lp - t_new), (0, 0))).reshape(db * lp, a.shape[-1])
    gate_pad = jnp.concatenate([jnp.zeros((H_IDX,), F32), jnp.full((H_M,), -jnp.inf, F32),
                                jnp.zeros((H_M,), F32)])
    small_pad = jnp.concatenate(
        [small_s.reshape(db, t_new, 16), jnp.broadcast_to(gate_pad, (db, lp - t_new, 16))], axis=1
    ).reshape(db * lp, 16)
    grow_pad = jnp.concatenate(
        [grow_s.reshape(8, db, t_new),
         jnp.broadcast_to(gate_pad[H_IDX:].reshape(8, 1, 1), (8, db, lp - t_new))], axis=2
    ).transpose(1, 0, 2)
    hm_s_pad, C_s, n_sst, m_sb = _mlstm(
        pad_tok(mqkv_s), small_pad, grow_pad, pad_tok(mo_s), ng,
        state_C, state_n, jnp.broadcast_to(state_m[:, :, None], (db, H_M, LANES)), db, lp)
    hm_s = hm_s_pad.reshape(db, lp, H_M * DH_M)[:, :t_new].reshape(n_s, H_M * DH_M)

    pad_rows = lambda a: jnp.pad(a.reshape(db, t_new, a.shape[-1]), ((0, 0), (0, LANES - t_new), (0, 0)))
    ha_s = _dsa_sample(
        page_table, iq_s.reshape(db, t_new * H_IDX, D_IDX),
        small_s[:, 0:H_IDX].reshape(db, t_new * H_IDX, 1),
        aq_s.reshape(db, t_new * H_A, DH_A),
        pad_rows(ikb_s), pad_rows(kb_s), pad_rows(vb_s),
        cache_kidx, cache_k, cache_v, t_new).reshape(n_s, H_A * DH_A)

    y_s = _outffn(x_sample.reshape(n_s, d), hm_s, ha_s, mod_s[2:6], True, 0, g2,
                  wom, woa, wg, wu, wd, tm_s)

    return (y_p.reshape(bp, seq, d), y_s.reshape(db, t_new, d),
            k_p.reshape(bp, seq, DH_A), v_p.reshape(bp, seq, DH_A), kidx_p.reshape(bp, seq, D_IDX),
            C_p, n_pst, m_pb[:, :, 0],
            k_s.reshape(db, t_new, DH_A), v_s.reshape(db, t_new, DH_A), kidx_s.reshape(db, t_new, D_IDX),
            C_s, n_sst, m_sb[:, :, 0])
```

```python
import functools

import jax
import jax.numpy as jnp
import numpy as np
from jax import lax
from jax.experimental import pallas as pl
from jax.experimental.pallas import tpu as pltpu

F32 = jnp.float32
BF16 = jnp.bfloat16
I32 = jnp.int32

H_M = 4
DH_M = 128
H_A = 4
DH_A = 128
H_IDX = 8
D_IDX = 64
TOPK_MAX = 256
PAGE = 128
EPS = 1e-6
INDEX_SCALE = D_IDX ** -0.5 * H_IDX ** -0.5
ALIBI_SLOPES = tuple(float(2.0 ** (-8.0 * (h + 1) / H_A)) for h in range(H_A))

LANES = 128
VMEM_LIMIT = 56 * 1024 * 1024
NEG = -1e30
I32_MIN = -2 ** 31
NEGINF_KEY = -2139095041
INDEX_BITS = 14

C_MQ, C_MK, C_MV, C_MO, C_AQ, C_AK, C_AV, C_IQ, C_IK, C_SM, C_END = (
    0, 512, 1024, 1536, 2048, 2560, 2688, 2816, 3328, 3456, 3584)


def _cparams(sem):
    return pltpu.CompilerParams(dimension_semantics=sem, vmem_limit_bytes=VMEM_LIMIT)


def _sigmoid(x):
    return 1.0 / (1.0 + jnp.exp(-x))


def _log_sigmoid(x):
    return jnp.minimum(x, 0.0) - jnp.log1p(jnp.exp(-jnp.abs(x)))


def _dot(a, b):
    return jnp.dot(a, b, preferred_element_type=F32)


def _dot_nt(a, b):
    return lax.dot_general(a, b, (((1,), (1,)), ((), ())), preferred_element_type=F32)


def _dot_tn(a, b):
    return lax.dot_general(a, b, (((0,), (0,)), ((), ())), preferred_element_type=F32)


def _split3(x):
    hi = x.astype(BF16)
    r1 = x - hi.astype(F32)
    mid = r1.astype(BF16)
    lo = (r1 - mid.astype(F32)).astype(BF16)
    return hi, mid, lo


def _sort_key(x):
    b = pltpu.bitcast(x, I32)
    return b ^ ((b >> 31) & 0x7FFFFFFF)


def _ada_kernel(c_ref, w_ref, b_ref, o_ref):
    c = c_ref[...]
    s = c * _sigmoid(c)
    o_ref[...] = _dot(s.astype(BF16), w_ref[...]) + b_ref[...]


def _ada(c, w_bf, b):
    r, d = c.shape
    n = w_bf.shape[1]
    return pl.pallas_call(
        _ada_kernel,
        grid=(n // d,),
        in_specs=[pl.BlockSpec((r, d), lambda j: (0, 0)),
                  pl.BlockSpec((d, d), lambda j: (0, j)),
                  pl.BlockSpec((1, d), lambda j: (0, j))],
        out_specs=pl.BlockSpec((r, d), lambda j: (0, j)),
        out_shape=jax.ShapeDtypeStruct((r, n), F32),
        compiler_params=_cparams(("arbitrary",)),
        name="ada",
    )(c, w_bf, b.reshape(1, n))


def _inproj_kernel(x_ref, mod_ref, g1_ref, w_ref, wgt_ref, bsm_ref, brow_ref, qg_ref, kg_ref,
                   mqkv_ref, mo_ref, aq_ref, k_ref, v_ref, kidx_ref, kb_ref, vb_ref, ikb_ref,
                   iq_ref, small_ref, grow_ref):
    x = x_ref[...]
    xn = x * lax.rsqrt(jnp.mean(x * x, axis=-1, keepdims=True) + EPS) * g1_ref[...]
    h = xn * (1.0 + mod_ref[1, 0]) + mod_ref[0, 0]
    hb = h.astype(BF16)

    def sec(a, b):
        return _dot(hb, w_ref[:, a:b])

    mqkv_ref[:, 0:512] = sec(C_MQ, C_MK).astype(BF16)
    mqkv_ref[:, 512:1024] = (sec(C_MK, C_MV) * (DH_M ** -0.5)).astype(BF16)
    mqkv_ref[:, 1024:1536] = sec(C_MV, C_MO).astype(BF16)
    mo_ref[...] = sec(C_MO, C_AQ)

    aq = sec(C_AQ, C_AK)
    qg = qg_ref[...]
    for hh in range(H_A):
        a = aq[:, hh * DH_A:(hh + 1) * DH_A]
        a = a * lax.rsqrt(jnp.mean(a * a, axis=-1, keepdims=True) + EPS) * qg
        aq_ref[:, hh * DH_A:(hh + 1) * DH_A] = (a * (DH_A ** -0.5)).astype(BF16)

    ak = sec(C_AK, C_AV)
    ak = ak * lax.rsqrt(jnp.mean(ak * ak, axis=-1, keepdims=True) + EPS) * kg_ref[...]
    k_ref[...] = ak
    kb_ref[...] = ak.astype(BF16)
    av = sec(C_AV, C_IQ)
    v_ref[...] = av
    vb_ref[...] = av.astype(BF16)

    iq_ref[...] = sec(C_IQ, C_IK).astype(BF16)
    ik = sec(C_IK, C_SM)[:, 0:D_IDX]
    kidx_ref[...] = ik
    ikb_ref[...] = ik.astype(BF16)

    sm = sec(C_SM, C_END)[:, 0:16] + bsm_ref[...]
    col = lax.broadcasted_iota(I32, sm.shape, 1)
    small_ref[...] = jnp.where(col < H_IDX, sm * INDEX_SCALE,
                               jnp.where(col < H_IDX + H_M, sm, _log_sigmoid(sm)))

    gr = _dot_nt(wgt_ref[...], hb) + brow_ref[...]
    row = lax.broadcasted_iota(I32, gr.shape, 0)
    grow_ref[...] = jnp.where(row < H_M, gr, _log_sigmoid(gr))


def _inproj(x, mod_in, per_token_mod, rows_per_mod, g1, w_r, wgt, bsm, brow, qg, kg, tm):
    n, d = x.shape
    if per_token_mod:
        mod_spec = pl.BlockSpec((2, 1, tm, d), lambda i: (0, 0, i, 0))
    else:
        tiles = rows_per_mod // tm
        mod_spec = pl.BlockSpec((2, 1, 1, d), lambda i: (0, i // tiles, 0, 0))

    def full(a):
        return pl.BlockSpec(a.shape, lambda i: (0,) * a.ndim)

    def rows(w):
        return pl.BlockSpec((tm, w), lambda i: (i, 0))

    outs = [(1536, BF16), (512, F32), (512, BF16), (128, F32), (128, F32), (D_IDX, F32),
            (128, BF16), (128, BF16), (D_IDX, BF16), (512, BF16), (16, F32)]
    out_shape = [jax.ShapeDtypeStruct((n, w), dt) for w, dt in outs]
    out_specs = [rows(w) for w, _ in outs]
    out_shape.append(jax.ShapeDtypeStruct((8, n), F32))
    out_specs.append(pl.BlockSpec((8, tm), lambda i: (0, i)))
    return pl.pallas_call(
        _inproj_kernel,
        grid=(n // tm,),
        in_specs=[rows(d), mod_spec, full(g1), full(w_r), full(wgt), full(bsm), full(brow),
                  full(qg), full(kg)],
        out_specs=out_specs,
        out_shape=out_shape,
        compiler_params=_cparams(("parallel",)),
        name="inproj",
    )(x, mod_in, g1, w_r, wgt, bsm, brow, qg, kg)


def _mlstm_kernel(q_ref, k_ref, v_ref, small_ref, grow_ref, o_ref, ng_ref, c0_ref, n0_ref, m0_ref,
                  h_ref, c_out_ref, n_out_ref, m_out_ref, state_sc, m_sc, *, L, nc):
    c = pl.program_id(1)

    @pl.when(c == 0)
    def _():
        state_sc[...] = jnp.zeros(state_sc.shape, F32)
        for hh in range(H_M):
            state_sc[hh, 0:DH_M, :] = c0_ref[0, hh]
            state_sc[hh, DH_M:DH_M + 1, :] = n0_ref[0, hh:hh + 1, :]
        m_sc[...] = jnp.zeros(m_sc.shape, F32)
        m_sc[0:H_M, :] = m0_ref[0]

    ti = lax.broadcasted_iota(I32, (L, L), 0)
    si = lax.broadcasted_iota(I32, (L, L), 1)
    causal = si <= ti
    tri = jnp.where(causal, 1.0, 0.0).astype(BF16)
    tri_t = jnp.where(ti <= si, 1.0, 0.0).astype(BF16)

    sm = small_ref[...]
    col = lax.broadcasted_iota(I32, sm.shape, 1)
    lf_cols = jnp.where(col >= H_IDX + H_M, sm, 0.0)
    hi, mid, lo = _split3(lf_cols)
    b_cols = _dot(tri, hi) + _dot(tri, mid) + _dot(tri, lo)
    gr = grow_ref[0]
    row = lax.broadcasted_iota(I32, gr.shape, 0)
    lf_rows = jnp.where(row >= H_M, gr, 0.0)
    hi, mid, lo = _split3(lf_rows)
    b_rows = _dot(hi, tri_t) + _dot(mid, tri_t) + _dot(lo, tri_t)

    lane = lax.broadcasted_iota(I32, (L, DH_M), 1)
    ones_col = jnp.where(lane == 0, 1.0, 0.0).astype(BF16)

    for hh in range(H_M):
        q = q_ref[:, hh * DH_M:(hh + 1) * DH_M]
        k = k_ref[:, hh * DH_M:(hh + 1) * DH_M]
        v = v_ref[:, hh * DH_M:(hh + 1) * DH_M]
        ig_c = sm[:, H_IDX + hh:H_IDX + hh + 1]
        b_c = b_cols[:, H_IDX + H_M + hh:H_IDX + H_M + hh + 1]
        ig_r = gr[hh:hh + 1, :]
        b_r = b_rows[H_M + hh:H_M + hh + 1, :]
        m_prev = m_sc[hh:hh + 1, 0:1]

        dmat = jnp.where(causal, b_c - b_r + ig_r, -jnp.inf)
        inter = b_c + m_prev
        m_t = jnp.maximum(inter, jnp.max(dmat, axis=-1, keepdims=True))
        w = jnp.exp(dmat - m_t)
        g = jnp.exp(inter - m_t)
        a = w * _dot_nt(q, k)
        st = state_sc[hh]
        qs = _dot_nt(q, st.astype(BF16))
        num = g * qs[:, 0:DH_M] + _dot(a.astype(BF16), v)
        den = g * qs[:, DH_M:DH_M + 1] + jnp.sum(a, axis=-1, keepdims=True)
        hv = num / jnp.maximum(jnp.abs(den), jnp.exp(-m_t))

        m_new = m_t[L - 1:L, :]
        b_last = b_c[L - 1:L, :]
        wend = jnp.exp(b_last - b_c + ig_c - m_new)
        gend = jnp.exp(b_last + m_prev - m_new)
        kw = (k.astype(F32) * wend).astype(BF16)
        v_aug = jnp.concatenate([v, ones_col], axis=1)
        state_sc[hh] = gend * st + _dot_tn(v_aug, kw)
        m_sc[hh:hh + 1, :] = jnp.broadcast_to(m_new, (1, LANES))

        hn = hv * lax.rsqrt(jnp.mean(hv * hv, axis=-1, keepdims=True) + EPS)
        hn = hn * ng_ref[:, hh * DH_M:(hh + 1) * DH_M]
        hn = hn * _sigmoid(o_ref[:, hh * DH_M:(hh + 1) * DH_M])
        h_ref[:, hh * DH_M:(hh + 1) * DH_M] = hn.astype(BF16)

    @pl.when(c == nc - 1)
    def _():
        for hh in range(H_M):
            c_out_ref[0, hh] = state_sc[hh, 0:DH_M, :]
            n_out_ref[0, hh:hh + 1, :] = state_sc[hh, DH_M:DH_M + 1, :]
        m_out_ref[0] = m_sc[0:H_M, :]


def _mlstm(mqkv, small, grow3, mo, ng, c0, n0, m0b, nb, L):
    n = mqkv.shape[0]
    nc = n // nb // L
    d = H_M * DH_M

    def tok(w, blk=0):
        return pl.BlockSpec((L, w), lambda b, c, blk=blk: (b * nc + c, blk))

    per_b3 = pl.BlockSpec((1, H_M, LANES), lambda b, c: (b, 0, 0))
    per_b4 = pl.BlockSpec((1, H_M, DH_M, DH_M), lambda b, c: (b, 0, 0, 0))
    return pl.pallas_call(
        functools.partial(_mlstm_kernel, L=L, nc=nc),
        grid=(nb, nc),
        in_specs=[tok(d, 0), tok(d, 1), tok(d, 2), tok(16),
                  pl.BlockSpec((1, 8, L), lambda b, c: (b * nc + c, 0, 0)),
                  tok(d), pl.BlockSpec((1, d), lambda b, c: (0, 0)),
                  per_b4, per_b3, per_b3],
        out_specs=[tok(d), per_b4, per_b3, per_b3],
        out_shape=[jax.ShapeDtypeStruct((n, d), BF16),
                   jax.ShapeDtypeStruct((nb, H_M, DH_M, DH_M), F32),
                   jax.ShapeDtypeStruct((nb, H_M, DH_M), F32),
                   jax.ShapeDtypeStruct((nb, H_M, LANES), F32)],
        scratch_shapes=[pltpu.VMEM((H_M, 2 * DH_M, DH_M), F32), pltpu.VMEM((8, LANES), F32)],
        compiler_params=_cparams(("parallel", "arbitrary")),
        name="mlstm",
    )(mqkv, mqkv, mqkv, small, grow3, mo, ng, c0, n0, m0b)


def _kth_largest_key(count_ge, rows, k):
    def body(p, carry):
        t_u, cge = carry
        cand_u = t_u | jnp.left_shift(jnp.int32(1), 31 - p)
        cnt = count_ge(cand_u ^ I32_MIN)
        take = cnt >= k
        return jnp.where(take, cand_u, t_u), jnp.where(take, cnt, cge)

    t_u, cge = lax.fori_loop(0, 32, body, (jnp.zeros((rows, 1), I32), jnp.full((rows, 1), 1e9, F32)))
    return t_u ^ I32_MIN, cge


def _tie_cut_index(count_tie_below, rows, need):
    def body(p, x):
        cand = x | jnp.left_shift(jnp.int32(1), INDEX_BITS - 1 - p)
        return jnp.where(count_tie_below(cand) < need, cand, x)

    return lax.fori_loop(0, INDEX_BITS, body, jnp.zeros((rows, 1), I32))


def _dsa_prompt_kernel(iq_ref, small_ref, aq_ref, qpos_ref, ik_ref, k_ref, v_ref, kpos_ref, o_ref,
                       qidx_sc, wrep_sc, qa_sc, s_sc, keys_sc, lg_sc, p_sc, alpha_sc, m_sc, l_sc, acc_sc,
                       cand_sc, t_sc, tu_sc, cnt_sc, aux_sc, *, tq, tk, topk, rb, hb):
    i = pl.program_id(1)
    q0 = i * tq
    nchunk = (i + 1) * (tq // tk)
    ncol = tk // LANES
    nrb = tq // rb
    lane_hb = lax.broadcasted_iota(I32, (hb, LANES), 1)
    lane_minus_row = lax.broadcasted_iota(I32, (rb, LANES), 1) - lax.broadcasted_iota(I32, (rb, LANES), 0)

    for h in range(H_IDX):
        qidx_sc[h * tq:(h + 1) * tq, :] = iq_ref[:, h * D_IDX:(h + 1) * D_IDX]
        wrep_sc[h] = jnp.broadcast_to(small_ref[:, h:h + 1], (tq, LANES))
    for hh in range(H_A):
        qa_sc[hh * tq:(hh + 1) * tq, 0:DH_A] = aq_ref[:, hh * DH_A:(hh + 1) * DH_A]
        qa_sc[hh * tq:(hh + 1) * tq, DH_A:2 * DH_A] = qpos_ref[0, hh * tq:(hh + 1) * tq, :]

    def score_body(c, _):
        ikc = ik_ref[pl.ds(pl.multiple_of(c * tk, tk), tk), :]
        s_sc[...] = _dot_nt(qidx_sc[...], ikc)
        bound = q0 - c * tk
        for r in range(nrb):
            for j in range(ncol):
                acc = jnp.zeros((rb, LANES), F32)
                for h in range(H_IDX):
                    s = s_sc[h * tq + r * rb:h * tq + (r + 1) * rb, j * LANES:(j + 1) * LANES]
                    acc = acc + wrep_sc[h, r * rb:(r + 1) * rb, :] * jnp.maximum(s, 0.0)
                acc = jnp.where(acc == 0.0, 0.0, acc)
                acc = jnp.where(lane_minus_row <= bound + (r * rb - j * LANES), acc, -jnp.inf)
                keys_sc[c, r * rb:(r + 1) * rb, j * LANES:(j + 1) * LANES] = _sort_key(acc)
        return 0

    lax.fori_loop(0, nchunk, score_body, 0)

    ones_mat = jnp.ones((LANES, LANES), BF16)

    def count_rows(hit_fn):
        cnt_sc[...] = jnp.zeros(cnt_sc.shape, F32)

        def body(c, _):
            for g in range(tq // hb):
                acc = cnt_sc[g * hb:(g + 1) * hb, :]
                for j in range(ncol):
                    kj = keys_sc[c, g * hb:(g + 1) * hb, j * LANES:(j + 1) * LANES]
                    acc = acc + jnp.where(hit_fn(c, g, j, kj), 1.0, 0.0)
                cnt_sc[g * hb:(g + 1) * hb, :] = acc
            return 0

        lax.fori_loop(0, nchunk, body, 0)
        return _dot(cnt_sc[...].astype(BF16), ones_mat)

    def hit_ge(c, g, j, kj):
        return kj >= cand_sc[g * hb:(g + 1) * hb, :]

    tu_sc[...] = jnp.zeros(tu_sc.shape, I32)
    aux_sc[...] = jnp.full(aux_sc.shape, 1e9, F32)

    def search_pass(p, _):
        bit = jnp.left_shift(jnp.int32(1), 31 - p)
        cand_sc[...] = (tu_sc[...] | bit) ^ I32_MIN
        tot = count_rows(hit_ge)
        take = tot >= topk
        tu_sc[...] = jnp.where(take, tu_sc[...] | bit, tu_sc[...])
        aux_sc[...] = jnp.where(take, tot, aux_sc[...])
        return 0

    lax.fori_loop(0, 32, search_pass, 0)
    t_sc[...] = tu_sc[...] ^ I32_MIN

    tied = jnp.logical_and(aux_sc[...] > topk, t_sc[...] > NEGINF_KEY)
    any_tied = jnp.max(jnp.where(tied, 1.0, 0.0)) > 0.0

    @pl.when(any_tied)
    def _():
        cand_sc[...] = t_sc[...] + 1
        aux_sc[...] = topk - count_rows(hit_ge)

        def hit_tie_below(c, g, j, kj):
            pos = c * tk + j * LANES + lane_hb
            return jnp.where(kj == t_sc[g * hb:(g + 1) * hb, :], pos, 2 ** 30) < cand_sc[g * hb:(g + 1) * hb, :]

        tu_sc[...] = jnp.zeros(tu_sc.shape, I32)

        def cut_pass(p, _):
            bit = jnp.left_shift(jnp.int32(1), INDEX_BITS - 1 - p)
            cand_sc[...] = tu_sc[...] | bit
            below = count_rows(hit_tie_below)
            tu_sc[...] = jnp.where(below < aux_sc[...], tu_sc[...] | bit, tu_sc[...])
            return 0

        lax.fori_loop(0, INDEX_BITS, cut_pass, 0)

        def drop_body(c, _):
            for g in range(tq // hb):
                tb = t_sc[g * hb:(g + 1) * hb, :]
                xb = tu_sc[g * hb:(g + 1) * hb, :]
                for j in range(ncol):
                    kj = keys_sc[c, g * hb:(g + 1) * hb, j * LANES:(j + 1) * LANES]
                    pos = c * tk + j * LANES + lane_hb
                    late = jnp.where(kj == tb, pos, -1) > xb
                    keys_sc[c, g * hb:(g + 1) * hb, j * LANES:(j + 1) * LANES] = jnp.where(late, kj - 1, kj)
            return 0

        lax.fori_loop(0, nchunk, drop_body, 0)

    t_sc[...] = jnp.maximum(t_sc[...], NEGINF_KEY + 1)
    m_sc[...] = jnp.full(m_sc.shape, NEG, F32)
    l_sc[...] = jnp.zeros(l_sc.shape, F32)
    acc_sc[...] = jnp.zeros(acc_sc.shape, F32)

    def att_body(c, _):
        start = pl.multiple_of(c * tk, tk)
        kaug = jnp.concatenate([k_ref[pl.ds(start, tk), :], kpos_ref[pl.ds(start, tk), :]], axis=1)
        lg_sc[...] = _dot_nt(qa_sc[...], kaug)
        for r in range(nrb):
            tb = t_sc[r * rb:(r + 1) * rb, :]
            selb = [jnp.where(keys_sc[c, r * rb:(r + 1) * rb, j * LANES:(j + 1) * LANES] >= tb, 0.0, NEG)
                    for j in range(ncol)]
            for hh in range(H_A):
                rows = slice(hh * tq + r * rb, hh * tq + (r + 1) * rb)
                lgs = [lg_sc[rows, j * LANES:(j + 1) * LANES] + selb[j] for j in range(ncol)]
                mx = lgs[0]
                for j in range(1, ncol):
                    mx = jnp.maximum(mx, lgs[j])
                m_old = m_sc[rows, :]
                m_new = jnp.maximum(m_old, jnp.max(mx, axis=1, keepdims=True))
                alpha = jnp.exp(m_old - m_new)
                ps = [jnp.exp(lg - m_new) for lg in lgs]
                psum = ps[0]
                for j in range(1, ncol):
                    psum = psum + ps[j]
                l_sc[rows, :] = alpha * l_sc[rows, :] + psum
                for j in range(ncol):
                    p_sc[rows, j * LANES:(j + 1) * LANES] = ps[j].astype(BF16)
                alpha_sc[rows, :] = alpha
                m_sc[rows, :] = m_new
        acc_sc[...] = alpha_sc[...] * acc_sc[...] + _dot(p_sc[...], v_ref[pl.ds(start, tk), :])
        return 0

    lax.fori_loop(0, nchunk, att_body, 0)
    for hh in range(H_A):
        rows = slice(hh * tq, (hh + 1) * tq)
        l = jnp.sum(l_sc[rows, :], axis=1, keepdims=True)
        o_ref[:, hh * DH_A:(hh + 1) * DH_A] = (acc_sc[rows, :] / l).astype(BF16)


def _alibi_tables(seq, tq):
    pos = np.arange(seq)
    hi, lo = (pos // 64).astype(np.float32), (pos % 64).astype(np.float32)
    ktab = np.zeros((seq, DH_A), np.float32)
    ktab[:, 0], ktab[:, 1], ktab[:, 2], ktab[:, 3] = 64.0 * hi, lo, 1.0, 1.0
    nq = seq // tq
    qtab = np.zeros((nq, H_A, tq, DH_A), np.float32)
    for hh, slope in enumerate(ALIBI_SLOPES):
        qtab[:, hh, :, 0] = slope
        qtab[:, hh, :, 1] = slope
        qtab[:, hh, :, 2] = (-slope * 64.0 * hi).reshape(nq, tq)
        qtab[:, hh, :, 3] = (-slope * lo).reshape(nq, tq)
    return jnp.asarray(qtab.reshape(nq, H_A * tq, DH_A), BF16), jnp.asarray(ktab, BF16)


def _dsa_prompt(iq, small, aq, ikb, kb, vb, nb, seq, tq):
    n = iq.shape[0]
    nq = seq // tq
    topk = min(TOPK_MAX, seq // 4)
    tk = min(256, tq)
    rb = min(64, tq)
    hb = min(128, tq)
    assert seq // LANES <= 256
    qtab, ktab = _alibi_tables(seq, tq)

    def tok(w):
        return pl.BlockSpec((tq, w), lambda b, i: (b * nq + i, 0))

    def per_b(w):
        return pl.BlockSpec((seq, w), lambda b, i: (b, 0))

    return pl.pallas_call(
        functools.partial(_dsa_prompt_kernel, tq=tq, tk=tk, topk=topk, rb=rb, hb=hb),
        grid=(nb, nq),
        in_specs=[tok(512), tok(16), tok(512),
                  pl.BlockSpec((1, H_A * tq, DH_A), lambda b, i: (i, 0, 0)),
                  per_b(D_IDX), per_b(DH_A), per_b(DH_A),
                  pl.BlockSpec((seq, DH_A), lambda b, i: (0, 0))],
        out_specs=tok(512),
        out_shape=jax.ShapeDtypeStruct((n, 512), BF16),
        scratch_shapes=[pltpu.VMEM((H_IDX * tq, D_IDX), BF16),
                        pltpu.VMEM((H_IDX, tq, LANES), F32),
                        pltpu.VMEM((H_A * tq, 2 * DH_A), BF16),
                        pltpu.VMEM((H_IDX * tq, tk), F32),
                        pltpu.VMEM((seq // tk, tq, tk), I32),
                        pltpu.VMEM((H_A * tq, tk), F32),
                        pltpu.VMEM((H_A * tq, tk), BF16),
                        pltpu.VMEM((H_A * tq, LANES), F32),
                        pltpu.VMEM((H_A * tq, LANES), F32),
                        pltpu.VMEM((H_A * tq, LANES), F32),
                        pltpu.VMEM((H_A * tq, DH_A), F32),
                        pltpu.VMEM((tq, LANES), I32),
                        pltpu.VMEM((tq, LANES), I32),
                        pltpu.VMEM((tq, LANES), I32),
                        pltpu.VMEM((tq, LANES), F32),
                        pltpu.VMEM((tq, LANES), F32)],
        compiler_params=_cparams(("parallel", "arbitrary")),
        name="dsa_prompt",
    )(iq, small, aq, qtab, ikb, kb, vb, ktab)


def _dsa_sample_kernel(pt_ref, iq_ref, w_ref, q_ref, iknew_ref, knew_ref, vnew_ref,
                       ckidx_hbm, ck_hbm, cv_hbm, o_ref,
                       ibuf, kbuf, vbuf, sems, keys_sc, lg_sc, *, n_pages, n_new, topk, cw):
    b = pl.program_id(0)
    nb = pl.num_programs(0)
    past = n_pages * PAGE
    total = past + LANES
    slot = b % 2
    rows_q = n_new * H_A

    def page_copies(bb, sl, j):
        pg = pt_ref[bb, j]
        dst = pl.ds(pl.multiple_of(j * PAGE, PAGE), PAGE)
        return (pltpu.make_async_copy(ckidx_hbm.at[pg], ibuf.at[sl, dst, :], sems.at[0, sl]),
                pltpu.make_async_copy(ck_hbm.at[pg], kbuf.at[sl, dst, :], sems.at[1, sl]),
                pltpu.make_async_copy(cv_hbm.at[pg], vbuf.at[sl, dst, :], sems.at[2, sl]))

    def start_all(bb, sl):
        def body(j, _):
            for cp in page_copies(bb, sl, j):
                cp.start()
            return 0
        lax.fori_loop(0, n_pages, body, 0)

    def wait_all(bb, sl):
        def body(j, _):
            for cp in page_copies(bb, sl, j):
                cp.wait()
            return 0
        lax.fori_loop(0, n_pages, body, 0)

    @pl.when(b == 0)
    def _():
        start_all(0, 0)

    @pl.when(b + 1 < nb)
    def _():
        start_all(b + 1, 1 - slot)

    wait_all(b, slot)

    iq = iq_ref[0]
    w = w_ref[0]
    keys_sc[...] = jnp.full(keys_sc.shape, NEGINF_KEY, I32)

    def scores(ik_chunk):
        s = jnp.maximum(_dot_nt(iq, ik_chunk), 0.0) * w
        s = jnp.sum(s.reshape(n_new, H_IDX, s.shape[-1]), axis=1)
        return jnp.where(s == 0.0, 0.0, s)

    for ch in range(past // cw):
        sc = scores(ibuf[slot, ch * cw:(ch + 1) * cw, :].astype(BF16))
        keys_sc[0:n_new, ch * cw:(ch + 1) * cw] = _sort_key(sc)
    sc = scores(iknew_ref[0])
    t_i = lax.broadcasted_iota(I32, (n_new, LANES), 0)
    j_i = lax.broadcasted_iota(I32, (n_new, LANES), 1)
    keys_sc[0:n_new, past:total] = _sort_key(jnp.where(j_i <= t_i, sc, -jnp.inf))

    ncols = total // LANES
    lane8 = lax.broadcasted_iota(I32, (8, LANES), 1)

    def count_ge(cand):
        cb = jnp.broadcast_to(cand, (8, LANES))
        accs = [jnp.zeros((8, LANES), F32) for _ in range(4)]
        for j in range(ncols):
            accs[j % 4] = accs[j % 4] + jnp.where(keys_sc[:, j * LANES:(j + 1) * LANES] >= cb, 1.0, 0.0)
        return jnp.sum((accs[0] + accs[1]) + (accs[2] + accs[3]), axis=1, keepdims=True)

    thr, cge = _kth_largest_key(count_ge, 8, topk)
    tied = jnp.logical_and(cge > topk, thr > NEGINF_KEY)
    any_tied = jnp.max(jnp.where(tied, 1.0, 0.0)) > 0.0

    @pl.when(any_tied)
    def _():
        need = topk - count_ge(thr + 1)
        tb = jnp.broadcast_to(thr, (8, LANES))

        def count_tie_below(xc):
            xb = jnp.broadcast_to(xc, (8, LANES))
            acc = jnp.zeros((8, LANES), F32)
            for j in range(ncols):
                pos = j * LANES + lane8
                hit = jnp.where(keys_sc[:, j * LANES:(j + 1) * LANES] == tb, pos, 2 ** 30) < xb
                acc = acc + jnp.where(hit, 1.0, 0.0)
            return jnp.sum(acc, axis=1, keepdims=True)

        cut = _tie_cut_index(count_tie_below, 8, need)
        xb = jnp.broadcast_to(cut, (8, LANES))
        for j in range(ncols):
            kj = keys_sc[:, j * LANES:(j + 1) * LANES]
            late = jnp.where(kj == tb, j * LANES + lane8, -1) > xb
            keys_sc[:, j * LANES:(j + 1) * LANES] = jnp.where(late, kj - 1, kj)

    teff = jnp.maximum(thr, NEGINF_KEY + 1)
    r_i = lax.broadcasted_iota(I32, (rows_q, 1), 0)
    r_t = r_i // H_A
    r_h = r_i % H_A
    slope = jnp.zeros((rows_q, 1), F32)
    for hh in range(H_A):
        slope = jnp.where(r_h == hh, ALIBI_SLOPES[hh], slope)
    qposf = (past + r_t).astype(F32)
    q = q_ref[0]

    def sel_bias(lo, width):
        out = jnp.full((rows_q, width), NEG, F32)
        for t in range(n_new):
            kt = keys_sc[t:t + 1, lo:lo + width]
            bias_t = jnp.where(kt >= teff[t:t + 1, :], 0.0, NEG)
            out = jnp.where(r_t == t, bias_t, out)
        return out

    def logits(k_chunk, lo, width):
        kposf = (lo + lax.broadcasted_iota(I32, (1, width), 1)).astype(F32)
        sel = sel_bias(lo, width)
        lg = _dot_nt(q, k_chunk) - slope * (qposf - kposf)
        return lg + sel

    for ch in range(past // cw):
        lg_sc[:, ch * cw:(ch + 1) * cw] = logits(kbuf[slot, ch * cw:(ch + 1) * cw, :].astype(BF16), ch * cw, cw)
    lg_sc[:, past:total] = logits(knew_ref[0], past, LANES)

    lg = lg_sc[...]
    m = jnp.max(lg, axis=1, keepdims=True)
    p = jnp.exp(lg - m)
    l = jnp.sum(p, axis=1, keepdims=True)
    pb = p.astype(BF16)
    acc = _dot(pb[:, past:total], vnew_ref[0])
    for ch in range(past // cw):
        acc = acc + _dot(pb[:, ch * cw:(ch + 1) * cw], vbuf[slot, ch * cw:(ch + 1) * cw, :].astype(BF16))
    o_ref[0] = (acc / l).astype(BF16)


def _dsa_sample(page_table, iq32, w32, q16, iknew, knew, vnew, cache_kidx, cache_k, cache_v, n_new):
    db, n_pages = page_table.shape
    past = n_pages * PAGE
    total = past + LANES
    topk = min(TOPK_MAX, (past + n_new) // 4)
    cw = 1024 if past % 1024 == 0 else PAGE
    rows_q = n_new * H_A

    def per_b(a):
        return pl.BlockSpec((1,) + a.shape[1:], lambda b, pt: (b, 0, 0))

    hbm = pl.BlockSpec(memory_space=pl.ANY)
    grid_spec = pltpu.PrefetchScalarGridSpec(
        num_scalar_prefetch=1,
        grid=(db,),
        in_specs=[per_b(iq32), per_b(w32), per_b(q16), per_b(iknew), per_b(knew), per_b(vnew), hbm, hbm, hbm],
        out_specs=pl.BlockSpec((1, rows_q, DH_A), lambda b, pt: (b, 0, 0)),
        scratch_shapes=[pltpu.VMEM((2, past, D_IDX), F32), pltpu.VMEM((2, past, DH_A), F32),
                        pltpu.VMEM((2, past, DH_A), F32), pltpu.SemaphoreType.DMA((3, 2)),
                        pltpu.VMEM((8, total), I32), pltpu.VMEM((rows_q, total), F32)],
    )
    return pl.pallas_call(
        functools.partial(_dsa_sample_kernel, n_pages=n_pages, n_new=n_new, topk=topk, cw=cw),
        grid_spec=grid_spec,
        out_shape=jax.ShapeDtypeStruct((db, rows_q, DH_A), BF16),
        compiler_params=_cparams(("arbitrary",)),
        name="dsa_sample",
    )(page_table, iq32, w32, q16, iknew, knew, vnew, cache_kidx, cache_k, cache_v)


def _outffn_kernel(x_ref, hm_ref, ha_ref, mod_ref, g2_ref, wom_ref, woa_ref, wg_ref, wu_ref, wd_ref, y_ref):
    mix = _dot(hm_ref[...], wom_ref[...]) + _dot(ha_ref[...], woa_ref[...])
    x1 = x_ref[...] + mod_ref[0, 0] * mix
    xn = x1 * lax.rsqrt(jnp.mean(x1 * x1, axis=-1, keepdims=True) + EPS) * g2_ref[...]
    hb = (xn * (1.0 + mod_ref[2, 0]) + mod_ref[1, 0]).astype(BF16)
    g = _dot(hb, wg_ref[...])
    u = _dot(hb, wu_ref[...])
    act = (g * _sigmoid(g) * u).astype(BF16)
    y_ref[...] = x1 + mod_ref[3, 0] * _dot(act, wd_ref[...])


def _outffn(x, hm, ha, mod_out, per_token_mod, rows_per_mod, g2, wom, woa, wg, wu, wd, tm):
    n, d = x.shape
    if per_token_mod:
        mod_spec = pl.BlockSpec((4, 1, tm, d), lambda i: (0, 0, i, 0))
    else:
        tiles = rows_per_mod // tm
        mod_spec = pl.BlockSpec((4, 1, 1, d), lambda i: (0, i // tiles, 0, 0))

    def full(a):
        return pl.BlockSpec(a.shape, lambda i: (0,) * a.ndim, pipeline_mode=pl.Buffered(1))

    def rows(w):
        return pl.BlockSpec((tm, w), lambda i: (i, 0))

    return pl.pallas_call(
        _outffn_kernel,
        grid=(n // tm,),
        in_specs=[rows(d), rows(512), rows(512), mod_spec, full(g2), full(wom), full(woa),
                  full(wg), full(wu), full(wd)],
        out_specs=rows(d),
        out_shape=jax.ShapeDtypeStruct((n, d), F32),
        compiler_params=_cparams(("parallel",)),
        name="outffn",
    )(x, hm, ha, mod_out, g2, wom, woa, wg, wu, wd)


def _pick_tile(n, pref):
    t = pref
    while n % t:
        t //= 2
    return t


def kernel(x_prompt, x_sample, cache_k, cache_v, cache_kidx, state_C, state_n, state_m, page_table,
           c_prompt, c_sample, w_ada, b_ada, g_norm1, w_in, b_igate, b_fgate, mlstm_norm_g,
           q_norm_g, k_norm_g, w_out, g_norm2, w_gate, w_up, w_down):
    bp, seq, d = x_prompt.shape
    db, t_new, _ = x_sample.shape
    n_p, n_s = bp * seq, db * t_new

    o = np.cumsum([0, 512, 512, 512, 512, H_M, H_M, 512, DH_A, DH_A, H_IDX * D_IDX, D_IDX, H_IDX])
    mq, mk, mv, mo, mi, mf, aq, ak, av, iq, ik, iw = [w_in[:, int(o[j]):int(o[j + 1])] for j in range(12)]
    zpad = lambda wdt: jnp.zeros((d, wdt), w_in.dtype)
    w_r = jnp.concatenate([mq, mk, mv, mo, aq, ak, av, iq, ik, zpad(LANES - D_IDX),
                           iw, mi, mf, zpad(LANES - 16)], axis=1).astype(BF16)
    wgt = jnp.concatenate([mi, mf], axis=1).T.astype(BF16)
    bsm = jnp.concatenate([jnp.zeros((H_IDX,), F32), b_igate, b_fgate]).reshape(1, 16)
    brow = jnp.concatenate([b_igate, b_fgate]).reshape(8, 1)
    g1 = g_norm1.reshape(1, d)
    g2 = g_norm2.reshape(1, d)
    qg = q_norm_g.reshape(1, DH_A)
    kg = k_norm_g.reshape(1, DH_A)
    ng = mlstm_norm_g.reshape(1, H_M * DH_M)
    wom = w_out[0:H_M * DH_M].astype(BF16)
    woa = w_out[H_M * DH_M:].astype(BF16)
    wg = w_gate.astype(BF16)
    wu = w_up.astype(BF16)
    wd = w_down.astype(BF16)

    mod = _ada(jnp.concatenate([c_prompt, c_sample], axis=0), w_ada.astype(BF16), b_ada)
    mod_p = mod[:bp].reshape(bp, 6, 1, d).transpose(1, 0, 2, 3)
    mod_s = jnp.repeat(mod[bp:].reshape(db, 6, d), t_new, axis=0)
    mod_s = mod_s.transpose(1, 0, 2).reshape(6, 1, n_s, d)

    tm_p = _pick_tile(seq, 512)
    (mqkv, mo_p, aq_p, k_p, v_p, kidx_p, kb, vb, ikb, iq_p, small_p, grow_p) = _inproj(
        x_prompt.reshape(n_p, d), mod_p[0:2], False, seq, g1, w_r, wgt, bsm, brow, qg, kg, tm_p)

    lc = _pick_tile(seq, 256)
    grow3 = grow_p.reshape(8, n_p // lc, lc).transpose(1, 0, 2)
    hm_p, C_p, n_pst, m_pb = _mlstm(
        mqkv, small_p, grow3, mo_p, ng,
        jnp.zeros((bp, H_M, DH_M, DH_M), F32), jnp.zeros((bp, H_M, DH_M), F32),
        jnp.zeros((bp, H_M, LANES), F32), bp, lc)

    tq = _pick_tile(seq, 512)
    ha_p = _dsa_prompt(iq_p, small_p, aq_p, ikb, kb, vb, bp, seq, tq)

    y_p = _outffn(x_prompt.reshape(n_p, d), hm_p, ha_p, mod_p[2:6], False, seq, g2,
                  wom, woa, wg, wu, wd, tm_p)

    tm_s = _pick_tile(n_s, 128)
    (mqkv_s, mo_s, aq_s, k_s, v_s, kidx_s, kb_s, vb_s, ikb_s, iq_s, small_s, grow_s) = _inproj(
        x_sample.reshape(n_s, d), mod_s[0:2], True, 0, g1, w_r, wgt, bsm, brow, qg, kg, tm_s)

    lp = 16
    pad_tok = lambda a: jnp.pad(a.reshape(db, t_new, a.shape[-1]),
                                ((0, 0), (0, lp - t_new), (0, 0))).reshape(db * lp, a.shape[-1])
    gate_pad = jnp.concatenate([jnp.zeros((H_IDX,), F32), jnp.full((H_M,), -jnp.inf, F32),
                                jnp.zeros((H_M,), F32)])
    small_pad = jnp.concatenate(
        [small_s.reshape(db, t_new, 16), jnp.broadcast_to(gate_pad, (db, lp - t_new, 16))], axis=1
    ).reshape(db * lp, 16)
    grow_pad = jnp.concatenate(
        [grow_s.reshape(8, db, t_new),
         jnp.broadcast_to(gate_pad[H_IDX:].reshape(8, 1, 1), (8, db, lp - t_new))], axis=2
    ).transpose(1, 0, 2)
    hm_s_pad, C_s, n_sst, m_sb = _mlstm(
        pad_tok(mqkv_s), small_pad, grow_pad, pad_tok(mo_s), ng,
        state_C, state_n, jnp.broadcast_to(state_m[:, :, None], (db, H_M, LANES)), db, lp)
    hm_s = hm_s_pad.reshape(db, lp, H_M * DH_M)[:, :t_new].reshape(n_s, H_M * DH_M)

    pad_rows = lambda a: jnp.pad(a.reshape(db, t_new, a.shape[-1]), ((0, 0), (0, LANES - t_new), (0, 0)))
    ha_s = _dsa_sample(
        page_table, iq_s.reshape(db, t_new * H_IDX, D_IDX),
        small_s[:, 0:H_IDX].reshape(db, t_new * H_IDX, 1),
        aq_s.reshape(db, t_new * H_A, DH_A),
        pad_rows(ikb_s), pad_rows(kb_s), pad_rows(vb_s),
        cache_kidx, cache_k, cache_v, t_new).reshape(n_s, H_A * DH_A)

    y_s = _outffn(x_sample.reshape(n_s, d), hm_s, ha_s, mod_s[2:6], True, 0, g2,
                  wom, woa, wg, wu, wd, tm_s)

    return (y_p.reshape(bp, seq, d), y_s.reshape(db, t_new, d),
            k_p.reshape(bp, seq, DH_A), v_p.reshape(bp, seq, DH_A), kidx_p.reshape(bp, seq, D_IDX),
            C_p, n_pst, m_pb[:, :, 0],
            k_s.reshape(db, t_new, DH_A), v_s.reshape(db, t_new, DH_A), kidx_s.reshape(db, t_new, D_IDX),
            C_s, n_sst, m_sb[:, :, 0])
```

```python
import functools

import jax
import jax.numpy as jnp
import numpy as np
from jax import lax
from jax.experimental import pallas as pl
from jax.experimental.pallas import tpu as pltpu

F32 = jnp.float32
BF16 = jnp.bfloat16
I32 = jnp.int32
I16 = jnp.int16

H_M = 4
DH_M = 128
H_A = 4
DH_A = 128
H_IDX = 8
D_IDX = 64
TOPK_MAX = 256
PAGE = 128
EPS = 1e-6
INDEX_SCALE = D_IDX ** -0.5 * H_IDX ** -0.5
ALIBI_SLOPES = tuple(float(2.0 ** (-8.0 * (h + 1) / H_A)) for h in range(H_A))

LANES = 128
VMEM_LIMIT = 56 * 1024 * 1024
NEG = -1e30
I32_MIN = -2 ** 31
NEGINF_KEY = -2139095041
INDEX_BITS = 14

C_MQ, C_MK, C_MV, C_MO, C_AQ, C_AK, C_AV, C_IQ, C_IK, C_SM, C_END = (
    0, 512, 1024, 1536, 2048, 2560, 2688, 2816, 3328, 3456, 3584)


def _cparams(sem):
    return pltpu.CompilerParams(dimension_semantics=sem, vmem_limit_bytes=VMEM_LIMIT)


def _sigmoid(x):
    return 1.0 / (1.0 + jnp.exp(-x))


def _log_sigmoid(x):
    return jnp.minimum(x, 0.0) - jnp.log1p(jnp.exp(-jnp.abs(x)))


def _dot(a, b):
    return jnp.dot(a, b, preferred_element_type=F32)


def _dot_nt(a, b):
    return lax.dot_general(a, b, (((1,), (1,)), ((), ())), preferred_element_type=F32)


def _dot_tn(a, b):
    return lax.dot_general(a, b, (((0,), (0,)), ((), ())), preferred_element_type=F32)


def _split3(x):
    hi = x.astype(BF16)
    r1 = x - hi.astype(F32)
    mid = r1.astype(BF16)
    lo = (r1 - mid.astype(F32)).astype(BF16)
    return hi, mid, lo


def _sort_key(x):
    b = pltpu.bitcast(x, I32)
    return b ^ ((b >> 31) & 0x7FFFFFFF)


def _ada_kernel(c_ref, w_ref, b_ref, o_ref):
    c = c_ref[...]
    s = c * _sigmoid(c)
    o_ref[...] = _dot(s.astype(BF16), w_ref[...]) + b_ref[...]


def _ada(c, w_bf, b):
    r, d = c.shape
    n = w_bf.shape[1]
    return pl.pallas_call(
        _ada_kernel,
        grid=(n // d,),
        in_specs=[pl.BlockSpec((r, d), lambda j: (0, 0)),
                  pl.BlockSpec((d, d), lambda j: (0, j)),
                  pl.BlockSpec((1, d), lambda j: (0, j))],
        out_specs=pl.BlockSpec((r, d), lambda j: (0, j)),
        out_shape=jax.ShapeDtypeStruct((r, n), F32),
        compiler_params=_cparams(("arbitrary",)),
        name="ada",
    )(c, w_bf, b.reshape(1, n))


def _inproj_kernel(x_ref, mod_ref, g1_ref, w_ref, wgt_ref, bsm_ref, brow_ref, qg_ref, kg_ref,
                   mqkv_ref, mo_ref, aq_ref, k_ref, v_ref, kidx_ref, kb_ref, vb_ref, ikb_ref,
                   iq_ref, small_ref, grow_ref):
    x = x_ref[...]
    xn = x * lax.rsqrt(jnp.mean(x * x, axis=-1, keepdims=True) + EPS) * g1_ref[...]
    h = xn * (1.0 + mod_ref[1, 0]) + mod_ref[0, 0]
    hb = h.astype(BF16)

    def sec(a, b):
        return _dot(hb, w_ref[:, a:b])

    mqkv_ref[:, 0:512] = sec(C_MQ, C_MK).astype(BF16)
    mqkv_ref[:, 512:1024] = (sec(C_MK, C_MV) * (DH_M ** -0.5)).astype(BF16)
    mqkv_ref[:, 1024:1536] = sec(C_MV, C_MO).astype(BF16)
    mo_ref[...] = sec(C_MO, C_AQ)

    aq = sec(C_AQ, C_AK)
    qg = qg_ref[...]
    for hh in range(H_A):
        a = aq[:, hh * DH_A:(hh + 1) * DH_A]
        a = a * lax.rsqrt(jnp.mean(a * a, axis=-1, keepdims=True) + EPS) * qg
        aq_ref[:, hh * DH_A:(hh + 1) * DH_A] = (a * (DH_A ** -0.5)).astype(BF16)

    ak = sec(C_AK, C_AV)
    ak = ak * lax.rsqrt(jnp.mean(ak * ak, axis=-1, keepdims=True) + EPS) * kg_ref[...]
    k_ref[...] = ak
    kb_ref[...] = ak.astype(BF16)
    av = sec(C_AV, C_IQ)
    v_ref[...] = av
    vb_ref[...] = av.astype(BF16)

    iq_ref[...] = sec(C_IQ, C_IK).astype(BF16)
    ik = sec(C_IK, C_SM)[:, 0:D_IDX]
    kidx_ref[...] = ik
    ikb_ref[...] = ik.astype(BF16)

    sm = sec(C_SM, C_END)[:, 0:16] + bsm_ref[...]
    col = lax.broadcasted_iota(I32, sm.shape, 1)
    small_ref[...] = jnp.where(col < H_IDX, sm * INDEX_SCALE,
                               jnp.where(col < H_IDX + H_M, sm, _log_sigmoid(sm)))

    gr = _dot_nt(wgt_ref[...], hb) + brow_ref[...]
    row = lax.broadcasted_iota(I32, gr.shape, 0)
    grow_ref[...] = jnp.where(row < H_M, gr, _log_sigmoid(gr))


def _inproj(x, mod_in, per_token_mod, rows_per_mod, g1, w_r, wgt, bsm, brow, qg, kg, tm):
    n, d = x.shape
    if per_token_mod:
        mod_spec = pl.BlockSpec((2, 1, tm, d), lambda i: (0, 0, i, 0))
    else:
        tiles = rows_per_mod // tm
        mod_spec = pl.BlockSpec((2, 1, 1, d), lambda i: (0, i // tiles, 0, 0))

    def full(a):
        return pl.BlockSpec(a.shape, lambda i: (0,) * a.ndim)

    def rows(w):
        return pl.BlockSpec((tm, w), lambda i: (i, 0))

    outs = [(1536, BF16), (512, F32), (512, BF16), (128, F32), (128, F32), (D_IDX, F32),
            (128, BF16), (128, BF16), (D_IDX, BF16), (512, BF16), (16, F32)]
    out_shape = [jax.ShapeDtypeStruct((n, w), dt) for w, dt in outs]
    out_specs = [rows(w) for w, _ in outs]
    out_shape.append(jax.ShapeDtypeStruct((8, n), F32))
    out_specs.append(pl.BlockSpec((8, tm), lambda i: (0, i)))
    return pl.pallas_call(
        _inproj_kernel,
        grid=(n // tm,),
        in_specs=[rows(d), mod_spec, full(g1), full(w_r), full(wgt), full(bsm), full(brow),
                  full(qg), full(kg)],
        out_specs=out_specs,
        out_shape=out_shape,
        compiler_params=_cparams(("parallel",)),
        name="inproj",
    )(x, mod_in, g1, w_r, wgt, bsm, brow, qg, kg)


def _mlstm_kernel(q_ref, k_ref, v_ref, small_ref, grow_ref, o_ref, ng_ref, c0_ref, n0_ref, m0_ref,
                  h_ref, c_out_ref, n_out_ref, m_out_ref, state_sc, m_sc, *, L, nc):
    c = pl.program_id(1)

    @pl.when(c == 0)
    def _():
        state_sc[...] = jnp.zeros(state_sc.shape, F32)
        for hh in range(H_M):
            state_sc[hh, 0:DH_M, :] = c0_ref[0, hh]
            state_sc[hh, DH_M:DH_M + 1, :] = n0_ref[0, hh:hh + 1, :]
        m_sc[...] = jnp.zeros(m_sc.shape, F32)
        m_sc[0:H_M, :] = m0_ref[0]

    ti = lax.broadcasted_iota(I32, (L, L), 0)
    si = lax.broadcasted_iota(I32, (L, L), 1)
    causal = si <= ti
    tri = jnp.where(causal, 1.0, 0.0).astype(BF16)
    tri_t = jnp.where(ti <= si, 1.0, 0.0).astype(BF16)

    sm = small_ref[...]
    col = lax.broadcasted_iota(I32, sm.shape, 1)
    lf_cols = jnp.where(col >= H_IDX + H_M, sm, 0.0)
    hi, mid, lo = _split3(lf_cols)
    b_cols = _dot(tri, hi) + _dot(tri, mid) + _dot(tri, lo)
    gr = grow_ref[0]
    row = lax.broadcasted_iota(I32, gr.shape, 0)
    lf_rows = jnp.where(row >= H_M, gr, 0.0)
    hi, mid, lo = _split3(lf_rows)
    b_rows = _dot(hi, tri_t) + _dot(mid, tri_t) + _dot(lo, tri_t)

    lane = lax.broadcasted_iota(I32, (L, DH_M), 1)
    ones_col = jnp.where(lane == 0, 1.0, 0.0).astype(BF16)

    for hh in range(H_M):
        q = q_ref[:, hh * DH_M:(hh + 1) * DH_M]
        k = k_ref[:, hh * DH_M:(hh + 1) * DH_M]
        v = v_ref[:, hh * DH_M:(hh + 1) * DH_M]
        ig_c = sm[:, H_IDX + hh:H_IDX + hh + 1]
        b_c = b_cols[:, H_IDX + H_M + hh:H_IDX + H_M + hh + 1]
        ig_r = gr[hh:hh + 1, :]
        b_r = b_rows[H_M + hh:H_M + hh + 1, :]
        m_prev = m_sc[hh:hh + 1, 0:1]

        dmat = jnp.where(causal, b_c - b_r + ig_r, -jnp.inf)
        inter = b_c + m_prev
        m_t = jnp.maximum(inter, jnp.max(dmat, axis=-1, keepdims=True))
        w = jnp.exp(dmat - m_t)
        g = jnp.exp(inter - m_t)
        a = w * _dot_nt(q, k)
        st = state_sc[hh]
        qs = _dot_nt(q, st.astype(BF16))
        num = g * qs[:, 0:DH_M] + _dot(a.astype(BF16), v)
        den = g * qs[:, DH_M:DH_M + 1] + jnp.sum(a, axis=-1, keepdims=True)
        hv = num / jnp.maximum(jnp.abs(den), jnp.exp(-m_t))

        m_new = m_t[L - 1:L, :]
        b_last = b_c[L - 1:L, :]
        wend = jnp.exp(b_last - b_c + ig_c - m_new)
        gend = jnp.exp(b_last + m_prev - m_new)
        kw = (k.astype(F32) * wend).astype(BF16)
        v_aug = jnp.concatenate([v, ones_col], axis=1)
        state_sc[hh] = gend * st + _dot_tn(v_aug, kw)
        m_sc[hh:hh + 1, :] = jnp.broadcast_to(m_new, (1, LANES))

        hn = hv * lax.rsqrt(jnp.mean(hv * hv, axis=-1, keepdims=True) + EPS)
        hn = hn * ng_ref[:, hh * DH_M:(hh + 1) * DH_M]
        hn = hn * _sigmoid(o_ref[:, hh * DH_M:(hh + 1) * DH_M])
        h_ref[:, hh * DH_M:(hh + 1) * DH_M] = hn.astype(BF16)

    @pl.when(c == nc - 1)
    def _():
        for hh in range(H_M):
            c_out_ref[0, hh] = state_sc[hh, 0:DH_M, :]
            n_out_ref[0, hh:hh + 1, :] = state_sc[hh, DH_M:DH_M + 1, :]
        m_out_ref[0] = m_sc[0:H_M, :]


def _mlstm(mqkv, small, grow3, mo, ng, c0, n0, m0b, nb, L):
    n = mqkv.shape[0]
    nc = n // nb // L
    d = H_M * DH_M

    def tok(w, blk=0):
        return pl.BlockSpec((L, w), lambda b, c, blk=blk: (b * nc + c, blk))

    per_b3 = pl.BlockSpec((1, H_M, LANES), lambda b, c: (b, 0, 0))
    per_b4 = pl.BlockSpec((1, H_M, DH_M, DH_M), lambda b, c: (b, 0, 0, 0))
    return pl.pallas_call(
        functools.partial(_mlstm_kernel, L=L, nc=nc),
        grid=(nb, nc),
        in_specs=[tok(d, 0), tok(d, 1), tok(d, 2), tok(16),
                  pl.BlockSpec((1, 8, L), lambda b, c: (b * nc + c, 0, 0)),
                  tok(d), pl.BlockSpec((1, d), lambda b, c: (0, 0)),
                  per_b4, per_b3, per_b3],
        out_specs=[tok(d), per_b4, per_b3, per_b3],
        out_shape=[jax.ShapeDtypeStruct((n, d), BF16),
                   jax.ShapeDtypeStruct((nb, H_M, DH_M, DH_M), F32),
                   jax.ShapeDtypeStruct((nb, H_M, DH_M), F32),
                   jax.ShapeDtypeStruct((nb, H_M, LANES), F32)],
        scratch_shapes=[pltpu.VMEM((H_M, 2 * DH_M, DH_M), F32), pltpu.VMEM((8, LANES), F32)],
        compiler_params=_cparams(("parallel", "arbitrary")),
        name="mlstm",
    )(mqkv, mqkv, mqkv, small, grow3, mo, ng, c0, n0, m0b)


def _kth_largest_key(count_ge, rows, k):
    def body(p, carry):
        t_u, cge = carry
        cand_u = t_u | jnp.left_shift(jnp.int32(1), 31 - p)
        cnt = count_ge(cand_u ^ I32_MIN)
        take = cnt >= k
        return jnp.where(take, cand_u, t_u), jnp.where(take, cnt, cge)

    t_u, cge = lax.fori_loop(0, 32, body, (jnp.zeros((rows, 1), I32), jnp.full((rows, 1), 1e9, F32)))
    return t_u ^ I32_MIN, cge


def _tie_cut_index(count_tie_below, rows, need):
    def body(p, x):
        cand = x | jnp.left_shift(jnp.int32(1), INDEX_BITS - 1 - p)
        return jnp.where(count_tie_below(cand) < need, cand, x)

    return lax.fori_loop(0, INDEX_BITS, body, jnp.zeros((rows, 1), I32))


def _dsa_prompt_kernel(iq_ref, small_ref, aq_ref, qpos_ref, ik_ref, k_ref, v_ref, kpos_ref, o_ref,
                       qidx_sc, wrep_sc, qa_sc, s_sc, keys_sc, lg_sc, p_sc, alpha_sc, m_sc, l_sc, acc_sc,
                       cand_sc, t_sc, tu_sc, cnt_sc, aux_sc, hi_sc, kb_sc, cand16_sc, cnt16_sc,
                       *, tq, tk, topk, rb, hb):
    i = pl.program_id(1)
    q0 = i * tq
    nchunk = (i + 1) * (tq // tk)
    ncol = tk // LANES
    nrb = tq // rb
    lane_hb = lax.broadcasted_iota(I32, (hb, LANES), 1)
    lane_minus_row = lax.broadcasted_iota(I32, (rb, LANES), 1) - lax.broadcasted_iota(I32, (rb, LANES), 0)

    for h in range(H_IDX):
        qidx_sc[h * tq:(h + 1) * tq, :] = iq_ref[:, h * D_IDX:(h + 1) * D_IDX]
        wrep_sc[h] = jnp.broadcast_to(small_ref[:, h:h + 1], (tq, LANES))
    for hh in range(H_A):
        qa_sc[hh * tq:(hh + 1) * tq, 0:DH_A] = aq_ref[:, hh * DH_A:(hh + 1) * DH_A]
        qa_sc[hh * tq:(hh + 1) * tq, DH_A:2 * DH_A] = qpos_ref[0, hh * tq:(hh + 1) * tq, :]

    def score_body(c, _):
        ikc = ik_ref[pl.ds(pl.multiple_of(c * tk, tk), tk), :]
        s_sc[...] = _dot_nt(qidx_sc[...], ikc)
        bound = q0 - c * tk
        for r in range(nrb):
            for j in range(ncol):
                acc = jnp.zeros((rb, LANES), F32)
                for h in range(H_IDX):
                    s = s_sc[h * tq + r * rb:h * tq + (r + 1) * rb, j * LANES:(j + 1) * LANES]
                    acc = acc + wrep_sc[h, r * rb:(r + 1) * rb, :] * jnp.maximum(s, 0.0)
                acc = jnp.where(acc == 0.0, 0.0, acc)
                acc = jnp.where(lane_minus_row <= bound + (r * rb - j * LANES), acc, -jnp.inf)
                key = _sort_key(acc)
                keys_sc[c, r * rb:(r + 1) * rb, j * LANES:(j + 1) * LANES] = key
                hi_sc[c, r * rb:(r + 1) * rb, j * LANES:(j + 1) * LANES] = (key >> 16).astype(I16)
        return 0

    lax.fori_loop(0, nchunk, score_body, 0)

    ones_mat = jnp.ones((LANES, LANES), BF16)

    def count_rows(hit_fn):
        cnt_sc[...] = jnp.zeros(cnt_sc.shape, F32)

        def body(c, _):
            for g in range(tq // hb):
                acc = cnt_sc[g * hb:(g + 1) * hb, :]
                for j in range(ncol):
                    kj = keys_sc[c, g * hb:(g + 1) * hb, j * LANES:(j + 1) * LANES]
                    acc = acc + jnp.where(hit_fn(c, g, j, kj), 1.0, 0.0)
                cnt_sc[g * hb:(g + 1) * hb, :] = acc
            return 0

        lax.fori_loop(0, nchunk, body, 0)
        return _dot(cnt_sc[...].astype(BF16), ones_mat)

    def hit_ge(c, g, j, kj):
        return kj >= cand_sc[g * hb:(g + 1) * hb, :]

    def count_rows16(src_sc, strict):
        cnt16_sc[...] = jnp.zeros(cnt16_sc.shape, I16)

        def body(c, _):
            for g in range(tq // hb):
                cb = cand16_sc[g * hb:(g + 1) * hb, :]
                acc = cnt16_sc[g * hb:(g + 1) * hb, :]
                for j in range(ncol):
                    kj = src_sc[c, g * hb:(g + 1) * hb, j * LANES:(j + 1) * LANES]
                    acc = acc + jnp.where((kj > cb) if strict else (kj >= cb), jnp.int16(1), jnp.int16(0))
                cnt16_sc[g * hb:(g + 1) * hb, :] = acc
            return 0

        lax.fori_loop(0, nchunk, body, 0)
        return _dot(cnt16_sc[...].astype(F32).astype(BF16), ones_mat)

    def to_i16(u):
        return (u ^ 0x8000).astype(I16)

    def search16(src_sc, need):
        tu_sc[...] = jnp.zeros(tu_sc.shape, I32)

        def search_pass(p, _):
            bit = jnp.left_shift(jnp.int32(1), 15 - p)
            cand16_sc[...] = to_i16(tu_sc[...] | bit)
            tot = count_rows16(src_sc, False)
            tu_sc[...] = jnp.where(tot >= need(), tu_sc[...] | bit, tu_sc[...])
            return 0

        lax.fori_loop(0, 16, search_pass, 0)

    search16(hi_sc, lambda: topk)
    t_sc[...] = (tu_sc[...] ^ 0x8000) << 16
    cand16_sc[...] = to_i16(tu_sc[...])
    aux_sc[...] = topk - count_rows16(hi_sc, True)

    def low_body(c, _):
        for g in range(tq // hb):
            cb = cand16_sc[g * hb:(g + 1) * hb, :]
            for j in range(ncol):
                kj = keys_sc[c, g * hb:(g + 1) * hb, j * LANES:(j + 1) * LANES]
                lo = ((kj & 0xFFFF) ^ 0x8000).astype(I16)
                hj = hi_sc[c, g * hb:(g + 1) * hb, j * LANES:(j + 1) * LANES]
                kb_sc[c, g * hb:(g + 1) * hb, j * LANES:(j + 1) * LANES] = jnp.where(hj == cb, lo, jnp.int16(-32768))
        return 0

    lax.fori_loop(0, nchunk, low_body, 0)
    search16(kb_sc, lambda: aux_sc[...])
    t_sc[...] = t_sc[...] | tu_sc[...]

    cand_sc[...] = t_sc[...]
    tied = jnp.logical_and(count_rows(hit_ge) > topk, t_sc[...] > NEGINF_KEY)
    any_tied = jnp.max(jnp.where(tied, 1.0, 0.0)) > 0.0

    @pl.when(any_tied)
    def _():
        cand_sc[...] = t_sc[...] + 1
        aux_sc[...] = topk - count_rows(hit_ge)

        def hit_tie_below(c, g, j, kj):
            pos = c * tk + j * LANES + lane_hb
            return jnp.where(kj == t_sc[g * hb:(g + 1) * hb, :], pos, 2 ** 30) < cand_sc[g * hb:(g + 1) * hb, :]

        tu_sc[...] = jnp.zeros(tu_sc.shape, I32)

        def cut_pass(p, _):
            bit = jnp.left_shift(jnp.int32(1), INDEX_BITS - 1 - p)
            cand_sc[...] = tu_sc[...] | bit
            below = count_rows(hit_tie_below)
            tu_sc[...] = jnp.where(below < aux_sc[...], tu_sc[...] | bit, tu_sc[...])
            return 0

        lax.fori_loop(0, INDEX_BITS, cut_pass, 0)

        def drop_body(c, _):
            for g in range(tq // hb):
                tb = t_sc[g * hb:(g + 1) * hb, :]
                xb = tu_sc[g * hb:(g + 1) * hb, :]
                for j in range(ncol):
                    kj = keys_sc[c, g * hb:(g + 1) * hb, j * LANES:(j + 1) * LANES]
                    pos = c * tk + j * LANES + lane_hb
                    late = jnp.where(kj == tb, pos, -1) > xb
                    keys_sc[c, g * hb:(g + 1) * hb, j * LANES:(j + 1) * LANES] = jnp.where(late, kj - 1, kj)
            return 0

        lax.fori_loop(0, nchunk, drop_body, 0)

    t_sc[...] = jnp.maximum(t_sc[...], NEGINF_KEY + 1)
    m_sc[...] = jnp.full(m_sc.shape, NEG, F32)
    l_sc[...] = jnp.zeros(l_sc.shape, F32)
    acc_sc[...] = jnp.zeros(acc_sc.shape, F32)

    def att_body(c, _):
        start = pl.multiple_of(c * tk, tk)
        kaug = jnp.concatenate([k_ref[pl.ds(start, tk), :], kpos_ref[pl.ds(start, tk), :]], axis=1)
        lg_sc[...] = _dot_nt(qa_sc[...], kaug)
        for r in range(nrb):
            tb = t_sc[r * rb:(r + 1) * rb, :]
            selb = [jnp.where(keys_sc[c, r * rb:(r + 1) * rb, j * LANES:(j + 1) * LANES] >= tb, 0.0, NEG)
                    for j in range(ncol)]
            for hh in range(H_A):
                rows = slice(hh * tq + r * rb, hh * tq + (r + 1) * rb)
                lgs = [lg_sc[rows, j * LANES:(j + 1) * LANES] + selb[j] for j in range(ncol)]
                mx = lgs[0]
                for j in range(1, ncol):
                    mx = jnp.maximum(mx, lgs[j])
                m_old = m_sc[rows, :]
                m_new = jnp.maximum(m_old, jnp.max(mx, axis=1, keepdims=True))
                alpha = jnp.exp(m_old - m_new)
                ps = [jnp.exp(lg - m_new) for lg in lgs]
                psum = ps[0]
                for j in range(1, ncol):
                    psum = psum + ps[j]
                l_sc[rows, :] = alpha * l_sc[rows, :] + psum
                for j in range(ncol):
                    p_sc[rows, j * LANES:(j + 1) * LANES] = ps[j].astype(BF16)
                alpha_sc[rows, :] = alpha
                m_sc[rows, :] = m_new
        acc_sc[...] = alpha_sc[...] * acc_sc[...] + _dot(p_sc[...], v_ref[pl.ds(start, tk), :])
        return 0

    lax.fori_loop(0, nchunk, att_body, 0)
    for hh in range(H_A):
        rows = slice(hh * tq, (hh + 1) * tq)
        l = jnp.sum(l_sc[rows, :], axis=1, keepdims=True)
        o_ref[:, hh * DH_A:(hh + 1) * DH_A] = (acc_sc[rows, :] / l).astype(BF16)


def _alibi_tables(seq, tq):
    pos = np.arange(seq)
    hi, lo = (pos // 64).astype(np.float32), (pos % 64).astype(np.float32)
    ktab = np.zeros((seq, DH_A), np.float32)
    ktab[:, 0], ktab[:, 1], ktab[:, 2], ktab[:, 3] = 64.0 * hi, lo, 1.0, 1.0
    nq = seq // tq
    qtab = np.zeros((nq, H_A, tq, DH_A), np.float32)
    for hh, slope in enumerate(ALIBI_SLOPES):
        qtab[:, hh, :, 0] = slope
        qtab[:, hh, :, 1] = slope
        qtab[:, hh, :, 2] = (-slope * 64.0 * hi).reshape(nq, tq)
        qtab[:, hh, :, 3] = (-slope * lo).reshape(nq, tq)
    return jnp.asarray(qtab.reshape(nq, H_A * tq, DH_A), BF16), jnp.asarray(ktab, BF16)


def _dsa_prompt(iq, small, aq, ikb, kb, vb, nb, seq, tq):
    n = iq.shape[0]
    nq = seq // tq
    topk = min(TOPK_MAX, seq // 4)
    tk = min(256, tq)
    rb = min(64, tq)
    hb = min(128, tq)
    assert seq // LANES <= 256
    qtab, ktab = _alibi_tables(seq, tq)

    def tok(w):
        return pl.BlockSpec((tq, w), lambda b, i: (b * nq + i, 0))

    def per_b(w):
        return pl.BlockSpec((seq, w), lambda b, i: (b, 0))

    return pl.pallas_call(
        functools.partial(_dsa_prompt_kernel, tq=tq, tk=tk, topk=topk, rb=rb, hb=hb),
        grid=(nb, nq),
        in_specs=[tok(512), tok(16), tok(512),
                  pl.BlockSpec((1, H_A * tq, DH_A), lambda b, i: (i, 0, 0)),
                  per_b(D_IDX), per_b(DH_A), per_b(DH_A),
                  pl.BlockSpec((seq, DH_A), lambda b, i: (0, 0))],
        out_specs=tok(512),
        out_shape=jax.ShapeDtypeStruct((n, 512), BF16),
        scratch_shapes=[pltpu.VMEM((H_IDX * tq, D_IDX), BF16),
                        pltpu.VMEM((H_IDX, tq, LANES), F32),
                        pltpu.VMEM((H_A * tq, 2 * DH_A), BF16),
                        pltpu.VMEM((H_IDX * tq, tk), F32),
                        pltpu.VMEM((seq // tk, tq, tk), I32),
                        pltpu.VMEM((H_A * tq, tk), F32),
                        pltpu.VMEM((H_A * tq, tk), BF16),
                        pltpu.VMEM((H_A * tq, LANES), F32),
                        pltpu.VMEM((H_A * tq, LANES), F32),
                        pltpu.VMEM((H_A * tq, LANES), F32),
                        pltpu.VMEM((H_A * tq, DH_A), F32),
                        pltpu.VMEM((tq, LANES), I32),
                        pltpu.VMEM((tq, LANES), I32),
                        pltpu.VMEM((tq, LANES), I32),
                        pltpu.VMEM((tq, LANES), F32),
                        pltpu.VMEM((tq, LANES), F32),
                        pltpu.VMEM((seq // tk, tq, tk), I16),
                        pltpu.VMEM((seq // tk, tq, tk), I16),
                        pltpu.VMEM((tq, LANES), I16),
                        pltpu.VMEM((tq, LANES), I16)],
        compiler_params=_cparams(("parallel", "arbitrary")),
        name="dsa_prompt",
    )(iq, small, aq, qtab, ikb, kb, vb, ktab)


def _dsa_sample_kernel(pt_ref, iq_ref, w_ref, q_ref, iknew_ref, knew_ref, vnew_ref,
                       ckidx_hbm, ck_hbm, cv_hbm, o_ref,
                       ibuf, kbuf, vbuf, sems, keys_sc, lg_sc, *, n_pages, n_new, topk, cw):
    b = pl.program_id(0)
    nb = pl.num_programs(0)
    past = n_pages * PAGE
    total = past + LANES
    slot = b % 2
    rows_q = n_new * H_A

    def page_copies(bb, sl, j):
        pg = pt_ref[bb, j]
        dst = pl.ds(pl.multiple_of(j * PAGE, PAGE), PAGE)
        return (pltpu.make_async_copy(ckidx_hbm.at[pg], ibuf.at[sl, :, dst], sems.at[0, sl]),
                pltpu.make_async_copy(ck_hbm.at[pg], kbuf.at[sl, dst, :], sems.at[1, sl]),
                pltpu.make_async_copy(cv_hbm.at[pg], vbuf.at[sl, dst, :], sems.at[2, sl]))

    def start_all(bb, sl):
        def body(j, _):
            for cp in page_copies(bb, sl, j):
                cp.start()
            return 0
        lax.fori_loop(0, n_pages, body, 0)

    def wait_all(bb, sl):
        def body(j, _):
            for cp in page_copies(bb, sl, j):
                cp.wait()
            return 0
        lax.fori_loop(0, n_pages, body, 0)

    @pl.when(b == 0)
    def _():
        start_all(0, 0)

    @pl.when(b + 1 < nb)
    def _():
        start_all(b + 1, 1 - slot)

    wait_all(b, slot)

    iq = iq_ref[0]
    w = w_ref[0]
    keys_sc[...] = jnp.full(keys_sc.shape, NEGINF_KEY, I32)

    def scores(dots):
        s = jnp.maximum(dots, 0.0) * w
        s = jnp.sum(s.reshape(n_new, H_IDX, s.shape[-1]), axis=1)
        return jnp.where(s == 0.0, 0.0, s)

    for ch in range(past // cw):
        sc = scores(_dot(iq, ibuf[slot, :, ch * cw:(ch + 1) * cw].astype(BF16)))
        keys_sc[0:n_new, ch * cw:(ch + 1) * cw] = _sort_key(sc)
    sc = scores(_dot_nt(iq, iknew_ref[0]))
    t_i = lax.broadcasted_iota(I32, (n_new, LANES), 0)
    j_i = lax.broadcasted_iota(I32, (n_new, LANES), 1)
    keys_sc[0:n_new, past:total] = _sort_key(jnp.where(j_i <= t_i, sc, -jnp.inf))

    ncols = total // LANES
    lane8 = lax.broadcasted_iota(I32, (8, LANES), 1)

    def count_ge(cand):
        cb = jnp.broadcast_to(cand, (8, LANES))
        accs = [jnp.zeros((8, LANES), F32) for _ in range(4)]
        for j in range(ncols):
            accs[j % 4] = accs[j % 4] + jnp.where(keys_sc[:, j * LANES:(j + 1) * LANES] >= cb, 1.0, 0.0)
        return jnp.sum((accs[0] + accs[1]) + (accs[2] + accs[3]), axis=1, keepdims=True)

    thr, cge = _kth_largest_key(count_ge, 8, topk)
    tied = jnp.logical_and(cge > topk, thr > NEGINF_KEY)
    any_tied = jnp.max(jnp.where(tied, 1.0, 0.0)) > 0.0

    @pl.when(any_tied)
    def _():
        need = topk - count_ge(thr + 1)
        tb = jnp.broadcast_to(thr, (8, LANES))

        def count_tie_below(xc):
            xb = jnp.broadcast_to(xc, (8, LANES))
            acc = jnp.zeros((8, LANES), F32)
            for j in range(ncols):
                pos = j * LANES + lane8
                hit = jnp.where(keys_sc[:, j * LANES:(j + 1) * LANES] == tb, pos, 2 ** 30) < xb
                acc = acc + jnp.where(hit, 1.0, 0.0)
            return jnp.sum(acc, axis=1, keepdims=True)

        cut = _tie_cut_index(count_tie_below, 8, need)
        xb = jnp.broadcast_to(cut, (8, LANES))
        for j in range(ncols):
            kj = keys_sc[:, j * LANES:(j + 1) * LANES]
            late = jnp.where(kj == tb, j * LANES + lane8, -1) > xb
            keys_sc[:, j * LANES:(j + 1) * LANES] = jnp.where(late, kj - 1, kj)

    teff = jnp.maximum(thr, NEGINF_KEY + 1)
    r_i = lax.broadcasted_iota(I32, (rows_q, 1), 0)
    r_t = r_i // H_A
    r_h = r_i % H_A
    slope = jnp.zeros((rows_q, 1), F32)
    for hh in range(H_A):
        slope = jnp.where(r_h == hh, ALIBI_SLOPES[hh], slope)
    qposf = (past + r_t).astype(F32)
    q = q_ref[0]

    def sel_bias(lo, width):
        out = jnp.full((rows_q, width), NEG, F32)
        for t in range(n_new):
            kt = keys_sc[t:t + 1, lo:lo + width]
            bias_t = jnp.where(kt >= teff[t:t + 1, :], 0.0, NEG)
            out = jnp.where(r_t == t, bias_t, out)
        return out

    def logits(k_chunk, lo, width):
        kposf = (lo + lax.broadcasted_iota(I32, (1, width), 1)).astype(F32)
        sel = sel_bias(lo, width)
        lg = _dot_nt(q, k_chunk) - slope * (qposf - kposf)
        return lg + sel

    for ch in range(past // cw):
        lg_sc[:, ch * cw:(ch + 1) * cw] = logits(kbuf[slot, ch * cw:(ch + 1) * cw, :].astype(BF16), ch * cw, cw)
    lg_sc[:, past:total] = logits(knew_ref[0], past, LANES)

    lg = lg_sc[...]
    m = jnp.max(lg, axis=1, keepdims=True)
    p = jnp.exp(lg - m)
    l = jnp.sum(p, axis=1, keepdims=True)
    pb = p.astype(BF16)
    acc = _dot(pb[:, past:total], vnew_ref[0])
    for ch in range(past // cw):
        acc = acc + _dot(pb[:, ch * cw:(ch + 1) * cw], vbuf[slot, ch * cw:(ch + 1) * cw, :].astype(BF16))
    o_ref[0] = (acc / l).astype(BF16)


def _dsa_sample(page_table, iq32, w32, q16, iknew, knew, vnew, cache_kidx, cache_k, cache_v, n_new):
    db, n_pages = page_table.shape
    past = n_pages * PAGE
    total = past + LANES
    topk = min(TOPK_MAX, (past + n_new) // 4)
    cw = 1024 if past % 1024 == 0 else PAGE
    rows_q = n_new * H_A

    def per_b(a):
        return pl.BlockSpec((1,) + a.shape[1:], lambda b, pt: (b, 0, 0))

    hbm = pl.BlockSpec(memory_space=pl.ANY)
    grid_spec = pltpu.PrefetchScalarGridSpec(
        num_scalar_prefetch=1,
        grid=(db,),
        in_specs=[per_b(iq32), per_b(w32), per_b(q16), per_b(iknew), per_b(knew), per_b(vnew), hbm, hbm, hbm],
        out_specs=pl.BlockSpec((1, rows_q, DH_A), lambda b, pt: (b, 0, 0)),
        scratch_shapes=[pltpu.VMEM((2, D_IDX, past), F32), pltpu.VMEM((2, past, DH_A), F32),
                        pltpu.VMEM((2, past, DH_A), F32), pltpu.SemaphoreType.DMA((3, 2)),
                        pltpu.VMEM((8, total), I32), pltpu.VMEM((rows_q, total), F32)],
    )
    return pl.pallas_call(
        functools.partial(_dsa_sample_kernel, n_pages=n_pages, n_new=n_new, topk=topk, cw=cw),
        grid_spec=grid_spec,
        out_shape=jax.ShapeDtypeStruct((db, rows_q, DH_A), BF16),
        compiler_params=_cparams(("arbitrary",)),
        name="dsa_sample",
    )(page_table, iq32, w32, q16, iknew, knew, vnew, cache_kidx, cache_k, cache_v)


def _outffn_kernel(x_ref, hm_ref, ha_ref, mod_ref, g2_ref, wom_ref, woa_ref, wg_ref, wu_ref, wd_ref, y_ref):
    mix = _dot(hm_ref[...], wom_ref[...]) + _dot(ha_ref[...], woa_ref[...])
    x1 = x_ref[...] + mod_ref[0, 0] * mix
    xn = x1 * lax.rsqrt(jnp.mean(x1 * x1, axis=-1, keepdims=True) + EPS) * g2_ref[...]
    hb = (xn * (1.0 + mod_ref[2, 0]) + mod_ref[1, 0]).astype(BF16)
    g = _dot(hb, wg_ref[...])
    u = _dot(hb, wu_ref[...])
    act = (g * _sigmoid(g) * u).astype(BF16)
    y_ref[...] = x1 + mod_ref[3, 0] * _dot(act, wd_ref[...])


def _outffn(x, hm, ha, mod_out, per_token_mod, rows_per_mod, g2, wom, woa, wg, wu, wd, tm):
    n, d = x.shape
    if per_token_mod:
        mod_spec = pl.BlockSpec((4, 1, tm, d), lambda i: (0, 0, i, 0))
    else:
        tiles = rows_per_mod // tm
        mod_spec = pl.BlockSpec((4, 1, 1, d), lambda i: (0, i // tiles, 0, 0))

    def full(a):
        return pl.BlockSpec(a.shape, lambda i: (0,) * a.ndim, pipeline_mode=pl.Buffered(1))

    def rows(w):
        return pl.BlockSpec((tm, w), lambda i: (i, 0))

    return pl.pallas_call(
        _outffn_kernel,
        grid=(n // tm,),
        in_specs=[rows(d), rows(512), rows(512), mod_spec, full(g2), full(wom), full(woa),
                  full(wg), full(wu), full(wd)],
        out_specs=rows(d),
        out_shape=jax.ShapeDtypeStruct((n, d), F32),
        compiler_params=_cparams(("parallel",)),
        name="outffn",
    )(x, hm, ha, mod_out, g2, wom, woa, wg, wu, wd)


def _pick_tile(n, pref):
    t = pref
    while n % t:
        t //= 2
    return t


def kernel(x_prompt, x_sample, cache_k, cache_v, cache_kidx, state_C, state_n, state_m, page_table,
           c_prompt, c_sample, w_ada, b_ada, g_norm1, w_in, b_igate, b_fgate, mlstm_norm_g,
           q_norm_g, k_norm_g, w_out, g_norm2, w_gate, w_up, w_down):
    bp, seq, d = x_prompt.shape
    db, t_new, _ = x_sample.shape
    n_p, n_s = bp * seq, db * t_new

    o = np.cumsum([0, 512, 512, 512, 512, H_M, H_M, 512, DH_A, DH_A, H_IDX * D_IDX, D_IDX, H_IDX])
    mq, mk, mv, mo, mi, mf, aq, ak, av, iq, ik, iw = [w_in[:, int(o[j]):int(o[j + 1])] for j in range(12)]
    zpad = lambda wdt: jnp.zeros((d, wdt), w_in.dtype)
    w_r = jnp.concatenate([mq, mk, mv, mo, aq, ak, av, iq, ik, zpad(LANES - D_IDX),
                           iw, mi, mf, zpad(LANES - 16)], axis=1).astype(BF16)
    wgt = jnp.concatenate([mi, mf], axis=1).T.astype(BF16)
    bsm = jnp.concatenate([jnp.zeros((H_IDX,), F32), b_igate, b_fgate]).reshape(1, 16)
    brow = jnp.concatenate([b_igate, b_fgate]).reshape(8, 1)
    g1 = g_norm1.reshape(1, d)
    g2 = g_norm2.reshape(1, d)
    qg = q_norm_g.reshape(1, DH_A)
    kg = k_norm_g.reshape(1, DH_A)
    ng = mlstm_norm_g.reshape(1, H_M * DH_M)
    wom = w_out[0:H_M * DH_M].astype(BF16)
    woa = w_out[H_M * DH_M:].astype(BF16)
    wg = w_gate.astype(BF16)
    wu = w_up.astype(BF16)
    wd = w_down.astype(BF16)

    mod = _ada(jnp.concatenate([c_prompt, c_sample], axis=0), w_ada.astype(BF16), b_ada)
    mod_p = mod[:bp].reshape(bp, 6, 1, d).transpose(1, 0, 2, 3)
    mod_s = jnp.repeat(mod[bp:].reshape(db, 6, d), t_new, axis=0)
    mod_s = mod_s.transpose(1, 0, 2).reshape(6, 1, n_s, d)

    tm_p = _pick_tile(seq, 512)
    (mqkv, mo_p, aq_p, k_p, v_p, kidx_p, kb, vb, ikb, iq_p, small_p, grow_p) = _inproj(
        x_prompt.reshape(n_p, d), mod_p[0:2], False, seq, g1, w_r, wgt, bsm, brow, qg, kg, tm_p)

    lc = _pick_tile(seq, 256)
    grow3 = grow_p.reshape(8, n_p // lc, lc).transpose(1, 0, 2)
    hm_p, C_p, n_pst, m_pb = _mlstm(
        mqkv, small_p, grow3, mo_p, ng,
        jnp.zeros((bp, H_M, DH_M, DH_M), F32), jnp.zeros((bp, H_M, DH_M), F32),
        jnp.zeros((bp, H_M, LANES), F32), bp, lc)

    tq = _pick_tile(seq, 512)
    ha_p = _dsa_prompt(iq_p, small_p, aq_p, ikb, kb, vb, bp, seq, tq)

    y_p = _outffn(x_prompt.reshape(n_p, d), hm_p, ha_p, mod_p[2:6], False, seq, g2,
                  wom, woa, wg, wu, wd, tm_p)

    tm_s = _pick_tile(n_s, 128)
    (mqkv_s, mo_s, aq_s, k_s, v_s, kidx_s, kb_s, vb_s, ikb_s, iq_s, small_s, grow_s) = _inproj(
        x_sample.reshape(n_s, d), mod_s[0:2], True, 0, g1, w_r, wgt, bsm, brow, qg, kg, tm_s)

    lp = 16
    pad_tok = lambda a: jnp.pad(a.reshape(db, t_new, a.shape[-1]),
                                ((0, 0), (0, lp - t_new), (0, 0))).reshape(db * lp, a.shape[-1])
    gate_pad = jnp.concatenate([jnp.zeros((H_IDX,), F32), jnp.full((H_M,), -jnp.inf, F32),
                                jnp.zeros((H_M,), F32)])
    small_pad = jnp.concatenate(
        [small_s.reshape(db, t_new, 16), jnp.broadcast_to(gate_pad, (db, lp - t_new, 16))], axis=1
    ).reshape(db * lp, 16)
    grow_pad = jnp.concatenate(
        [grow_s.reshape(8, db, t_new),
         jnp.broadcast_to(gate_pad[H_IDX:].reshape(8, 1, 1), (8, db, lp - t_new))], axis=2
    ).transpose(1, 0, 2)
    hm_s_pad, C_s, n_sst, m_sb = _mlstm(
        pad_tok(mqkv_s), small_pad, grow_pad, pad_tok(mo_s), ng,
        state_C, state_n, jnp.broadcast_to(state_m[:, :, None], (db, H_M, LANES)), db, lp)
    hm_s = hm_s_pad.reshape(db, lp, H_M * DH_M)[:, :t_new].reshape(n_s, H_M * DH_M)

    pad_rows = lambda a: jnp.pad(a.reshape(db, t_new, a.shape[-1]), ((0, 0), (0, LANES - t_new), (0, 0)))
    ha_s = _dsa_sample(
        page_table, iq_s.reshape(db, t_new * H_IDX, D_IDX),
        small_s[:, 0:H_IDX].reshape(db, t_new * H_IDX, 1),
        aq_s.reshape(db, t_new * H_A, DH_A),
        pad_rows(ikb_s), pad_rows(kb_s), pad_rows(vb_s),
        jnp.swapaxes(cache_kidx, 1, 2), cache_k, cache_v, t_new).reshape(n_s, H_A * DH_A)

    y_s = _outffn(x_sample.reshape(n_s, d), hm_s, ha_s, mod_s[2:6], True, 0, g2,
                  wom, woa, wg, wu, wd, tm_s)

    return (y_p.reshape(bp, seq, d), y_s.reshape(db, t_new, d),
            k_p.reshape(bp, seq, DH_A), v_p.reshape(bp, seq, DH_A), kidx_p.reshape(bp, seq, D_IDX),
            C_p, n_pst, m_pb[:, :, 0],
            k_s.reshape(db, t_new, DH_A), v_s.reshape(db, t_new, DH_A), kidx_s.reshape(db, t_new, D_IDX),
            C_s, n_sst, m_sb[:, :, 0])
```

```python
import functools

import jax
import jax.numpy as jnp
import numpy as np
from jax import lax
from jax.experimental import pallas as pl
from jax.experimental.pallas import tpu as pltpu

F32 = jnp.float32
BF16 = jnp.bfloat16
I32 = jnp.int32

H_M = 4
DH_M = 128
H_A = 4
DH_A = 128
H_IDX = 8
D_IDX = 64
TOPK_MAX = 256
PAGE = 128
EPS = 1e-6
INDEX_SCALE = D_IDX ** -0.5 * H_IDX ** -0.5
ALIBI_SLOPES = tuple(float(2.0 ** (-8.0 * (h + 1) / H_A)) for h in range(H_A))

LANES = 128
VMEM_LIMIT = 56 * 1024 * 1024
NEG = -1e30
I32_MIN = -2 ** 31
NEGINF_KEY = -2139095041
INDEX_BITS = 14

C_MQ, C_MK, C_MV, C_MO, C_AQ, C_AK, C_AV, C_IQ, C_IK, C_SM, C_END = (
    0, 512, 1024, 1536, 2048, 2560, 2688, 2816, 3328, 3456, 3584)


def _cparams(sem):
    return pltpu.CompilerParams(dimension_semantics=sem, vmem_limit_bytes=VMEM_LIMIT)


def _sigmoid(x):
    return 1.0 / (1.0 + jnp.exp(-x))


def _log_sigmoid(x):
    return jnp.minimum(x, 0.0) - jnp.log1p(jnp.exp(-jnp.abs(x)))


def _dot(a, b):
    return jnp.dot(a, b, preferred_element_type=F32)


def _dot_nt(a, b):
    return lax.dot_general(a, b, (((1,), (1,)), ((), ())), preferred_element_type=F32)


def _dot_tn(a, b):
    return lax.dot_general(a, b, (((0,), (0,)), ((), ())), preferred_element_type=F32)


def _split3(x):
    hi = x.astype(BF16)
    r1 = x - hi.astype(F32)
    mid = r1.astype(BF16)
    lo = (r1 - mid.astype(F32)).astype(BF16)
    return hi, mid, lo


def _sort_key(x):
    b = pltpu.bitcast(x, I32)
    return b ^ ((b >> 31) & 0x7FFFFFFF)


def _ada_kernel(c_ref, w_ref, b_ref, o_ref):
    c = c_ref[...]
    s = c * _sigmoid(c)
    o_ref[...] = _dot(s.astype(BF16), w_ref[...]) + b_ref[...]


def _ada(c, w_bf, b):
    r, d = c.shape
    n = w_bf.shape[1]
    return pl.pallas_call(
        _ada_kernel,
        grid=(n // d,),
        in_specs=[pl.BlockSpec((r, d), lambda j: (0, 0)),
                  pl.BlockSpec((d, d), lambda j: (0, j)),
                  pl.BlockSpec((1, d), lambda j: (0, j))],
        out_specs=pl.BlockSpec((r, d), lambda j: (0, j)),
        out_shape=jax.ShapeDtypeStruct((r, n), F32),
        compiler_params=_cparams(("arbitrary",)),
        name="ada",
    )(c, w_bf, b.reshape(1, n))


def _inproj_kernel(x_ref, mod_ref, g1_ref, w_ref, wgt_ref, bsm_ref, brow_ref, qg_ref, kg_ref,
                   mqkv_ref, mo_ref, aq_ref, k_ref, v_ref, kidx_ref, kb_ref, vb_ref, ikb_ref,
                   iq_ref, small_ref, grow_ref):
    x = x_ref[...]
    xn = x * lax.rsqrt(jnp.mean(x * x, axis=-1, keepdims=True) + EPS) * g1_ref[...]
    h = xn * (1.0 + mod_ref[1, 0]) + mod_ref[0, 0]
    hb = h.astype(BF16)

    def sec(a, b):
        return _dot(hb, w_ref[:, a:b])

    mqkv_ref[:, 0:512] = sec(C_MQ, C_MK).astype(BF16)
    mqkv_ref[:, 512:1024] = (sec(C_MK, C_MV) * (DH_M ** -0.5)).astype(BF16)
    mqkv_ref[:, 1024:1536] = sec(C_MV, C_MO).astype(BF16)
    mo_ref[...] = sec(C_MO, C_AQ)

    aq = sec(C_AQ, C_AK)
    qg = qg_ref[...]
    for hh in range(H_A):
        a = aq[:, hh * DH_A:(hh + 1) * DH_A]
        a = a * lax.rsqrt(jnp.mean(a * a, axis=-1, keepdims=True) + EPS) * qg
        aq_ref[:, hh * DH_A:(hh + 1) * DH_A] = (a * (DH_A ** -0.5)).astype(BF16)

    ak = sec(C_AK, C_AV)
    ak = ak * lax.rsqrt(jnp.mean(ak * ak, axis=-1, keepdims=True) + EPS) * kg_ref[...]
    k_ref[...] = ak
    kb_ref[...] = ak.astype(BF16)
    av = sec(C_AV, C_IQ)
    v_ref[...] = av
    vb_ref[...] = av.astype(BF16)

    iq_ref[...] = sec(C_IQ, C_IK).astype(BF16)
    ik = sec(C_IK, C_SM)[:, 0:D_IDX]
    kidx_ref[...] = ik
    ikb_ref[...] = ik.astype(BF16)

    sm = sec(C_SM, C_END)[:, 0:16] + bsm_ref[...]
    col = lax.broadcasted_iota(I32, sm.shape, 1)
    small_ref[...] = jnp.where(col < H_IDX, sm * INDEX_SCALE,
                               jnp.where(col < H_IDX + H_M, sm, _log_sigmoid(sm)))

    gr = _dot_nt(wgt_ref[...], hb) + brow_ref[...]
    row = lax.broadcasted_iota(I32, gr.shape, 0)
    grow_ref[...] = jnp.where(row < H_M, gr, _log_sigmoid(gr))


def _inproj(x, mod_in, per_token_mod, rows_per_mod, g1, w_r, wgt, bsm, brow, qg, kg, tm):
    n, d = x.shape
    if per_token_mod:
        mod_spec = pl.BlockSpec((2, 1, tm, d), lambda i: (0, 0, i, 0))
    else:
        tiles = rows_per_mod // tm
        mod_spec = pl.BlockSpec((2, 1, 1, d), lambda i: (0, i // tiles, 0, 0))

    def full(a):
        return pl.BlockSpec(a.shape, lambda i: (0,) * a.ndim)

    def rows(w):
        return pl.BlockSpec((tm, w), lambda i: (i, 0))

    outs = [(1536, BF16), (512, F32), (512, BF16), (128, F32), (128, F32), (D_IDX, F32),
            (128, BF16), (128, BF16), (D_IDX, BF16), (512, BF16), (16, F32)]
    out_shape = [jax.ShapeDtypeStruct((n, w), dt) for w, dt in outs]
    out_specs = [rows(w) for w, _ in outs]
    out_shape.append(jax.ShapeDtypeStruct((8, n), F32))
    out_specs.append(pl.BlockSpec((8, tm), lambda i: (0, i)))
    return pl.pallas_call(
        _inproj_kernel,
        grid=(n // tm,),
        in_specs=[rows(d), mod_spec, full(g1), full(w_r), full(wgt), full(bsm), full(brow),
                  full(qg), full(kg)],
        out_specs=out_specs,
        out_shape=out_shape,
        compiler_params=_cparams(("parallel",)),
        name="inproj",
    )(x, mod_in, g1, w_r, wgt, bsm, brow, qg, kg)


def _mlstm_kernel(q_ref, k_ref, v_ref, small_ref, grow_ref, o_ref, ng_ref, c0_ref, n0_ref, m0_ref,
                  h_ref, c_out_ref, n_out_ref, m_out_ref, state_sc, m_sc, *, L, nc):
    c = pl.program_id(1)

    @pl.when(c == 0)
    def _():
        state_sc[...] = jnp.zeros(state_sc.shape, F32)
        for hh in range(H_M):
            state_sc[hh, 0:DH_M, :] = c0_ref[0, hh]
            state_sc[hh, DH_M:DH_M + 1, :] = n0_ref[0, hh:hh + 1, :]
        m_sc[...] = jnp.zeros(m_sc.shape, F32)
        m_sc[0:H_M, :] = m0_ref[0]

    ti = lax.broadcasted_iota(I32, (L, L), 0)
    si = lax.broadcasted_iota(I32, (L, L), 1)
    causal = si <= ti
    tri = jnp.where(causal, 1.0, 0.0).astype(BF16)
    tri_t = jnp.where(ti <= si, 1.0, 0.0).astype(BF16)

    sm = small_ref[...]
    col = lax.broadcasted_iota(I32, sm.shape, 1)
    lf_cols = jnp.where(col >= H_IDX + H_M, sm, 0.0)
    hi, mid, lo = _split3(lf_cols)
    b_cols = _dot(tri, hi) + _dot(tri, mid) + _dot(tri, lo)
    gr = grow_ref[0]
    row = lax.broadcasted_iota(I32, gr.shape, 0)
    lf_rows = jnp.where(row >= H_M, gr, 0.0)
    hi, mid, lo = _split3(lf_rows)
    b_rows = _dot(hi, tri_t) + _dot(mid, tri_t) + _dot(lo, tri_t)

    lane = lax.broadcasted_iota(I32, (L, DH_M), 1)
    ones_col = jnp.where(lane == 0, 1.0, 0.0).astype(BF16)

    for hh in range(H_M):
        q = q_ref[:, hh * DH_M:(hh + 1) * DH_M]
        k = k_ref[:, hh * DH_M:(hh + 1) * DH_M]
        v = v_ref[:, hh * DH_M:(hh + 1) * DH_M]
        ig_c = sm[:, H_IDX + hh:H_IDX + hh + 1]
        b_c = b_cols[:, H_IDX + H_M + hh:H_IDX + H_M + hh + 1]
        ig_r = gr[hh:hh + 1, :]
        b_r = b_rows[H_M + hh:H_M + hh + 1, :]
        m_prev = m_sc[hh:hh + 1, 0:1]

        dmat = jnp.where(causal, b_c - b_r + ig_r, -jnp.inf)
        inter = b_c + m_prev
        m_t = jnp.maximum(inter, jnp.max(dmat, axis=-1, keepdims=True))
        w = jnp.exp(dmat - m_t)
        g = jnp.exp(inter - m_t)
        a = w * _dot_nt(q, k)
        st = state_sc[hh]
        qs = _dot_nt(q, st.astype(BF16))
        num = g * qs[:, 0:DH_M] + _dot(a.astype(BF16), v)
        den = g * qs[:, DH_M:DH_M + 1] + jnp.sum(a, axis=-1, keepdims=True)
        hv = num / jnp.maximum(jnp.abs(den), jnp.exp(-m_t))

        m_new = m_t[L - 1:L, :]
        b_last = b_c[L - 1:L, :]
        wend = jnp.exp(b_last - b_c + ig_c - m_new)
        gend = jnp.exp(b_last + m_prev - m_new)
        kw = (k.astype(F32) * wend).astype(BF16)
        v_aug = jnp.concatenate([v, ones_col], axis=1)
        state_sc[hh] = gend * st + _dot_tn(v_aug, kw)
        m_sc[hh:hh + 1, :] = jnp.broadcast_to(m_new, (1, LANES))

        hn = hv * lax.rsqrt(jnp.mean(hv * hv, axis=-1, keepdims=True) + EPS)
        hn = hn * ng_ref[:, hh * DH_M:(hh + 1) * DH_M]
        hn = hn * _sigmoid(o_ref[:, hh * DH_M:(hh + 1) * DH_M])
        h_ref[:, hh * DH_M:(hh + 1) * DH_M] = hn.astype(BF16)

    @pl.when(c == nc - 1)
    def _():
        for hh in range(H_M):
            c_out_ref[0, hh] = state_sc[hh, 0:DH_M, :]
            n_out_ref[0, hh:hh + 1, :] = state_sc[hh, DH_M:DH_M + 1, :]
        m_out_ref[0] = m_sc[0:H_M, :]


def _mlstm(mqkv, small, grow3, mo, ng, c0, n0, m0b, nb, L):
    n = mqkv.shape[0]
    nc = n // nb // L
    d = H_M * DH_M

    def tok(w, blk=0):
        return pl.BlockSpec((L, w), lambda b, c, blk=blk: (b * nc + c, blk))

    per_b3 = pl.BlockSpec((1, H_M, LANES), lambda b, c: (b, 0, 0))
    per_b4 = pl.BlockSpec((1, H_M, DH_M, DH_M), lambda b, c: (b, 0, 0, 0))
    return pl.pallas_call(
        functools.partial(_mlstm_kernel, L=L, nc=nc),
        grid=(nb, nc),
        in_specs=[tok(d, 0), tok(d, 1), tok(d, 2), tok(16),
                  pl.BlockSpec((1, 8, L), lambda b, c: (b * nc + c, 0, 0)),
                  tok(d), pl.BlockSpec((1, d), lambda b, c: (0, 0)),
                  per_b4, per_b3, per_b3],
        out_specs=[tok(d), per_b4, per_b3, per_b3],
        out_shape=[jax.ShapeDtypeStruct((n, d), BF16),
                   jax.ShapeDtypeStruct((nb, H_M, DH_M, DH_M), F32),
                   jax.ShapeDtypeStruct((nb, H_M, DH_M), F32),
                   jax.ShapeDtypeStruct((nb, H_M, LANES), F32)],
        scratch_shapes=[pltpu.VMEM((H_M, 2 * DH_M, DH_M), F32), pltpu.VMEM((8, LANES), F32)],
        compiler_params=_cparams(("parallel", "arbitrary")),
        name="mlstm",
    )(mqkv, mqkv, mqkv, small, grow3, mo, ng, c0, n0, m0b)


def _topk_search(keys_sc, cand_sc, t_sc, tu_sc, aux_sc, *, nchunk, rows, tk, hb, topk):
    ncol = tk // LANES
    ngrp = rows // hb
    lane_hb = lax.broadcasted_iota(I32, (hb, LANES), 1)
    ones_mat = jnp.ones((LANES, LANES), BF16)

    def count_rows(hit_fn):
        parts = []
        for g in range(ngrp):
            def body(c, acc, g=g):
                for j in range(ncol):
                    kj = keys_sc[c, g * hb:(g + 1) * hb, j * LANES:(j + 1) * LANES]
                    acc = acc + jnp.where(hit_fn(c, g, j, kj), 1.0, 0.0)
                return acc

            parts.append(lax.fori_loop(0, nchunk, body, jnp.zeros((hb, LANES), F32)))
        cnt = parts[0] if ngrp == 1 else jnp.concatenate(parts, axis=0)
        return _dot(cnt.astype(BF16), ones_mat)

    def hit_ge(c, g, j, kj):
        return kj >= cand_sc[g * hb:(g + 1) * hb, :]

    tu_sc[...] = jnp.zeros(tu_sc.shape, I32)

    def search_pass(p, _):
        bit = jnp.left_shift(jnp.int32(1), 31 - p)
        cand_sc[...] = (tu_sc[...] | bit) ^ I32_MIN
        tu_sc[...] = jnp.where(count_rows(hit_ge) >= topk, tu_sc[...] | bit, tu_sc[...])
        return 0

    lax.fori_loop(0, 32, search_pass, 0)
    t_sc[...] = tu_sc[...] ^ I32_MIN

    cand_sc[...] = t_sc[...]
    tied = jnp.logical_and(count_rows(hit_ge) > topk, t_sc[...] > NEGINF_KEY)
    any_tied = jnp.max(jnp.where(tied, 1.0, 0.0)) > 0.0
    tu_sc[...] = jnp.full(tu_sc.shape, 2 ** 31 - 1, I32)

    @pl.when(any_tied)
    def _():
        cand_sc[...] = t_sc[...] + 1
        aux_sc[...] = topk - count_rows(hit_ge)

        def hit_tie_below(c, g, j, kj):
            pos = c * tk + j * LANES + lane_hb
            return jnp.where(kj == t_sc[g * hb:(g + 1) * hb, :], pos, 2 ** 30) < cand_sc[g * hb:(g + 1) * hb, :]

        tu_sc[...] = jnp.zeros(tu_sc.shape, I32)

        def cut_pass(p, _):
            bit = jnp.left_shift(jnp.int32(1), INDEX_BITS - 1 - p)
            cand_sc[...] = tu_sc[...] | bit
            tu_sc[...] = jnp.where(count_rows(hit_tie_below) < aux_sc[...], tu_sc[...] | bit, tu_sc[...])
            return 0

        lax.fori_loop(0, INDEX_BITS, cut_pass, 0)

    return any_tied


def _dsa_prompt_kernel(iq_ref, small_ref, aq_ref, qpos_ref, ik_ref, k_ref, v_ref, kpos_ref, o_ref,
                       qidx_sc, wrep_sc, qa_sc, s_sc, keys_sc, lg_sc, p_sc, alpha_sc, m_sc, l_sc, acc_sc,
                       cand_sc, t_sc, tu_sc, aux_sc, *, tq, tk, topk, rb, hb):
    i = pl.program_id(1)
    q0 = i * tq
    nchunk = (i + 1) * (tq // tk)
    ncol = tk // LANES
    nrb = tq // rb
    lane_hb = lax.broadcasted_iota(I32, (hb, LANES), 1)
    lane_minus_row = lax.broadcasted_iota(I32, (rb, LANES), 1) - lax.broadcasted_iota(I32, (rb, LANES), 0)

    for h in range(H_IDX):
        qidx_sc[h * tq:(h + 1) * tq, :] = iq_ref[:, h * D_IDX:(h + 1) * D_IDX]
        wrep_sc[h] = jnp.broadcast_to(small_ref[:, h:h + 1], (tq, LANES))
    for hh in range(H_A):
        qa_sc[hh * tq:(hh + 1) * tq, 0:DH_A] = aq_ref[:, hh * DH_A:(hh + 1) * DH_A]
        qa_sc[hh * tq:(hh + 1) * tq, DH_A:2 * DH_A] = qpos_ref[0, hh * tq:(hh + 1) * tq, :]

    def score_body(c, _):
        ikc = ik_ref[pl.ds(pl.multiple_of(c * tk, tk), tk), :]
        s_sc[...] = _dot_nt(qidx_sc[...], ikc)
        bound = q0 - c * tk
        for r in range(nrb):
            for j in range(ncol):
                acc = jnp.zeros((rb, LANES), F32)
                for h in range(H_IDX):
                    s = s_sc[h * tq + r * rb:h * tq + (r + 1) * rb, j * LANES:(j + 1) * LANES]
                    acc = acc + wrep_sc[h, r * rb:(r + 1) * rb, :] * jnp.maximum(s, 0.0)
                acc = jnp.where(acc == 0.0, 0.0, acc)
                acc = jnp.where(lane_minus_row <= bound + (r * rb - j * LANES), acc, -jnp.inf)
                keys_sc[c, r * rb:(r + 1) * rb, j * LANES:(j + 1) * LANES] = _sort_key(acc)
        return 0

    lax.fori_loop(0, nchunk, score_body, 0)

    any_tied = _topk_search(keys_sc, cand_sc, t_sc, tu_sc, aux_sc,
                            nchunk=nchunk, rows=tq, tk=tk, hb=hb, topk=topk)

    @pl.when(any_tied)
    def _():
        def drop_body(c, _):
            for g in range(tq // hb):
                tb = t_sc[g * hb:(g + 1) * hb, :]
                xb = tu_sc[g * hb:(g + 1) * hb, :]
                for j in range(ncol):
                    kj = keys_sc[c, g * hb:(g + 1) * hb, j * LANES:(j + 1) * LANES]
                    pos = c * tk + j * LANES + lane_hb
                    late = jnp.where(kj == tb, pos, -1) > xb
                    keys_sc[c, g * hb:(g + 1) * hb, j * LANES:(j + 1) * LANES] = jnp.where(late, kj - 1, kj)
            return 0

        lax.fori_loop(0, nchunk, drop_body, 0)

    t_sc[...] = jnp.maximum(t_sc[...], NEGINF_KEY + 1)
    m_sc[...] = jnp.full(m_sc.shape, NEG, F32)
    l_sc[...] = jnp.zeros(l_sc.shape, F32)
    acc_sc[...] = jnp.zeros(acc_sc.shape, F32)

    def att_body(c, _):
        start = pl.multiple_of(c * tk, tk)
        kaug = jnp.concatenate([k_ref[pl.ds(start, tk), :], kpos_ref[pl.ds(start, tk), :]], axis=1)
        lg_sc[...] = _dot_nt(qa_sc[...], kaug)
        for r in range(nrb):
            tb = t_sc[r * rb:(r + 1) * rb, :]
            selb = [jnp.where(keys_sc[c, r * rb:(r + 1) * rb, j * LANES:(j + 1) * LANES] >= tb, 0.0, NEG)
                    for j in range(ncol)]
            for hh in range(H_A):
                rows = slice(hh * tq + r * rb, hh * tq + (r + 1) * rb)
                lgs = [lg_sc[rows, j * LANES:(j + 1) * LANES] + selb[j] for j in range(ncol)]
                mx = lgs[0]
                for j in range(1, ncol):
                    mx = jnp.maximum(mx, lgs[j])
                m_old = m_sc[rows, :]
                m_new = jnp.maximum(m_old, jnp.max(mx, axis=1, keepdims=True))
                alpha = jnp.exp(m_old - m_new)
                ps = [jnp.exp(lg - m_new) for lg in lgs]
                psum = ps[0]
                for j in range(1, ncol):
                    psum = psum + ps[j]
                l_sc[rows, :] = alpha * l_sc[rows, :] + psum
                for j in range(ncol):
                    p_sc[rows, j * LANES:(j + 1) * LANES] = ps[j].astype(BF16)
                alpha_sc[rows, :] = alpha
                m_sc[rows, :] = m_new
        acc_sc[...] = alpha_sc[...] * acc_sc[...] + _dot(p_sc[...], v_ref[pl.ds(start, tk), :])
        return 0

    lax.fori_loop(0, nchunk, att_body, 0)
    for hh in range(H_A):
        rows = slice(hh * tq, (hh + 1) * tq)
        l = jnp.sum(l_sc[rows, :], axis=1, keepdims=True)
        o_ref[:, hh * DH_A:(hh + 1) * DH_A] = (acc_sc[rows, :] / l).astype(BF16)


def _alibi_tables(seq, tq):
    pos = np.arange(seq)
    hi, lo = (pos // 64).astype(np.float32), (pos % 64).astype(np.float32)
    ktab = np.zeros((seq, DH_A), np.float32)
    ktab[:, 0], ktab[:, 1], ktab[:, 2], ktab[:, 3] = 64.0 * hi, lo, 1.0, 1.0
    nq = seq // tq
    qtab = np.zeros((nq, H_A, tq, DH_A), np.float32)
    for hh, slope in enumerate(ALIBI_SLOPES):
        qtab[:, hh, :, 0] = slope
        qtab[:, hh, :, 1] = slope
        qtab[:, hh, :, 2] = (-slope * 64.0 * hi).reshape(nq, tq)
        qtab[:, hh, :, 3] = (-slope * lo).reshape(nq, tq)
    return jnp.asarray(qtab.reshape(nq, H_A * tq, DH_A), BF16), jnp.asarray(ktab, BF16)


def _dsa_prompt(iq, small, aq, ikb, kb, vb, nb, seq, tq):
    n = iq.shape[0]
    nq = seq // tq
    topk = min(TOPK_MAX, seq // 4)
    tk = min(256, tq)
    rb = min(64, tq)
    hb = min(128, tq)
    assert seq // LANES <= 256
    qtab, ktab = _alibi_tables(seq, tq)

    def tok(w):
        return pl.BlockSpec((tq, w), lambda b, i: (b * nq + i, 0))

    def per_b(w):
        return pl.BlockSpec((seq, w), lambda b, i: (b, 0))

    return pl.pallas_call(
        functools.partial(_dsa_prompt_kernel, tq=tq, tk=tk, topk=topk, rb=rb, hb=hb),
        grid=(nb, nq),
        in_specs=[tok(512), tok(16), tok(512),
                  pl.BlockSpec((1, H_A * tq, DH_A), lambda b, i: (i, 0, 0)),
                  per_b(D_IDX), per_b(DH_A), per_b(DH_A),
                  pl.BlockSpec((seq, DH_A), lambda b, i: (0, 0))],
        out_specs=tok(512),
        out_shape=jax.ShapeDtypeStruct((n, 512), BF16),
        scratch_shapes=[pltpu.VMEM((H_IDX * tq, D_IDX), BF16),
                        pltpu.VMEM((H_IDX, tq, LANES), F32),
                        pltpu.VMEM((H_A * tq, 2 * DH_A), BF16),
                        pltpu.VMEM((H_IDX * tq, tk), F32),
                        pltpu.VMEM((seq // tk, tq, tk), I32),
                        pltpu.VMEM((H_A * tq, tk), F32),
                        pltpu.VMEM((H_A * tq, tk), BF16),
                        pltpu.VMEM((H_A * tq, LANES), F32),
                        pltpu.VMEM((H_A * tq, LANES), F32),
                        pltpu.VMEM((H_A * tq, LANES), F32),
                        pltpu.VMEM((H_A * tq, DH_A), F32),
                        pltpu.VMEM((tq, LANES), I32),
                        pltpu.VMEM((tq, LANES), I32),
                        pltpu.VMEM((tq, LANES), I32),
                        pltpu.VMEM((tq, LANES), F32)],
        compiler_params=_cparams(("parallel", "arbitrary")),
        name="dsa_prompt",
    )(iq, small, aq, qtab, ikb, kb, vb, ktab)


def _page_pipeline(pt_ref, copies_of, n_pages):
    b = pl.program_id(0)
    slot = b % 2

    def for_all(bb, sl, act):
        def body(j, _):
            for cp in copies_of(pt_ref[bb, j], sl, j):
                act(cp)
            return 0
        lax.fori_loop(0, n_pages, body, 0)

    @pl.when(b == 0)
    def _():
        for_all(0, 0, lambda cp: cp.start())

    @pl.when(b + 1 < pl.num_programs(0))
    def _():
        for_all(b + 1, 1 - slot, lambda cp: cp.start())

    for_all(b, slot, lambda cp: cp.wait())
    return slot


def _sample_scores_kernel(pt_ref, iq_ref, w_ref, iknew_ref, ckidx_hbm, keys_ref, ibuf, sems,
                          *, n_pages, n_new, cw):
    past = n_pages * PAGE
    total = past + LANES

    def copies_of(pg, sl, j):
        dst = pl.ds(pl.multiple_of(j * PAGE, PAGE), PAGE)
        return (pltpu.make_async_copy(ckidx_hbm.at[pg], ibuf.at[sl, :, dst], sems.at[sl]),)

    slot = _page_pipeline(pt_ref, copies_of, n_pages)

    iq = iq_ref[0]
    w = w_ref[0]

    def scores(dots):
        s = jnp.maximum(dots, 0.0) * w
        s = jnp.sum(s.reshape(n_new, H_IDX, s.shape[-1]), axis=1)
        return jnp.where(s == 0.0, 0.0, s)

    for ch in range(past // cw):
        sc = scores(_dot(iq, ibuf[slot, :, ch * cw:(ch + 1) * cw].astype(BF16)))
        keys_ref[0, :, ch * cw:(ch + 1) * cw] = _sort_key(sc)
    sc = scores(_dot_nt(iq, iknew_ref[0]))
    t_i = lax.broadcasted_iota(I32, (n_new, LANES), 0)
    j_i = lax.broadcasted_iota(I32, (n_new, LANES), 1)
    keys_ref[0, :, past:total] = _sort_key(jnp.where(j_i <= t_i, sc, -jnp.inf))


def _sample_search_kernel(keys_ref, t_ref, x_ref, cand_sc, aux_sc, *, nchunk, rows, topk):
    _topk_search(keys_ref, cand_sc, t_ref, x_ref, aux_sc, nchunk=nchunk, rows=rows, tk=LANES, hb=rows, topk=topk)


def _sample_attend_kernel(pt_ref, q_ref, knew_ref, vnew_ref, keys_ref, t_ref, x_ref, ck_hbm, cv_hbm, o_ref,
                          kbuf, vbuf, sems, lg_sc, *, n_pages, n_new, cw):
    past = n_pages * PAGE
    total = past + LANES
    rows_q = n_new * H_A

    def copies_of(pg, sl, j):
        dst = pl.ds(pl.multiple_of(j * PAGE, PAGE), PAGE)
        return (pltpu.make_async_copy(ck_hbm.at[pg], kbuf.at[sl, dst, :], sems.at[0, sl]),
                pltpu.make_async_copy(cv_hbm.at[pg], vbuf.at[sl, dst, :], sems.at[1, sl]))

    slot = _page_pipeline(pt_ref, copies_of, n_pages)

    r_i = lax.broadcasted_iota(I32, (rows_q, 1), 0)
    r_t = r_i // H_A
    r_h = r_i % H_A
    slope = jnp.zeros((rows_q, 1), F32)
    for hh in range(H_A):
        slope = jnp.where(r_h == hh, ALIBI_SLOPES[hh], slope)
    qposf = (past + r_t).astype(F32)
    q = q_ref[0]

    def sel_bias(lo, width):
        pos = lo + lax.broadcasted_iota(I32, (1, width), 1)
        out = jnp.full((rows_q, width), NEG, F32)
        for t in range(n_new):
            kt = keys_ref[0, t:t + 1, lo:lo + width]
            thr = t_ref[0, t:t + 1, 0:1]
            keep_tie = jnp.logical_and(kt == thr, pos <= x_ref[0, t:t + 1, 0:1])
            chosen = jnp.logical_and(jnp.logical_or(kt > thr, keep_tie), kt > NEGINF_KEY)
            bias_t = jnp.where(chosen, 0.0, NEG)
            out = jnp.where(r_t == t, bias_t, out)
        return out

    def logits(k_chunk, lo, width):
        kposf = (lo + lax.broadcasted_iota(I32, (1, width), 1)).astype(F32)
        sel = sel_bias(lo, width)
        lg = _dot_nt(q, k_chunk) - slope * (qposf - kposf)
        return lg + sel

    for ch in range(past // cw):
        lg_sc[:, ch * cw:(ch + 1) * cw] = logits(kbuf[slot, ch * cw:(ch + 1) * cw, :].astype(BF16), ch * cw, cw)
    lg_sc[:, past:total] = logits(knew_ref[0], past, LANES)

    lg = lg_sc[...]
    m = jnp.max(lg, axis=1, keepdims=True)
    p = jnp.exp(lg - m)
    l = jnp.sum(p, axis=1, keepdims=True)
    pb = p.astype(BF16)
    acc = _dot(pb[:, past:total], vnew_ref[0])
    for ch in range(past // cw):
        acc = acc + _dot(pb[:, ch * cw:(ch + 1) * cw], vbuf[slot, ch * cw:(ch + 1) * cw, :].astype(BF16))
    o_ref[0] = (acc / l).astype(BF16)


def _dsa_sample(page_table, iq32, w32, q16, iknew, knew, vnew, cache_kidx, cache_k, cache_v, n_new):
    db, n_pages = page_table.shape
    past = n_pages * PAGE
    total = past + LANES
    topk = min(TOPK_MAX, (past + n_new) // 4)
    cw = 1024 if past % 1024 == 0 else PAGE
    rows_q = n_new * H_A

    n_rows = db * n_new
    nchunk = total // LANES
    srows = _pick_tile(n_rows, 128)

    def per_b(a):
        return pl.BlockSpec((1,) + a.shape[1:], lambda b, pt: (b, 0, 0))

    hbm = pl.BlockSpec(memory_space=pl.ANY)

    keys = pl.pallas_call(
        functools.partial(_sample_scores_kernel, n_pages=n_pages, n_new=n_new, cw=cw),
        grid_spec=pltpu.PrefetchScalarGridSpec(
            num_scalar_prefetch=1,
            grid=(db,),
            in_specs=[per_b(iq32), per_b(w32), per_b(iknew), hbm],
            out_specs=pl.BlockSpec((1, n_new, total), lambda b, pt: (b, 0, 0)),
            scratch_shapes=[pltpu.VMEM((2, D_IDX, past), F32), pltpu.SemaphoreType.DMA((2,))],
        ),
        out_shape=jax.ShapeDtypeStruct((db, n_new, total), I32),
        compiler_params=_cparams(("arbitrary",)),
        name="sample_scores",
    )(page_table, iq32, w32, iknew, cache_kidx)

    keys_cm = keys.reshape(n_rows, nchunk, LANES).transpose(1, 0, 2)
    thr, cut = pl.pallas_call(
        functools.partial(_sample_search_kernel, nchunk=nchunk, rows=srows, topk=topk),
        grid=(n_rows // srows,),
        in_specs=[pl.BlockSpec((nchunk, srows, LANES), lambda i: (0, i, 0))],
        out_specs=[pl.BlockSpec((srows, LANES), lambda i: (i, 0))] * 2,
        out_shape=[jax.ShapeDtypeStruct((n_rows, LANES), I32)] * 2,
        scratch_shapes=[pltpu.VMEM((srows, LANES), I32), pltpu.VMEM((srows, LANES), F32)],
        compiler_params=_cparams(("parallel",)),
        name="sample_search",
    )(keys_cm)

    thr3 = thr.reshape(db, n_new, LANES)
    cut3 = cut.reshape(db, n_new, LANES)
    return pl.pallas_call(
        functools.partial(_sample_attend_kernel, n_pages=n_pages, n_new=n_new, cw=cw),
        grid_spec=pltpu.PrefetchScalarGridSpec(
            num_scalar_prefetch=1,
            grid=(db,),
            in_specs=[per_b(q16), per_b(knew), per_b(vnew), per_b(keys), per_b(thr3), per_b(cut3), hbm, hbm],
            out_specs=pl.BlockSpec((1, rows_q, DH_A), lambda b, pt: (b, 0, 0)),
            scratch_shapes=[pltpu.VMEM((2, past, DH_A), F32), pltpu.VMEM((2, past, DH_A), F32),
                            pltpu.SemaphoreType.DMA((2, 2)), pltpu.VMEM((rows_q, total), F32)],
        ),
        out_shape=jax.ShapeDtypeStruct((db, rows_q, DH_A), BF16),
        compiler_params=_cparams(("arbitrary",)),
        name="sample_attend",
    )(page_table, q16, knew, vnew, keys, thr3, cut3, cache_k, cache_v)


def _outffn_kernel(x_ref, hm_ref, ha_ref, mod_ref, g2_ref, wom_ref, woa_ref, wg_ref, wu_ref, wd_ref, y_ref):
    mix = _dot(hm_ref[...], wom_ref[...]) + _dot(ha_ref[...], woa_ref[...])
    x1 = x_ref[...] + mod_ref[0, 0] * mix
    xn = x1 * lax.rsqrt(jnp.mean(x1 * x1, axis=-1, keepdims=True) + EPS) * g2_ref[...]
    hb = (xn * (1.0 + mod_ref[2, 0]) + mod_ref[1, 0]).astype(BF16)
    g = _dot(hb, wg_ref[...])
    u = _dot(hb, wu_ref[...])
    act = (g * _sigmoid(g) * u).astype(BF16)
    y_ref[...] = x1 + mod_ref[3, 0] * _dot(act, wd_ref[...])


def _outffn(x, hm, ha, mod_out, per_token_mod, rows_per_mod, g2, wom, woa, wg, wu, wd, tm):
    n, d = x.shape
    if per_token_mod:
        mod_spec = pl.BlockSpec((4, 1, tm, d), lambda i: (0, 0, i, 0))
    else:
        tiles = rows_per_mod // tm
        mod_spec = pl.BlockSpec((4, 1, 1, d), lambda i: (0, i // tiles, 0, 0))

    def full(a):
        return pl.BlockSpec(a.shape, lambda i: (0,) * a.ndim, pipeline_mode=pl.Buffered(1))

    def rows(w):
        return pl.BlockSpec((tm, w), lambda i: (i, 0))

    return pl.pallas_call(
        _outffn_kernel,
        grid=(n // tm,),
        in_specs=[rows(d), rows(512), rows(512), mod_spec, full(g2), full(wom), full(woa),
                  full(wg), full(wu), full(wd)],
        out_specs=rows(d),
        out_shape=jax.ShapeDtypeStruct((n, d), F32),
        compiler_params=_cparams(("parallel",)),
        name="outffn",
    )(x, hm, ha, mod_out, g2, wom, woa, wg, wu, wd)


def _pick_tile(n, pref):
    t = pref
    while n % t:
        t //= 2
    return t


def kernel(x_prompt, x_sample, cache_k, cache_v, cache_kidx, state_C, state_n, state_m, page_table,
           c_prompt, c_sample, w_ada, b_ada, g_norm1, w_in, b_igate, b_fgate, mlstm_norm_g,
           q_norm_g, k_norm_g, w_out, g_norm2, w_gate, w_up, w_down):
    bp, seq, d = x_prompt.shape
    db, t_new, _ = x_sample.shape
    n_p, n_s = bp * seq, db * t_new

    o = np.cumsum([0, 512, 512, 512, 512, H_M, H_M, 512, DH_A, DH_A, H_IDX * D_IDX, D_IDX, H_IDX])
    mq, mk, mv, mo, mi, mf, aq, ak, av, iq, ik, iw = [w_in[:, int(o[j]):int(o[j + 1])] for j in range(12)]
    zpad = lambda wdt: jnp.zeros((d, wdt), w_in.dtype)
    w_r = jnp.concatenate([mq, mk, mv, mo, aq, ak, av, iq, ik, zpad(LANES - D_IDX),
                           iw, mi, mf, zpad(LANES - 16)], axis=1).astype(BF16)
    wgt = jnp.concatenate([mi, mf], axis=1).T.astype(BF16)
    bsm = jnp.concatenate([jnp.zeros((H_IDX,), F32), b_igate, b_fgate]).reshape(1, 16)
    brow = jnp.concatenate([b_igate, b_fgate]).reshape(8, 1)
    g1 = g_norm1.reshape(1, d)
    g2 = g_norm2.reshape(1, d)
    qg = q_norm_g.reshape(1, DH_A)
    kg = k_norm_g.reshape(1, DH_A)
    ng = mlstm_norm_g.reshape(1, H_M * DH_M)
    wom = w_out[0:H_M * DH_M].astype(BF16)
    woa = w_out[H_M * DH_M:].astype(BF16)
    wg = w_gate.astype(BF16)
    wu = w_up.astype(BF16)
    wd = w_down.astype(BF16)

    mod = _ada(jnp.concatenate([c_prompt, c_sample], axis=0), w_ada.astype(BF16), b_ada)
    mod_p = mod[:bp].reshape(bp, 6, 1, d).transpose(1, 0, 2, 3)
    mod_s = jnp.repeat(mod[bp:].reshape(db, 6, d), t_new, axis=0)
    mod_s = mod_s.transpose(1, 0, 2).reshape(6, 1, n_s, d)

    tm_p = _pick_tile(seq, 512)
    (mqkv, mo_p, aq_p, k_p, v_p, kidx_p, kb, vb, ikb, iq_p, small_p, grow_p) = _inproj(
        x_prompt.reshape(n_p, d), mod_p[0:2], False, seq, g1, w_r, wgt, bsm, brow, qg, kg, tm_p)

    lc = _pick_tile(seq, 256)
    grow3 = grow_p.reshape(8, n_p // lc, lc).transpose(1, 0, 2)
    hm_p, C_p, n_pst, m_pb = _mlstm(
        mqkv, small_p, grow3, mo_p, ng,
        jnp.zeros((bp, H_M, DH_M, DH_M), F32), jnp.zeros((bp, H_M, DH_M), F32),
        jnp.zeros((bp, H_M, LANES), F32), bp, lc)

    tq = _pick_tile(seq, 512)
    ha_p = _dsa_prompt(iq_p, small_p, aq_p, ikb, kb, vb, bp, seq, tq)

    y_p = _outffn(x_prompt.reshape(n_p, d), hm_p, ha_p, mod_p[2:6], False, seq, g2,
                  wom, woa, wg, wu, wd, tm_p)

    tm_s = _pick_tile(n_s, 128)
    (mqkv_s, mo_s, aq_s, k_s, v_s, kidx_s, kb_s, vb_s, ikb_s, iq_s, small_s, grow_s) = _inproj(
        x_sample.reshape(n_s, d), mod_s[0:2], True, 0, g1, w_r, wgt, bsm, brow, qg, kg, tm_s)

    lp = 16
    pad_tok = lambda a: jnp.pad(a.reshape(db, t_new, a.shape[-1]),
                                ((0, 0), (0, lp - t_new), (0, 0))).reshape(db * lp, a.shape[-1])
    gate_pad = jnp.concatenate([jnp.zeros((H_IDX,), F32), jnp.full((H_M,), -jnp.inf, F32),
                                jnp.zeros((H_M,), F32)])
    small_pad = jnp.concatenate(
        [small_s.reshape(db, t_new, 16), jnp.broadcast_to(gate_pad, (db, lp - t_new, 16))], axis=1
    ).reshape(db * lp, 16)
    grow_pad = jnp.concatenate(
        [grow_s.reshape(8, db, t_new),
         jnp.broadcast_to(gate_pad[H_IDX:].reshape(8, 1, 1), (8, db, lp - t_new))], axis=2
    ).transpose(1, 0, 2)
    hm_s_pad, C_s, n_sst, m_sb = _mlstm(
        pad_tok(mqkv_s), small_pad, grow_pad, pad_tok(mo_s), ng,
        state_C, state_n, jnp.broadcast_to(state_m[:, :, None], (db, H_M, LANES)), db, lp)
    hm_s = hm_s_pad.reshape(db, lp, H_M * DH_M)[:, :t_new].reshape(n_s, H_M * DH_M)

    pad_rows = lambda a: jnp.pad(a.reshape(db, t_new, a.shape[-1]), ((0, 0), (0, LANES - t_new), (0, 0)))
    ha_s = _dsa_sample(
        page_table, iq_s.reshape(db, t_new * H_IDX, D_IDX),
        small_s[:, 0:H_IDX].reshape(db, t_new * H_IDX, 1),
        aq_s.reshape(db, t_new * H_A, DH_A),
        pad_rows(ikb_s), pad_rows(kb_s), pad_rows(vb_s),
        jnp.swapaxes(cache_kidx, 1, 2), cache_k, cache_v, t_new).reshape(n_s, H_A * DH_A)

    y_s = _outffn(x_sample.reshape(n_s, d), hm_s, ha_s, mod_s[2:6], True, 0, g2,
                  wom, woa, wg, wu, wd, tm_s)

    return (y_p.reshape(bp, seq, d), y_s.reshape(db, t_new, d),
            k_p.reshape(bp, seq, DH_A), v_p.reshape(bp, seq, DH_A), kidx_p.reshape(bp, seq, D_IDX),
            C_p, n_pst, m_pb[:, :, 0],
            k_s.reshape(db, t_new, DH_A), v_s.reshape(db, t_new, DH_A), kidx_s.reshape(db, t_new, D_IDX),
            C_s, n_sst, m_sb[:, :, 0])
```

```python
import functools

import jax
import jax.numpy as jnp
import numpy as np
from jax import lax
from jax.experimental import pallas as pl
from jax.experimental.pallas import tpu as pltpu

F32 = jnp.float32
BF16 = jnp.bfloat16
I32 = jnp.int32

H_M = 4
DH_M = 128
H_A = 4
DH_A = 128
H_IDX = 8
D_IDX = 64
TOPK_MAX = 256
PAGE = 128
EPS = 1e-6
INDEX_SCALE = D_IDX ** -0.5 * H_IDX ** -0.5
ALIBI_SLOPES = tuple(float(2.0 ** (-8.0 * (h + 1) / H_A)) for h in range(H_A))

LANES = 128
VMEM_LIMIT = 56 * 1024 * 1024
NEG = -1e30
I32_MIN = -2 ** 31
F32_TINY = float(np.finfo(np.float32).tiny)
NEGINF_KEY = -2139095041
INDEX_BITS = 14

C_MQ, C_MK, C_MV, C_MO, C_AQ, C_AK, C_AV, C_IQ, C_IK, C_SM, C_END = (
    0, 512, 1024, 1536, 2048, 2560, 2688, 2816, 3328, 3456, 3584)


def _cparams(sem):
    return pltpu.CompilerParams(dimension_semantics=sem, vmem_limit_bytes=VMEM_LIMIT)


def _sigmoid(x):
    return 1.0 / (1.0 + jnp.exp(-x))


def _log_sigmoid(x):
    return jnp.minimum(x, 0.0) - jnp.log1p(jnp.exp(-jnp.abs(x)))


def _dot(a, b):
    return jnp.dot(a, b, preferred_element_type=F32)


def _dot_nt(a, b):
    return lax.dot_general(a, b, (((1,), (1,)), ((), ())), preferred_element_type=F32)


def _dot_tn(a, b):
    return lax.dot_general(a, b, (((0,), (0,)), ((), ())), preferred_element_type=F32)


def _split3(x):
    hi = x.astype(BF16)
    r1 = x - hi.astype(F32)
    mid = r1.astype(BF16)
    lo = (r1 - mid.astype(F32)).astype(BF16)
    return hi, mid, lo


def _sort_key(x):
    b = pltpu.bitcast(x, I32)
    return b ^ ((b >> 31) & 0x7FFFFFFF)


def _score_key(score, pos):
    return jnp.where(jnp.abs(score) < F32_TINY, -pos, _sort_key(score))


def _ada_kernel(c_ref, w_ref, b_ref, o_ref):
    c = c_ref[...]
    s = c * _sigmoid(c)
    o_ref[...] = _dot(s.astype(BF16), w_ref[...]) + b_ref[...]


def _ada(c, w_bf, b):
    r, d = c.shape
    n = w_bf.shape[1]
    return pl.pallas_call(
        _ada_kernel,
        grid=(n // d,),
        in_specs=[pl.BlockSpec((r, d), lambda j: (0, 0)),
                  pl.BlockSpec((d, d), lambda j: (0, j)),
                  pl.BlockSpec((1, d), lambda j: (0, j))],
        out_specs=pl.BlockSpec((r, d), lambda j: (0, j)),
        out_shape=jax.ShapeDtypeStruct((r, n), F32),
        compiler_params=_cparams(("arbitrary",)),
        name="ada",
    )(c, w_bf, b.reshape(1, n))


def _inproj_kernel(x_ref, mod_ref, g1_ref, w_ref, wgt_ref, bsm_ref, brow_ref, qg_ref, kg_ref,
                   mqkv_ref, mo_ref, aq_ref, k_ref, v_ref, kidx_ref, kb_ref, vb_ref, ikb_ref,
                   iq_ref, small_ref, grow_ref):
    x = x_ref[...]
    xn = x * lax.rsqrt(jnp.mean(x * x, axis=-1, keepdims=True) + EPS) * g1_ref[...]
    h = xn * (1.0 + mod_ref[1, 0]) + mod_ref[0, 0]
    hb = h.astype(BF16)

    def sec(a, b):
        return _dot(hb, w_ref[:, a:b])

    mqkv_ref[:, 0:512] = sec(C_MQ, C_MK).astype(BF16)
    mqkv_ref[:, 512:1024] = (sec(C_MK, C_MV) * (DH_M ** -0.5)).astype(BF16)
    mqkv_ref[:, 1024:1536] = sec(C_MV, C_MO).astype(BF16)
    mo_ref[...] = sec(C_MO, C_AQ)

    aq = sec(C_AQ, C_AK)
    qg = qg_ref[...]
    for hh in range(H_A):
        a = aq[:, hh * DH_A:(hh + 1) * DH_A]
        a = a * lax.rsqrt(jnp.mean(a * a, axis=-1, keepdims=True) + EPS) * qg
        aq_ref[:, hh * DH_A:(hh + 1) * DH_A] = (a * (DH_A ** -0.5)).astype(BF16)

    ak = sec(C_AK, C_AV)
    ak = ak * lax.rsqrt(jnp.mean(ak * ak, axis=-1, keepdims=True) + EPS) * kg_ref[...]
    k_ref[...] = ak
    kb_ref[...] = ak.astype(BF16)
    av = sec(C_AV, C_IQ)
    v_ref[...] = av
    vb_ref[...] = av.astype(BF16)

    iq_ref[...] = sec(C_IQ, C_IK).astype(BF16)
    ik = sec(C_IK, C_SM)[:, 0:D_IDX]
    kidx_ref[...] = ik
    ikb_ref[...] = ik.astype(BF16)

    sm = sec(C_SM, C_END)[:, 0:16] + bsm_ref[...]
    col = lax.broadcasted_iota(I32, sm.shape, 1)
    small_ref[...] = jnp.where(col < H_IDX, sm * INDEX_SCALE,
                               jnp.where(col < H_IDX + H_M, sm, _log_sigmoid(sm)))

    gr = _dot_nt(wgt_ref[...], hb) + brow_ref[...]
    row = lax.broadcasted_iota(I32, gr.shape, 0)
    grow_ref[...] = jnp.where(row < H_M, gr, _log_sigmoid(gr))


def _inproj(x, mod_in, per_token_mod, rows_per_mod, g1, w_r, wgt, bsm, brow, qg, kg, tm):
    n, d = x.shape
    if per_token_mod:
        mod_spec = pl.BlockSpec((2, 1, tm, d), lambda i: (0, 0, i, 0))
    else:
        tiles = rows_per_mod // tm
        mod_spec = pl.BlockSpec((2, 1, 1, d), lambda i: (0, i // tiles, 0, 0))

    def full(a):
        return pl.BlockSpec(a.shape, lambda i: (0,) * a.ndim)

    def rows(w):
        return pl.BlockSpec((tm, w), lambda i: (i, 0))

    outs = [(1536, BF16), (512, F32), (512, BF16), (128, F32), (128, F32), (D_IDX, F32),
            (128, BF16), (128, BF16), (D_IDX, BF16), (512, BF16), (16, F32)]
    out_shape = [jax.ShapeDtypeStruct((n, w), dt) for w, dt in outs]
    out_specs = [rows(w) for w, _ in outs]
    out_shape.append(jax.ShapeDtypeStruct((8, n), F32))
    out_specs.append(pl.BlockSpec((8, tm), lambda i: (0, i)))
    return pl.pallas_call(
        _inproj_kernel,
        grid=(n // tm,),
        in_specs=[rows(d), mod_spec, full(g1), full(w_r), full(wgt), full(bsm), full(brow),
                  full(qg), full(kg)],
        out_specs=out_specs,
        out_shape=out_shape,
        compiler_params=_cparams(("parallel",)),
        name="inproj",
    )(x, mod_in, g1, w_r, wgt, bsm, brow, qg, kg)


def _mlstm_kernel(q_ref, k_ref, v_ref, small_ref, grow_ref, o_ref, ng_ref, c0_ref, n0_ref, m0_ref,
                  h_ref, c_out_ref, n_out_ref, m_out_ref, state_sc, m_sc, *, L, nc):
    c = pl.program_id(1)

    @pl.when(c == 0)
    def _():
        state_sc[...] = jnp.zeros(state_sc.shape, F32)
        for hh in range(H_M):
            state_sc[hh, 0:DH_M, :] = c0_ref[0, hh]
            state_sc[hh, DH_M:DH_M + 1, :] = n0_ref[0, hh:hh + 1, :]
        m_sc[...] = jnp.zeros(m_sc.shape, F32)
        m_sc[0:H_M, :] = m0_ref[0]

    ti = lax.broadcasted_iota(I32, (L, L), 0)
    si = lax.broadcasted_iota(I32, (L, L), 1)
    causal = si <= ti
    tri = jnp.where(causal, 1.0, 0.0).astype(BF16)
    tri_t = jnp.where(ti <= si, 1.0, 0.0).astype(BF16)

    sm = small_ref[...]
    col = lax.broadcasted_iota(I32, sm.shape, 1)
    lf_cols = jnp.where(col >= H_IDX + H_M, sm, 0.0)
    hi, mid, lo = _split3(lf_cols)
    b_cols = _dot(tri, hi) + _dot(tri, mid) + _dot(tri, lo)
    gr = grow_ref[0]
    row = lax.broadcasted_iota(I32, gr.shape, 0)
    lf_rows = jnp.where(row >= H_M, gr, 0.0)
    hi, mid, lo = _split3(lf_rows)
    b_rows = _dot(hi, tri_t) + _dot(mid, tri_t) + _dot(lo, tri_t)

    lane = lax.broadcasted_iota(I32, (L, DH_M), 1)
    ones_col = jnp.where(lane == 0, 1.0, 0.0).astype(BF16)

    for hh in range(H_M):
        q = q_ref[:, hh * DH_M:(hh + 1) * DH_M]
        k = k_ref[:, hh * DH_M:(hh + 1) * DH_M]
        v = v_ref[:, hh * DH_M:(hh + 1) * DH_M]
        ig_c = sm[:, H_IDX + hh:H_IDX + hh + 1]
        b_c = b_cols[:, H_IDX + H_M + hh:H_IDX + H_M + hh + 1]
        ig_r = gr[hh:hh + 1, :]
        b_r = b_rows[H_M + hh:H_M + hh + 1, :]
        m_prev = m_sc[hh:hh + 1, 0:1]

        dmat = jnp.where(causal, b_c - b_r + ig_r, -jnp.inf)
        inter = b_c + m_prev
        m_t = jnp.maximum(inter, jnp.max(dmat, axis=-1, keepdims=True))
        w = jnp.exp(dmat - m_t)
        g = jnp.exp(inter - m_t)
        a = w * _dot_nt(q, k)
        st = state_sc[hh]
        qs = _dot_nt(q, st.astype(BF16))
        num = g * qs[:, 0:DH_M] + _dot(a.astype(BF16), v)
        den = g * qs[:, DH_M:DH_M + 1] + jnp.sum(a, axis=-1, keepdims=True)
        hv = num / jnp.maximum(jnp.abs(den), jnp.exp(-m_t))

        m_new = m_t[L - 1:L, :]
        b_last = b_c[L - 1:L, :]
        wend = jnp.exp(b_last - b_c + ig_c - m_new)
        gend = jnp.exp(b_last + m_prev - m_new)
        kw = (k.astype(F32) * wend).astype(BF16)
        v_aug = jnp.concatenate([v, ones_col], axis=1)
        state_sc[hh] = gend * st + _dot_tn(v_aug, kw)
        m_sc[hh:hh + 1, :] = jnp.broadcast_to(m_new, (1, LANES))

        hn = hv * lax.rsqrt(jnp.mean(hv * hv, axis=-1, keepdims=True) + EPS)
        hn = hn * ng_ref[:, hh * DH_M:(hh + 1) * DH_M]
        hn = hn * _sigmoid(o_ref[:, hh * DH_M:(hh + 1) * DH_M])
        h_ref[:, hh * DH_M:(hh + 1) * DH_M] = hn.astype(BF16)

    @pl.when(c == nc - 1)
    def _():
        for hh in range(H_M):
            c_out_ref[0, hh] = state_sc[hh, 0:DH_M, :]
            n_out_ref[0, hh:hh + 1, :] = state_sc[hh, DH_M:DH_M + 1, :]
        m_out_ref[0] = m_sc[0:H_M, :]


def _mlstm(mqkv, small, grow3, mo, ng, c0, n0, m0b, nb, L):
    n = mqkv.shape[0]
    nc = n // nb // L
    d = H_M * DH_M

    def tok(w, blk=0):
        return pl.BlockSpec((L, w), lambda b, c, blk=blk: (b * nc + c, blk))

    per_b3 = pl.BlockSpec((1, H_M, LANES), lambda b, c: (b, 0, 0))
    per_b4 = pl.BlockSpec((1, H_M, DH_M, DH_M), lambda b, c: (b, 0, 0, 0))
    return pl.pallas_call(
        functools.partial(_mlstm_kernel, L=L, nc=nc),
        grid=(nb, nc),
        in_specs=[tok(d, 0), tok(d, 1), tok(d, 2), tok(16),
                  pl.BlockSpec((1, 8, L), lambda b, c: (b * nc + c, 0, 0)),
                  tok(d), pl.BlockSpec((1, d), lambda b, c: (0, 0)),
                  per_b4, per_b3, per_b3],
        out_specs=[tok(d), per_b4, per_b3, per_b3],
        out_shape=[jax.ShapeDtypeStruct((n, d), BF16),
                   jax.ShapeDtypeStruct((nb, H_M, DH_M, DH_M), F32),
                   jax.ShapeDtypeStruct((nb, H_M, DH_M), F32),
                   jax.ShapeDtypeStruct((nb, H_M, LANES), F32)],
        scratch_shapes=[pltpu.VMEM((H_M, 2 * DH_M, DH_M), F32), pltpu.VMEM((8, LANES), F32)],
        compiler_params=_cparams(("parallel", "arbitrary")),
        name="mlstm",
    )(mqkv, mqkv, mqkv, small, grow3, mo, ng, c0, n0, m0b)


def _topk_search(keys_sc, cand_sc, t_sc, tu_sc, aux_sc, cnt_sc, *, nchunk, rows, tk, hb, topk):
    ncol = tk // LANES
    ngrp = rows // hb
    lane_hb = lax.broadcasted_iota(I32, (hb, LANES), 1)
    ones_mat = jnp.ones((LANES, LANES), BF16)

    def count_rows(hit_fn):
        cnt_sc[...] = jnp.zeros(cnt_sc.shape, F32)

        def body(c, _):
            for g in range(ngrp):
                acc = cnt_sc[g * hb:(g + 1) * hb, :]
                for j in range(ncol):
                    kj = keys_sc[c, g * hb:(g + 1) * hb, j * LANES:(j + 1) * LANES]
                    acc = acc + jnp.where(hit_fn(c, g, j, kj), 1.0, 0.0)
                cnt_sc[g * hb:(g + 1) * hb, :] = acc
            return 0

        lax.fori_loop(0, nchunk, body, 0)
        return _dot(cnt_sc[...].astype(BF16), ones_mat)

    def hit_ge(c, g, j, kj):
        return kj >= cand_sc[g * hb:(g + 1) * hb, :]

    tu_sc[...] = jnp.zeros(tu_sc.shape, I32)

    def search_pass(p, _):
        bit = jnp.left_shift(jnp.int32(1), 31 - p)
        cand_sc[...] = (tu_sc[...] | bit) ^ I32_MIN
        tu_sc[...] = jnp.where(count_rows(hit_ge) >= topk, tu_sc[...] | bit, tu_sc[...])
        return 0

    lax.fori_loop(0, 32, search_pass, 0)
    t_sc[...] = tu_sc[...] ^ I32_MIN

    cand_sc[...] = t_sc[...]
    tied = jnp.logical_and(count_rows(hit_ge) > topk, t_sc[...] > NEGINF_KEY)
    any_tied = jnp.max(jnp.where(tied, 1.0, 0.0)) > 0.0
    tu_sc[...] = jnp.full(tu_sc.shape, 2 ** 31 - 1, I32)

    @pl.when(any_tied)
    def _():
        cand_sc[...] = t_sc[...] + 1
        aux_sc[...] = topk - count_rows(hit_ge)

        def hit_tie_below(c, g, j, kj):
            pos = c * tk + j * LANES + lane_hb
            return jnp.where(kj == t_sc[g * hb:(g + 1) * hb, :], pos, 2 ** 30) < cand_sc[g * hb:(g + 1) * hb, :]

        tu_sc[...] = jnp.zeros(tu_sc.shape, I32)

        def cut_pass(p, _):
            bit = jnp.left_shift(jnp.int32(1), INDEX_BITS - 1 - p)
            cand_sc[...] = tu_sc[...] | bit
            tu_sc[...] = jnp.where(count_rows(hit_tie_below) < aux_sc[...], tu_sc[...] | bit, tu_sc[...])
            return 0

        lax.fori_loop(0, INDEX_BITS, cut_pass, 0)

    return any_tied


def _dsa_prompt_kernel(iq_ref, small_ref, aq_ref, qpos_ref, ik_ref, k_ref, v_ref, kpos_ref, o_ref,
                       qidx_sc, wrep_sc, qa_sc, s_sc, keys_sc, lg_sc, p_sc, alpha_sc, m_sc, l_sc, acc_sc,
                       cand_sc, t_sc, tu_sc, aux_sc, cnt_sc, *, tq, tk, topk, rb, hb):
    i = pl.program_id(1)
    q0 = i * tq
    nchunk = (i + 1) * (tq // tk)
    ncol = tk // LANES
    nrb = tq // rb
    lane_hb = lax.broadcasted_iota(I32, (hb, LANES), 1)
    lane_rb = lax.broadcasted_iota(I32, (rb, LANES), 1)
    lane_minus_row = lane_rb - lax.broadcasted_iota(I32, (rb, LANES), 0)

    for h in range(H_IDX):
        qidx_sc[h * tq:(h + 1) * tq, :] = iq_ref[:, h * D_IDX:(h + 1) * D_IDX]
        wrep_sc[h] = jnp.broadcast_to(small_ref[:, h:h + 1], (tq, LANES))
    for hh in range(H_A):
        qa_sc[hh * tq:(hh + 1) * tq, 0:DH_A] = aq_ref[:, hh * DH_A:(hh + 1) * DH_A]
        qa_sc[hh * tq:(hh + 1) * tq, DH_A:2 * DH_A] = qpos_ref[0, hh * tq:(hh + 1) * tq, :]

    def score_body(c, _):
        ikc = ik_ref[pl.ds(pl.multiple_of(c * tk, tk), tk), :]
        s_sc[...] = _dot_nt(qidx_sc[...], ikc)
        bound = q0 - c * tk
        for r in range(nrb):
            for j in range(ncol):
                acc = jnp.zeros((rb, LANES), F32)
                for h in range(H_IDX):
                    s = s_sc[h * tq + r * rb:h * tq + (r + 1) * rb, j * LANES:(j + 1) * LANES]
                    acc = acc + wrep_sc[h, r * rb:(r + 1) * rb, :] * jnp.maximum(s, 0.0)
                acc = jnp.where(lane_minus_row <= bound + (r * rb - j * LANES), acc, -jnp.inf)
                keys_sc[c, r * rb:(r + 1) * rb, j * LANES:(j + 1) * LANES] = _score_key(
                    acc, c * tk + j * LANES + lane_rb)
        return 0

    lax.fori_loop(0, nchunk, score_body, 0)

    any_tied = _topk_search(keys_sc, cand_sc, t_sc, tu_sc, aux_sc, cnt_sc,
                            nchunk=nchunk, rows=tq, tk=tk, hb=hb, topk=topk)

    @pl.when(any_tied)
    def _():
        def drop_body(c, _):
            for g in range(tq // hb):
                tb = t_sc[g * hb:(g + 1) * hb, :]
                xb = tu_sc[g * hb:(g + 1) * hb, :]
                for j in range(ncol):
                    kj = keys_sc[c, g * hb:(g + 1) * hb, j * LANES:(j + 1) * LANES]
                    pos = c * tk + j * LANES + lane_hb
                    late = jnp.where(kj == tb, pos, -1) > xb
                    keys_sc[c, g * hb:(g + 1) * hb, j * LANES:(j + 1) * LANES] = jnp.where(late, kj - 1, kj)
            return 0

        lax.fori_loop(0, nchunk, drop_body, 0)

    t_sc[...] = jnp.maximum(t_sc[...], NEGINF_KEY + 1)
    m_sc[...] = jnp.full(m_sc.shape, NEG, F32)
    l_sc[...] = jnp.zeros(l_sc.shape, F32)
    acc_sc[...] = jnp.zeros(acc_sc.shape, F32)

    def att_body(c, _):
        start = pl.multiple_of(c * tk, tk)
        kaug = jnp.concatenate([k_ref[pl.ds(start, tk), :], kpos_ref[pl.ds(start, tk), :]], axis=1)
        lg_sc[...] = _dot_nt(qa_sc[...], kaug)
        for r in range(nrb):
            tb = t_sc[r * rb:(r + 1) * rb, :]
            selb = [jnp.where(keys_sc[c, r * rb:(r + 1) * rb, j * LANES:(j + 1) * LANES] >= tb, 0.0, NEG)
                    for j in range(ncol)]
            for hh in range(H_A):
                rows = slice(hh * tq + r * rb, hh * tq + (r + 1) * rb)
                lgs = [lg_sc[rows, j * LANES:(j + 1) * LANES] + selb[j] for j in range(ncol)]
                mx = lgs[0]
                for j in range(1, ncol):
                    mx = jnp.maximum(mx, lgs[j])
                m_old = m_sc[rows, :]
                m_new = jnp.maximum(m_old, jnp.max(mx, axis=1, keepdims=True))
                alpha = jnp.exp(m_old - m_new)
                ps = [jnp.exp(lg - m_new) for lg in lgs]
                psum = ps[0]
                for j in range(1, ncol):
                    psum = psum + ps[j]
                l_sc[rows, :] = alpha * l_sc[rows, :] + psum
                for j in range(ncol):
                    p_sc[rows, j * LANES:(j + 1) * LANES] = ps[j].astype(BF16)
                alpha_sc[rows, :] = alpha
                m_sc[rows, :] = m_new
        acc_sc[...] = alpha_sc[...] * acc_sc[...] + _dot(p_sc[...], v_ref[pl.ds(start, tk), :])
        return 0

    lax.fori_loop(0, nchunk, att_body, 0)
    for hh in range(H_A):
        rows = slice(hh * tq, (hh + 1) * tq)
        l = jnp.sum(l_sc[rows, :], axis=1, keepdims=True)
        o_ref[:, hh * DH_A:(hh + 1) * DH_A] = (acc_sc[rows, :] / l).astype(BF16)


def _alibi_tables(seq, tq):
    pos = np.arange(seq)
    hi, lo = (pos // 64).astype(np.float32), (pos % 64).astype(np.float32)
    ktab = np.zeros((seq, DH_A), np.float32)
    ktab[:, 0], ktab[:, 1], ktab[:, 2], ktab[:, 3] = 64.0 * hi, lo, 1.0, 1.0
    nq = seq // tq
    qtab = np.zeros((nq, H_A, tq, DH_A), np.float32)
    for hh, slope in enumerate(ALIBI_SLOPES):
        qtab[:, hh, :, 0] = slope
        qtab[:, hh, :, 1] = slope
        qtab[:, hh, :, 2] = (-slope * 64.0 * hi).reshape(nq, tq)
        qtab[:, hh, :, 3] = (-slope * lo).reshape(nq, tq)
    return jnp.asarray(qtab.reshape(nq, H_A * tq, DH_A), BF16), jnp.asarray(ktab, BF16)


def _dsa_prompt(iq, small, aq, ikb, kb, vb, nb, seq, tq):
    n = iq.shape[0]
    nq = seq // tq
    topk = min(TOPK_MAX, seq // 4)
    tk = min(256, tq)
    rb = min(64, tq)
    hb = min(128, tq)
    assert seq // LANES <= 256
    qtab, ktab = _alibi_tables(seq, tq)

    def tok(w):
        return pl.BlockSpec((tq, w), lambda b, i: (b * nq + i, 0))

    def per_b(w):
        return pl.BlockSpec((seq, w), lambda b, i: (b, 0))

    return pl.pallas_call(
        functools.partial(_dsa_prompt_kernel, tq=tq, tk=tk, topk=topk, rb=rb, hb=hb),
        grid=(nb, nq),
        in_specs=[tok(512), tok(16), tok(512),
                  pl.BlockSpec((1, H_A * tq, DH_A), lambda b, i: (i, 0, 0)),
                  per_b(D_IDX), per_b(DH_A), per_b(DH_A),
                  pl.BlockSpec((seq, DH_A), lambda b, i: (0, 0))],
        out_specs=tok(512),
        out_shape=jax.ShapeDtypeStruct((n, 512), BF16),
        scratch_shapes=[pltpu.VMEM((H_IDX * tq, D_IDX), BF16),
                        pltpu.VMEM((H_IDX, tq, LANES), F32),
                        pltpu.VMEM((H_A * tq, 2 * DH_A), BF16),
                        pltpu.VMEM((H_IDX * tq, tk), F32),
                        pltpu.VMEM((seq // tk, tq, tk), I32),
                        pltpu.VMEM((H_A * tq, tk), F32),
                        pltpu.VMEM((H_A * tq, tk), BF16),
                        pltpu.VMEM((H_A * tq, LANES), F32),
                        pltpu.VMEM((H_A * tq, LANES), F32),
                        pltpu.VMEM((H_A * tq, LANES), F32),
                        pltpu.VMEM((H_A * tq, DH_A), F32),
                        pltpu.VMEM((tq, LANES), I32),
                        pltpu.VMEM((tq, LANES), I32),
                        pltpu.VMEM((tq, LANES), I32),
                        pltpu.VMEM((tq, LANES), F32),
                        pltpu.VMEM((tq, LANES), F32)],
        compiler_params=_cparams(("parallel", "arbitrary")),
        name="dsa_prompt",
    )(iq, small, aq, qtab, ikb, kb, vb, ktab)


def _page_pipeline(pt_ref, copies_of, n_pages):
    b = pl.program_id(0)
    slot = b % 2

    def for_all(bb, sl, act):
        def body(j, _):
            for cp in copies_of(pt_ref[bb, j], sl, j):
                act(cp)
            return 0
        lax.fori_loop(0, n_pages, body, 0)

    @pl.when(b == 0)
    def _():
        for_all(0, 0, lambda cp: cp.start())

    @pl.when(b + 1 < pl.num_programs(0))
    def _():
        for_all(b + 1, 1 - slot, lambda cp: cp.start())

    for_all(b, slot, lambda cp: cp.wait())
    return slot


def _sample_scores_kernel(pt_ref, iq_ref, w_ref, iknew_ref, ckidx_hbm, keys_ref, ibuf, sems,
                          *, n_pages, n_new, cw):
    past = n_pages * PAGE
    total = past + LANES

    def copies_of(pg, sl, j):
        dst = pl.ds(pl.multiple_of(j * PAGE, PAGE), PAGE)
        return (pltpu.make_async_copy(ckidx_hbm.at[pg], ibuf.at[sl, :, dst], sems.at[sl]),)

    slot = _page_pipeline(pt_ref, copies_of, n_pages)

    iq = iq_ref[0]
    w = w_ref[0]

    def scores(dots):
        s = jnp.maximum(dots, 0.0) * w
        return jnp.sum(s.reshape(n_new, H_IDX, s.shape[-1]), axis=1)

    for ch in range(past // cw):
        sc = scores(_dot(iq, ibuf[slot, :, ch * cw:(ch + 1) * cw].astype(BF16)))
        keys_ref[0, :, ch * cw:(ch + 1) * cw] = _score_key(sc, ch * cw + lax.broadcasted_iota(I32, sc.shape, 1))
    sc = scores(_dot_nt(iq, iknew_ref[0]))
    t_i = lax.broadcasted_iota(I32, (n_new, LANES), 0)
    j_i = lax.broadcasted_iota(I32, (n_new, LANES), 1)
    keys_ref[0, :, past:total] = _score_key(jnp.where(j_i <= t_i, sc, -jnp.inf), past + j_i)


def _sample_search_kernel(keys_ref, t_ref, x_ref, cand_sc, aux_sc, cnt_sc, *, nchunk, rows, topk):
    _topk_search(keys_ref, cand_sc, t_ref, x_ref, aux_sc, cnt_sc,
                 nchunk=nchunk, rows=rows, tk=LANES, hb=rows, topk=topk)


def _sample_attend_kernel(pt_ref, q_ref, knew_ref, vnew_ref, keys_ref, t_ref, x_ref, ck_hbm, cv_hbm, o_ref,
                          kbuf, vbuf, sems, lg_sc, *, n_pages, n_new, cw):
    past = n_pages * PAGE
    total = past + LANES
    rows_q = n_new * H_A

    def copies_of(pg, sl, j):
        dst = pl.ds(pl.multiple_of(j * PAGE, PAGE), PAGE)
        return (pltpu.make_async_copy(ck_hbm.at[pg], kbuf.at[sl, dst, :], sems.at[0, sl]),
                pltpu.make_async_copy(cv_hbm.at[pg], vbuf.at[sl, dst, :], sems.at[1, sl]))

    slot = _page_pipeline(pt_ref, copies_of, n_pages)

    r_i = lax.broadcasted_iota(I32, (rows_q, 1), 0)
    r_t = r_i // H_A
    r_h = r_i % H_A
    slope = jnp.zeros((rows_q, 1), F32)
    for hh in range(H_A):
        slope = jnp.where(r_h == hh, ALIBI_SLOPES[hh], slope)
    qposf = (past + r_t).astype(F32)
    q = q_ref[0]

    def sel_bias(lo, width):
        pos = lo + lax.broadcasted_iota(I32, (1, width), 1)
        out = jnp.full((rows_q, width), NEG, F32)
        for t in range(n_new):
            kt = keys_ref[0, t:t + 1, lo:lo + width]
            thr = t_ref[0, t:t + 1, 0:1]
            keep_tie = jnp.logical_and(kt == thr, pos <= x_ref[0, t:t + 1, 0:1])
            chosen = jnp.logical_and(jnp.logical_or(kt > thr, keep_tie), kt > NEGINF_KEY)
            bias_t = jnp.where(chosen, 0.0, NEG)
            out = jnp.where(r_t == t, bias_t, out)
        return out

    def logits(k_chunk, lo, width):
        kposf = (lo + lax.broadcasted_iota(I32, (1, width), 1)).astype(F32)
        sel = sel_bias(lo, width)
        lg = _dot_nt(q, k_chunk) - slope * (qposf - kposf)
        return lg + sel

    for ch in range(past // cw):
        lg_sc[:, ch * cw:(ch + 1) * cw] = logits(kbuf[slot, ch * cw:(ch + 1) * cw, :].astype(BF16), ch * cw, cw)
    lg_sc[:, past:total] = logits(knew_ref[0], past, LANES)

    lg = lg_sc[...]
    m = jnp.max(lg, axis=1, keepdims=True)
    p = jnp.exp(lg - m)
    l = jnp.sum(p, axis=1, keepdims=True)
    pb = p.astype(BF16)
    acc = _dot(pb[:, past:total], vnew_ref[0])
    for ch in range(past // cw):
        acc = acc + _dot(pb[:, ch * cw:(ch + 1) * cw], vbuf[slot, ch * cw:(ch + 1) * cw, :].astype(BF16))
    o_ref[0] = (acc / l).astype(BF16)


def _dsa_sample(page_table, iq32, w32, q16, iknew, knew, vnew, cache_kidx, cache_k, cache_v, n_new):
    db, n_pages = page_table.shape
    past = n_pages * PAGE
    total = past + LANES
    topk = min(TOPK_MAX, (past + n_new) // 4)
    cw = 1024 if past % 1024 == 0 else PAGE
    rows_q = n_new * H_A

    n_rows = db * n_new
    nchunk = total // LANES
    srows = _pick_tile(n_rows, 128)

    def per_b(a):
        return pl.BlockSpec((1,) + a.shape[1:], lambda b, pt: (b, 0, 0))

    hbm = pl.BlockSpec(memory_space=pl.ANY)

    keys = pl.pallas_call(
        functools.partial(_sample_scores_kernel, n_pages=n_pages, n_new=n_new, cw=cw),
        grid_spec=pltpu.PrefetchScalarGridSpec(
            num_scalar_prefetch=1,
            grid=(db,),
            in_specs=[per_b(iq32), per_b(w32), per_b(iknew), hbm],
            out_specs=pl.BlockSpec((1, n_new, total), lambda b, pt: (b, 0, 0)),
            scratch_shapes=[pltpu.VMEM((2, D_IDX, past), F32), pltpu.SemaphoreType.DMA((2,))],
        ),
        out_shape=jax.ShapeDtypeStruct((db, n_new, total), I32),
        compiler_params=_cparams(("arbitrary",)),
        name="sample_scores",
    )(page_table, iq32, w32, iknew, cache_kidx)

    keys_cm = keys.reshape(n_rows, nchunk, LANES).transpose(1, 0, 2)
    thr, cut = pl.pallas_call(
        functools.partial(_sample_search_kernel, nchunk=nchunk, rows=srows, topk=topk),
        grid=(n_rows // srows,),
        in_specs=[pl.BlockSpec((nchunk, srows, LANES), lambda i: (0, i, 0))],
        out_specs=[pl.BlockSpec((srows, LANES), lambda i: (i, 0))] * 2,
        out_shape=[jax.ShapeDtypeStruct((n_rows, LANES), I32)] * 2,
        scratch_shapes=[pltpu.VMEM((srows, LANES), I32), pltpu.VMEM((srows, LANES), F32),
                        pltpu.VMEM((srows, LANES), F32)],
        compiler_params=_cparams(("parallel",)),
        name="sample_search",
    )(keys_cm)

    thr3 = thr.reshape(db, n_new, LANES)
    cut3 = cut.reshape(db, n_new, LANES)
    return pl.pallas_call(
        functools.partial(_sample_attend_kernel, n_pages=n_pages, n_new=n_new, cw=cw),
        grid_spec=pltpu.PrefetchScalarGridSpec(
            num_scalar_prefetch=1,
            grid=(db,),
            in_specs=[per_b(q16), per_b(knew), per_b(vnew), per_b(keys), per_b(thr3), per_b(cut3), hbm, hbm],
            out_specs=pl.BlockSpec((1, rows_q, DH_A), lambda b, pt: (b, 0, 0)),
            scratch_shapes=[pltpu.VMEM((2, past, DH_A), F32), pltpu.VMEM((2, past, DH_A), F32),
                            pltpu.SemaphoreType.DMA((2, 2)), pltpu.VMEM((rows_q, total), F32)],
        ),
        out_shape=jax.ShapeDtypeStruct((db, rows_q, DH_A), BF16),
        compiler_params=_cparams(("arbitrary",)),
        name="sample_attend",
    )(page_table, q16, knew, vnew, keys, thr3, cut3, cache_k, cache_v)


def _outffn_kernel(x_ref, hm_ref, ha_ref, mod_ref, g2_ref, wom_ref, woa_ref, wg_ref, wu_ref, wd_ref, y_ref):
    mix = _dot(hm_ref[...], wom_ref[...]) + _dot(ha_ref[...], woa_ref[...])
    x1 = x_ref[...] + mod_ref[0, 0] * mix
    xn = x1 * lax.rsqrt(jnp.mean(x1 * x1, axis=-1, keepdims=True) + EPS) * g2_ref[...]
    hb = (xn * (1.0 + mod_ref[2, 0]) + mod_ref[1, 0]).astype(BF16)
    g = _dot(hb, wg_ref[...])
    u = _dot(hb, wu_ref[...])
    act = (g * _sigmoid(g) * u).astype(BF16)
    y_ref[...] = x1 + mod_ref[3, 0] * _dot(act, wd_ref[...])


def _outffn(x, hm, ha, mod_out, per_token_mod, rows_per_mod, g2, wom, woa, wg, wu, wd, tm):
    n, d = x.shape
    if per_token_mod:
        mod_spec = pl.BlockSpec((4, 1, tm, d), lambda i: (0, 0, i, 0))
    else:
        tiles = rows_per_mod // tm
        mod_spec = pl.BlockSpec((4, 1, 1, d), lambda i: (0, i // tiles, 0, 0))

    def full(a):
        return pl.BlockSpec(a.shape, lambda i: (0,) * a.ndim, pipeline_mode=pl.Buffered(1))

    def rows(w):
        return pl.BlockSpec((tm, w), lambda i: (i, 0))

    return pl.pallas_call(
        _outffn_kernel,
        grid=(n // tm,),
        in_specs=[rows(d), rows(512), rows(512), mod_spec, full(g2), full(wom), full(woa),
                  full(wg), full(wu), full(wd)],
        out_specs=rows(d),
        out_shape=jax.ShapeDtypeStruct((n, d), F32),
        compiler_params=_cparams(("parallel",)),
        name="outffn",
    )(x, hm, ha, mod_out, g2, wom, woa, wg, wu, wd)


def _pick_tile(n, pref):
    t = pref
    while n % t:
        t //= 2
    return t


def kernel(x_prompt, x_sample, cache_k, cache_v, cache_kidx, state_C, state_n, state_m, page_table,
           c_prompt, c_sample, w_ada, b_ada, g_norm1, w_in, b_igate, b_fgate, mlstm_norm_g,
           q_norm_g, k_norm_g, w_out, g_norm2, w_gate, w_up, w_down):
    bp, seq, d = x_prompt.shape
    db, t_new, _ = x_sample.shape
    n_p, n_s = bp * seq, db * t_new

    o = np.cumsum([0, 512, 512, 512, 512, H_M, H_M, 512, DH_A, DH_A, H_IDX * D_IDX, D_IDX, H_IDX])
    mq, mk, mv, mo, mi, mf, aq, ak, av, iq, ik, iw = [w_in[:, int(o[j]):int(o[j + 1])] for j in range(12)]
    zpad = lambda wdt: jnp.zeros((d, wdt), w_in.dtype)
    w_r = jnp.concatenate([mq, mk, mv, mo, aq, ak, av, iq, ik, zpad(LANES - D_IDX),
                           iw, mi, mf, zpad(LANES - 16)], axis=1).astype(BF16)
    wgt = jnp.concatenate([mi, mf], axis=1).T.astype(BF16)
    bsm = jnp.concatenate([jnp.zeros((H_IDX,), F32), b_igate, b_fgate]).reshape(1, 16)
    brow = jnp.concatenate([b_igate, b_fgate]).reshape(8, 1)
    g1 = g_norm1.reshape(1, d)
    g2 = g_norm2.reshape(1, d)
    qg = q_norm_g.reshape(1, DH_A)
    kg = k_norm_g.reshape(1, DH_A)
    ng = mlstm_norm_g.reshape(1, H_M * DH_M)
    wom = w_out[0:H_M * DH_M].astype(BF16)
    woa = w_out[H_M * DH_M:].astype(BF16)
    wg = w_gate.astype(BF16)
    wu = w_up.astype(BF16)
    wd = w_down.astype(BF16)

    mod = _ada(jnp.concatenate([c_prompt, c_sample], axis=0), w_ada.astype(BF16), b_ada)
    mod_p = mod[:bp].reshape(bp, 6, 1, d).transpose(1, 0, 2, 3)
    mod_s = jnp.repeat(mod[bp:].reshape(db, 6, d), t_new, axis=0)
    mod_s = mod_s.transpose(1, 0, 2).reshape(6, 1, n_s, d)

    tm_p = _pick_tile(seq, 512)
    (mqkv, mo_p, aq_p, k_p, v_p, kidx_p, kb, vb, ikb, iq_p, small_p, grow_p) = _inproj(
        x_prompt.reshape(n_p, d), mod_p[0:2], False, seq, g1, w_r, wgt, bsm, brow, qg, kg, tm_p)

    lc = _pick_tile(seq, 256)
    grow3 = grow_p.reshape(8, n_p // lc, lc).transpose(1, 0, 2)
    hm_p, C_p, n_pst, m_pb = _mlstm(
        mqkv, small_p, grow3, mo_p, ng,
        jnp.zeros((bp, H_M, DH_M, DH_M), F32), jnp.zeros((bp, H_M, DH_M), F32),
        jnp.zeros((bp, H_M, LANES), F32), bp, lc)

    tq = _pick_tile(seq, 512)
    ha_p = _dsa_prompt(iq_p, small_p, aq_p, ikb, kb, vb, bp, seq, tq)

    y_p = _outffn(x_prompt.reshape(n_p, d), hm_p, ha_p, mod_p[2:6], False, seq, g2,
                  wom, woa, wg, wu, wd, tm_p)

    tm_s = _pick_tile(n_s, 128)
    (mqkv_s, mo_s, aq_s, k_s, v_s, kidx_s, kb_s, vb_s, ikb_s, iq_s, small_s, grow_s) = _inproj(
        x_sample.reshape(n_s, d), mod_s[0:2], True, 0, g1, w_r, wgt, bsm, brow, qg, kg, tm_s)

    lp = 16
    pad_tok = lambda a: jnp.pad(a.reshape(db, t_new, a.shape[-1]),
                                ((0, 0), (0, lp - t_new), (0, 0))).reshape(db * lp, a.shape[-1])
    gate_pad = jnp.concatenate([jnp.zeros((H_IDX,), F32), jnp.full((H_M,), -jnp.inf, F32),
                                jnp.zeros((H_M,), F32)])
    small_pad = jnp.concatenate(
        [small_s.reshape(db, t_new, 16), jnp.broadcast_to(gate_pad, (db, lp - t_new, 16))], axis=1
    ).reshape(db * lp, 16)
    grow_pad = jnp.concatenate(
        [grow_s.reshape(8, db, t_new),
         jnp.broadcast_to(gate_pad[H_IDX:].reshape(8, 1, 1), (8, db, lp - t_new))], axis=2
    ).transpose(1, 0, 2)
    hm_s_pad, C_s, n_sst, m_sb = _mlstm(
        pad_tok(mqkv_s), small_pad, grow_pad, pad_tok(mo_s), ng,
        state_C, state_n, jnp.broadcast_to(state_m[:, :, None], (db, H_M, LANES)), db, lp)
    hm_s = hm_s_pad.reshape(db, lp, H_M * DH_M)[:, :t_new].reshape(n_s, H_M * DH_M)

    pad_rows = lambda a: jnp.pad(a.reshape(db, t_new, a.shape[-1]), ((0, 0), (0, LANES - t_new), (0, 0)))
    ha_s = _dsa_sample(
        page_table, iq_s.reshape(db, t_new * H_IDX, D_IDX),
        small_s[:, 0:H_IDX].reshape(db, t_new * H_IDX, 1),
        aq_s.reshape(db, t_new * H_A, DH_A),
        pad_rows(ikb_s), pad_rows(kb_s), pad_rows(vb_s),
        jnp.swapaxes(cache_kidx, 1, 2), cache_k, cache_v, t_new).reshape(n_s, H_A * DH_A)

    y_s = _outffn(x_sample.reshape(n_s, d), hm_s, ha_s, mod_s[2:6], True, 0, g2,
                  wom, woa, wg, wu, wd, tm_s)

    return (y_p.reshape(bp, seq, d), y_s.reshape(db, t_new, d),
            k_p.reshape(bp, seq, DH_A), v_p.reshape(bp, seq, DH_A), kidx_p.reshape(bp, seq, D_IDX),
            C_p, n_pst, m_pb[:, :, 0],
            k_s.reshape(db, t_new, DH_A), v_s.reshape(db, t_new, DH_A), kidx_s.reshape(db, t_new, D_IDX),
            C_s, n_sst, m_sb[:, :, 0])
```

```python
import functools

import jax
import jax.numpy as jnp
import numpy as np
from jax import lax
from jax.experimental import pallas as pl
from jax.experimental.pallas import tpu as pltpu

F32 = jnp.float32
BF16 = jnp.bfloat16
I32 = jnp.int32

H_M = 4
DH_M = 128
H_A = 4
DH_A = 128
H_IDX = 8
D_IDX = 64
TOPK_MAX = 256
PAGE = 128
EPS = 1e-6
INDEX_SCALE = D_IDX ** -0.5 * H_IDX ** -0.5
ALIBI_SLOPES = tuple(float(2.0 ** (-8.0 * (h + 1) / H_A)) for h in range(H_A))

LANES = 128
VMEM_LIMIT = 56 * 1024 * 1024
NEG = -1e30
I32_MIN = -2 ** 31
F32_TINY = float(np.finfo(np.float32).tiny)
NEGINF_KEY = -2139095041
INDEX_BITS = 14

C_MQ, C_MK, C_MV, C_MO, C_AQ, C_AK, C_AV, C_IQ, C_IK, C_SM, C_END = (
    0, 512, 1024, 1536, 2048, 2560, 2688, 2816, 3328, 3456, 3584)


def _cparams(sem):
    return pltpu.CompilerParams(dimension_semantics=sem, vmem_limit_bytes=VMEM_LIMIT)


def _sigmoid(x):
    return 1.0 / (1.0 + jnp.exp(-x))


def _log_sigmoid(x):
    return jnp.minimum(x, 0.0) - jnp.log1p(jnp.exp(-jnp.abs(x)))


def _dot(a, b):
    return jnp.dot(a, b, preferred_element_type=F32)


def _dot_nt(a, b):
    return lax.dot_general(a, b, (((1,), (1,)), ((), ())), preferred_element_type=F32)


def _dot_tn(a, b):
    return lax.dot_general(a, b, (((0,), (0,)), ((), ())), preferred_element_type=F32)


def _split3(x):
    hi = x.astype(BF16)
    r1 = x - hi.astype(F32)
    mid = r1.astype(BF16)
    lo = (r1 - mid.astype(F32)).astype(BF16)
    return hi, mid, lo


def _sort_key(x):
    b = pltpu.bitcast(x, I32)
    return b ^ ((b >> 31) & 0x7FFFFFFF)


def _score_key(score, pos):
    return jnp.where(jnp.abs(score) < F32_TINY, -pos, _sort_key(score))


def _ada_kernel(c_ref, w_ref, b_ref, o_ref):
    c = c_ref[...]
    s = c * _sigmoid(c)
    o_ref[...] = _dot(s.astype(BF16), w_ref[...]) + b_ref[...]


def _ada(c, w_bf, b):
    r, d = c.shape
    n = w_bf.shape[1]
    return pl.pallas_call(
        _ada_kernel,
        grid=(n // d,),
        in_specs=[pl.BlockSpec((r, d), lambda j: (0, 0)),
                  pl.BlockSpec((d, d), lambda j: (0, j)),
                  pl.BlockSpec((1, d), lambda j: (0, j))],
        out_specs=pl.BlockSpec((r, d), lambda j: (0, j)),
        out_shape=jax.ShapeDtypeStruct((r, n), F32),
        compiler_params=_cparams(("arbitrary",)),
        name="ada",
    )(c, w_bf, b.reshape(1, n))


def _inproj_kernel(x_ref, mod_ref, g1_ref, w_ref, wgt_ref, bsm_ref, brow_ref, qg_ref, kg_ref,
                   mqkv_ref, mo_ref, aq_ref, k_ref, v_ref, kidx_ref, kb_ref, vb_ref, ikb_ref,
                   iq_ref, small_ref, grow_ref):
    x = x_ref[...]
    xn = x * lax.rsqrt(jnp.mean(x * x, axis=-1, keepdims=True) + EPS) * g1_ref[...]
    h = xn * (1.0 + mod_ref[1, 0]) + mod_ref[0, 0]
    hb = h.astype(BF16)

    def sec(a, b):
        return _dot(hb, w_ref[:, a:b])

    mqkv_ref[:, 0:512] = sec(C_MQ, C_MK).astype(BF16)
    mqkv_ref[:, 512:1024] = (sec(C_MK, C_MV) * (DH_M ** -0.5)).astype(BF16)
    mqkv_ref[:, 1024:1536] = sec(C_MV, C_MO).astype(BF16)
    mo_ref[...] = sec(C_MO, C_AQ)

    aq = sec(C_AQ, C_AK)
    qg = qg_ref[...]
    for hh in range(H_A):
        a = aq[:, hh * DH_A:(hh + 1) * DH_A]
        a = a * lax.rsqrt(jnp.mean(a * a, axis=-1, keepdims=True) + EPS) * qg
        aq_ref[:, hh * DH_A:(hh + 1) * DH_A] = (a * (DH_A ** -0.5)).astype(BF16)

    ak = sec(C_AK, C_AV)
    ak = ak * lax.rsqrt(jnp.mean(ak * ak, axis=-1, keepdims=True) + EPS) * kg_ref[...]
    k_ref[...] = ak
    kb_ref[...] = ak.astype(BF16)
    av = sec(C_AV, C_IQ)
    v_ref[...] = av
    vb_ref[...] = av.astype(BF16)

    iq_ref[...] = sec(C_IQ, C_IK).astype(BF16)
    ik = sec(C_IK, C_SM)[:, 0:D_IDX]
    kidx_ref[...] = ik
    ikb_ref[...] = ik.astype(BF16)

    sm = sec(C_SM, C_END)[:, 0:16] + bsm_ref[...]
    col = lax.broadcasted_iota(I32, sm.shape, 1)
    small_ref[...] = jnp.where(col < H_IDX, sm * INDEX_SCALE,
                               jnp.where(col < H_IDX + H_M, sm, _log_sigmoid(sm)))

    gr = _dot_nt(wgt_ref[...], hb) + brow_ref[...]
    row = lax.broadcasted_iota(I32, gr.shape, 0)
    grow_ref[...] = jnp.where(row < H_M, gr, _log_sigmoid(gr))


def _inproj(x, mod_in, per_token_mod, rows_per_mod, g1, w_r, wgt, bsm, brow, qg, kg, tm):
    n, d = x.shape
    if per_token_mod:
        mod_spec = pl.BlockSpec((2, 1, tm, d), lambda i: (0, 0, i, 0))
    else:
        tiles = rows_per_mod // tm
        mod_spec = pl.BlockSpec((2, 1, 1, d), lambda i: (0, i // tiles, 0, 0))

    def full(a):
        return pl.BlockSpec(a.shape, lambda i: (0,) * a.ndim)

    def rows(w):
        return pl.BlockSpec((tm, w), lambda i: (i, 0))

    outs = [(1536, BF16), (512, F32), (512, BF16), (128, F32), (128, F32), (D_IDX, F32),
            (128, BF16), (128, BF16), (D_IDX, BF16), (512, BF16), (16, F32)]
    out_shape = [jax.ShapeDtypeStruct((n, w), dt) for w, dt in outs]
    out_specs = [rows(w) for w, _ in outs]
    out_shape.append(jax.ShapeDtypeStruct((8, n), F32))
    out_specs.append(pl.BlockSpec((8, tm), lambda i: (0, i)))
    return pl.pallas_call(
        _inproj_kernel,
        grid=(n // tm,),
        in_specs=[rows(d), mod_spec, full(g1), full(w_r), full(wgt), full(bsm), full(brow),
                  full(qg), full(kg)],
        out_specs=out_specs,
        out_shape=out_shape,
        compiler_params=_cparams(("parallel",)),
        name="inproj",
    )(x, mod_in, g1, w_r, wgt, bsm, brow, qg, kg)


def _mlstm_kernel(q_ref, k_ref, v_ref, small_ref, grow_ref, o_ref, ng_ref, c0_ref, n0_ref, m0_ref,
                  h_ref, c_out_ref, n_out_ref, m_out_ref, state_sc, m_sc, *, L, nc):
    c = pl.program_id(1)

    @pl.when(c == 0)
    def _():
        for hh in range(H_M):
            state_sc[hh, 0:DH_M, :] = c0_ref[0, hh]
            state_sc[hh, DH_M:2 * DH_M, :] = jnp.broadcast_to(n0_ref[0, hh:hh + 1, :], (DH_M, DH_M))
        m_sc[...] = jnp.zeros(m_sc.shape, F32)
        m_sc[0:H_M, :] = m0_ref[0]

    ti = lax.broadcasted_iota(I32, (L, L), 0)
    si = lax.broadcasted_iota(I32, (L, L), 1)
    causal = si <= ti
    tri = jnp.where(causal, 1.0, 0.0).astype(BF16)
    tri_t = jnp.where(ti <= si, 1.0, 0.0).astype(BF16)

    sm = small_ref[...]
    col = lax.broadcasted_iota(I32, sm.shape, 1)
    lf_cols = jnp.where(col >= H_IDX + H_M, sm, 0.0)
    hi, mid, lo = _split3(lf_cols)
    b_cols = _dot(tri, hi) + _dot(tri, mid) + _dot(tri, lo)
    gr = grow_ref[0]
    row = lax.broadcasted_iota(I32, gr.shape, 0)
    lf_rows = jnp.where(row >= H_M, gr, 0.0)
    hi, mid, lo = _split3(lf_rows)
    b_rows = _dot(hi, tri_t) + _dot(mid, tri_t) + _dot(lo, tri_t)

    ones_blk = jnp.ones((L, DH_M), BF16)

    def across(x):
        return x[:, 0:L] if L <= LANES else jnp.concatenate([x] * (L // LANES), axis=1)

    for hh in range(H_M):
        q = q_ref[:, hh * DH_M:(hh + 1) * DH_M]
        k = k_ref[:, hh * DH_M:(hh + 1) * DH_M]
        v = v_ref[:, hh * DH_M:(hh + 1) * DH_M]
        ig_c = jnp.broadcast_to(sm[:, H_IDX + hh:H_IDX + hh + 1], (L, LANES))
        b_c = jnp.broadcast_to(b_cols[:, H_IDX + H_M + hh:H_IDX + H_M + hh + 1], (L, LANES))
        ig_r = gr[hh:hh + 1, :]
        b_r = b_rows[H_M + hh:H_M + hh + 1, :]
        m_prev = m_sc[hh:hh + 1, :]

        dmat = jnp.where(causal, across(b_c) - b_r + ig_r, -jnp.inf)
        inter = b_c + m_prev
        m_t = jnp.maximum(inter, jnp.max(dmat, axis=-1, keepdims=True))
        w = jnp.exp(dmat - across(m_t))
        g = jnp.exp(inter - m_t)
        a = w * _dot_nt(q, k)
        st = state_sc[hh]
        qs = _dot_nt(q, st.astype(BF16))
        num = g * qs[:, 0:DH_M] + _dot(a.astype(BF16), v)
        den = g * qs[:, DH_M:2 * DH_M] + jnp.sum(a, axis=-1, keepdims=True)
        hv = num / jnp.maximum(jnp.abs(den), jnp.exp(-m_t))

        m_new = m_t[L - 1:L, :]
        b_last = b_c[L - 1:L, :]
        wend = jnp.exp(b_last - b_c + ig_c - m_new)
        gend = jnp.exp(b_last + m_prev - m_new)
        kw = (k.astype(F32) * wend).astype(BF16)
        v_aug = jnp.concatenate([v, ones_blk], axis=1)
        state_sc[hh] = gend * st + _dot_tn(v_aug, kw)
        m_sc[hh:hh + 1, :] = m_new

        hn = hv * lax.rsqrt(jnp.mean(hv * hv, axis=-1, keepdims=True) + EPS)
        hn = hn * ng_ref[:, hh * DH_M:(hh + 1) * DH_M]
        hn = hn * _sigmoid(o_ref[:, hh * DH_M:(hh + 1) * DH_M])
        h_ref[:, hh * DH_M:(hh + 1) * DH_M] = hn.astype(BF16)

    @pl.when(c == nc - 1)
    def _():
        for hh in range(H_M):
            c_out_ref[0, hh] = state_sc[hh, 0:DH_M, :]
            n_out_ref[0, hh:hh + 1, :] = state_sc[hh, DH_M:DH_M + 1, :]
        m_out_ref[0] = m_sc[0:H_M, :]


def _mlstm(mqkv, small, grow3, mo, ng, c0, n0, m0b, nb, L):
    n = mqkv.shape[0]
    nc = n // nb // L
    d = H_M * DH_M

    def tok(w, blk=0):
        return pl.BlockSpec((L, w), lambda b, c, blk=blk: (b * nc + c, blk))

    per_b3 = pl.BlockSpec((1, H_M, LANES), lambda b, c: (b, 0, 0))
    per_b4 = pl.BlockSpec((1, H_M, DH_M, DH_M), lambda b, c: (b, 0, 0, 0))
    return pl.pallas_call(
        functools.partial(_mlstm_kernel, L=L, nc=nc),
        grid=(nb, nc),
        in_specs=[tok(d, 0), tok(d, 1), tok(d, 2), tok(16),
                  pl.BlockSpec((1, 8, L), lambda b, c: (b * nc + c, 0, 0)),
                  tok(d), pl.BlockSpec((1, d), lambda b, c: (0, 0)),
                  per_b4, per_b3, per_b3],
        out_specs=[tok(d), per_b4, per_b3, per_b3],
        out_shape=[jax.ShapeDtypeStruct((n, d), BF16),
                   jax.ShapeDtypeStruct((nb, H_M, DH_M, DH_M), F32),
                   jax.ShapeDtypeStruct((nb, H_M, DH_M), F32),
                   jax.ShapeDtypeStruct((nb, H_M, LANES), F32)],
        scratch_shapes=[pltpu.VMEM((H_M, 2 * DH_M, DH_M), F32), pltpu.VMEM((8, LANES), F32)],
        compiler_params=_cparams(("parallel", "arbitrary")),
        name="mlstm",
    )(mqkv, mqkv, mqkv, small, grow3, mo, ng, c0, n0, m0b)


def _topk_search(keys_sc, cand_sc, t_sc, tu_sc, aux_sc, cnt_sc, *, nchunk, rows, tk, hb, topk,
                 last_chunk_first_group=0):
    ncol = tk // LANES
    ngrp = rows // hb
    lane_hb = lax.broadcasted_iota(I32, (hb, LANES), 1)
    ones_mat = jnp.ones((LANES, LANES), BF16)

    def count_rows(hit_fn):
        cnt_sc[...] = jnp.zeros(cnt_sc.shape, F32)

        def body(c, _, first_group=0):
            for g in range(first_group, ngrp):
                acc = cnt_sc[g * hb:(g + 1) * hb, :]
                for j in range(ncol):
                    kj = keys_sc[c, g * hb:(g + 1) * hb, j * LANES:(j + 1) * LANES]
                    acc = acc + jnp.where(hit_fn(c, g, j, kj), 1.0, 0.0)
                cnt_sc[g * hb:(g + 1) * hb, :] = acc
            return 0

        lax.fori_loop(0, nchunk - 1, body, 0)
        body(nchunk - 1, 0, last_chunk_first_group)
        return _dot(cnt_sc[...].astype(BF16), ones_mat)

    def hit_ge(c, g, j, kj):
        return kj >= cand_sc[g * hb:(g + 1) * hb, :]

    tu_sc[...] = jnp.zeros(tu_sc.shape, I32)

    def search_pass(p, _):
        bit = jnp.left_shift(jnp.int32(1), 31 - p)
        cand_sc[...] = (tu_sc[...] | bit) ^ I32_MIN
        tu_sc[...] = jnp.where(count_rows(hit_ge) >= topk, tu_sc[...] | bit, tu_sc[...])
        return 0

    lax.fori_loop(0, 32, search_pass, 0)
    t_sc[...] = tu_sc[...] ^ I32_MIN

    cand_sc[...] = t_sc[...]
    tied = jnp.logical_and(count_rows(hit_ge) > topk, t_sc[...] > NEGINF_KEY)
    any_tied = jnp.max(jnp.where(tied, 1.0, 0.0)) > 0.0
    tu_sc[...] = jnp.full(tu_sc.shape, 2 ** 31 - 1, I32)

    @pl.when(any_tied)
    def _():
        cand_sc[...] = t_sc[...] + 1
        aux_sc[...] = topk - count_rows(hit_ge)

        def hit_tie_below(c, g, j, kj):
            pos = c * tk + j * LANES + lane_hb
            return jnp.where(kj == t_sc[g * hb:(g + 1) * hb, :], pos, 2 ** 30) < cand_sc[g * hb:(g + 1) * hb, :]

        tu_sc[...] = jnp.zeros(tu_sc.shape, I32)

        def cut_pass(p, _):
            bit = jnp.left_shift(jnp.int32(1), INDEX_BITS - 1 - p)
            cand_sc[...] = tu_sc[...] | bit
            tu_sc[...] = jnp.where(count_rows(hit_tie_below) < aux_sc[...], tu_sc[...] | bit, tu_sc[...])
            return 0

        lax.fori_loop(0, INDEX_BITS, cut_pass, 0)

    return any_tied


def _dsa_prompt_kernel(iq_ref, small_ref, aq_ref, qpos_ref, ik_ref, k_ref, v_ref, kpos_ref, o_ref,
                       qidx_sc, wrep_sc, qa_sc, s_sc, keys_sc, lg_sc, p_sc, alpha_sc, m_sc, l_sc, acc_sc,
                       cand_sc, t_sc, tu_sc, aux_sc, cnt_sc, *, tq, tk, topk, rb, hb):
    i = pl.program_id(1)
    q0 = i * tq
    nchunk = (i + 1) * (tq // tk)
    ncol = tk // LANES
    nrb = tq // rb
    lane_hb = lax.broadcasted_iota(I32, (hb, LANES), 1)
    lane_rb = lax.broadcasted_iota(I32, (rb, LANES), 1)
    lane_minus_row = lane_rb - lax.broadcasted_iota(I32, (rb, LANES), 0)

    for h in range(H_IDX):
        qidx_sc[h * tq:(h + 1) * tq, :] = iq_ref[:, h * D_IDX:(h + 1) * D_IDX]
        wrep_sc[h] = jnp.broadcast_to(small_ref[:, h:h + 1], (tq, LANES))
    for hh in range(H_A):
        qa_sc[hh * tq:(hh + 1) * tq, 0:DH_A] = aq_ref[:, hh * DH_A:(hh + 1) * DH_A]
        qa_sc[hh * tq:(hh + 1) * tq, DH_A:2 * DH_A] = qpos_ref[0, hh * tq:(hh + 1) * tq, :]

    def score_body(c, _):
        ikc = ik_ref[pl.ds(pl.multiple_of(c * tk, tk), tk), :]
        s_sc[...] = _dot_nt(qidx_sc[...], ikc)
        bound = q0 - c * tk
        for r in range(nrb):
            accs = [jnp.zeros((rb, LANES), F32) for _ in range(ncol)]
            for h in range(H_IDX):
                wv = wrep_sc[h, r * rb:(r + 1) * rb, :]
                for j in range(ncol):
                    s = s_sc[h * tq + r * rb:h * tq + (r + 1) * rb, j * LANES:(j + 1) * LANES]
                    accs[j] = accs[j] + wv * jnp.maximum(s, 0.0)
            for j in range(ncol):
                acc = jnp.where(lane_minus_row <= bound + (r * rb - j * LANES), accs[j], -jnp.inf)
                keys_sc[c, r * rb:(r + 1) * rb, j * LANES:(j + 1) * LANES] = _score_key(
                    acc, c * tk + j * LANES + lane_rb)
        return 0

    lax.fori_loop(0, nchunk, score_body, 0)

    any_tied = _topk_search(keys_sc, cand_sc, t_sc, tu_sc, aux_sc, cnt_sc, nchunk=nchunk, rows=tq, tk=tk, hb=hb,
                            topk=topk, last_chunk_first_group=(tq - tk) // hb)

    @pl.when(any_tied)
    def _():
        def drop_body(c, _):
            for g in range(tq // hb):
                tb = t_sc[g * hb:(g + 1) * hb, :]
                xb = tu_sc[g * hb:(g + 1) * hb, :]
                for j in range(ncol):
                    kj = keys_sc[c, g * hb:(g + 1) * hb, j * LANES:(j + 1) * LANES]
                    pos = c * tk + j * LANES + lane_hb
                    late = jnp.where(kj == tb, pos, -1) > xb
                    keys_sc[c, g * hb:(g + 1) * hb, j * LANES:(j + 1) * LANES] = jnp.where(late, kj - 1, kj)
            return 0

        lax.fori_loop(0, nchunk, drop_body, 0)

    t_sc[...] = jnp.maximum(t_sc[...], NEGINF_KEY + 1)
    m_sc[...] = jnp.full(m_sc.shape, NEG, F32)
    l_sc[...] = jnp.zeros(l_sc.shape, F32)
    acc_sc[...] = jnp.zeros(acc_sc.shape, F32)

    def att_body(c, _):
        start = pl.multiple_of(c * tk, tk)
        kaug = jnp.concatenate([k_ref[pl.ds(start, tk), :], kpos_ref[pl.ds(start, tk), :]], axis=1)
        lg_sc[...] = _dot_nt(qa_sc[...], kaug)
        for r in range(nrb):
            tb = t_sc[r * rb:(r + 1) * rb, :]
            selb = [jnp.where(keys_sc[c, r * rb:(r + 1) * rb, j * LANES:(j + 1) * LANES] >= tb, 0.0, NEG)
                    for j in range(ncol)]
            for hh in range(H_A):
                rows = slice(hh * tq + r * rb, hh * tq + (r + 1) * rb)
                lgs = [lg_sc[rows, j * LANES:(j + 1) * LANES] + selb[j] for j in range(ncol)]
                mx = lgs[0]
                for j in range(1, ncol):
                    mx = jnp.maximum(mx, lgs[j])
                m_old = m_sc[rows, :]
                m_new = jnp.maximum(m_old, jnp.max(mx, axis=1, keepdims=True))
                alpha = jnp.exp(m_old - m_new)
                ps = [jnp.exp(lg - m_new) for lg in lgs]
                psum = ps[0]
                for j in range(1, ncol):
                    psum = psum + ps[j]
                l_sc[rows, :] = alpha * l_sc[rows, :] + psum
                for j in range(ncol):
                    p_sc[rows, j * LANES:(j + 1) * LANES] = ps[j].astype(BF16)
                alpha_sc[rows, :] = alpha
                m_sc[rows, :] = m_new
        acc_sc[...] = alpha_sc[...] * acc_sc[...] + _dot(p_sc[...], v_ref[pl.ds(start, tk), :])
        return 0

    lax.fori_loop(0, nchunk, att_body, 0)
    for hh in range(H_A):
        rows = slice(hh * tq, (hh + 1) * tq)
        l = jnp.sum(l_sc[rows, :], axis=1, keepdims=True)
        o_ref[:, hh * DH_A:(hh + 1) * DH_A] = (acc_sc[rows, :] / l).astype(BF16)


def _alibi_tables(seq, tq):
    pos = np.arange(seq)
    hi, lo = (pos // 64).astype(np.float32), (pos % 64).astype(np.float32)
    ktab = np.zeros((seq, DH_A), np.float32)
    ktab[:, 0], ktab[:, 1], ktab[:, 2], ktab[:, 3] = 64.0 * hi, lo, 1.0, 1.0
    nq = seq // tq
    qtab = np.zeros((nq, H_A, tq, DH_A), np.float32)
    for hh, slope in enumerate(ALIBI_SLOPES):
        qtab[:, hh, :, 0] = slope
        qtab[:, hh, :, 1] = slope
        qtab[:, hh, :, 2] = (-slope * 64.0 * hi).reshape(nq, tq)
        qtab[:, hh, :, 3] = (-slope * lo).reshape(nq, tq)
    return jnp.asarray(qtab.reshape(nq, H_A * tq, DH_A), BF16), jnp.asarray(ktab, BF16)


def _dsa_prompt(iq, small, aq, ikb, kb, vb, nb, seq, tq):
    n = iq.shape[0]
    nq = seq // tq
    topk = min(TOPK_MAX, seq // 4)
    tk = min(256, tq)
    rb = min(64, tq)
    hb = min(128, tq)
    assert seq // LANES <= 256
    qtab, ktab = _alibi_tables(seq, tq)

    def tok(w):
        return pl.BlockSpec((tq, w), lambda b, i: (b * nq + i, 0))

    def per_b(w):
        return pl.BlockSpec((seq, w), lambda b, i: (b, 0))

    return pl.pallas_call(
        functools.partial(_dsa_prompt_kernel, tq=tq, tk=tk, topk=topk, rb=rb, hb=hb),
        grid=(nb, nq),
        in_specs=[tok(512), tok(16), tok(512),
                  pl.BlockSpec((1, H_A * tq, DH_A), lambda b, i: (i, 0, 0)),
                  per_b(D_IDX), per_b(DH_A), per_b(DH_A),
                  pl.BlockSpec((seq, DH_A), lambda b, i: (0, 0))],
        out_specs=tok(512),
        out_shape=jax.ShapeDtypeStruct((n, 512), BF16),
        scratch_shapes=[pltpu.VMEM((H_IDX * tq, D_IDX), BF16),
                        pltpu.VMEM((H_IDX, tq, LANES), F32),
                        pltpu.VMEM((H_A * tq, 2 * DH_A), BF16),
                        pltpu.VMEM((H_IDX * tq, tk), F32),
                        pltpu.VMEM((seq // tk, tq, tk), I32),
                        pltpu.VMEM((H_A * tq, tk), F32),
                        pltpu.VMEM((H_A * tq, tk), BF16),
                        pltpu.VMEM((H_A * tq, LANES), F32),
                        pltpu.VMEM((H_A * tq, LANES), F32),
                        pltpu.VMEM((H_A * tq, LANES), F32),
                        pltpu.VMEM((H_A * tq, DH_A), F32),
                        pltpu.VMEM((tq, LANES), I32),
                        pltpu.VMEM((tq, LANES), I32),
                        pltpu.VMEM((tq, LANES), I32),
                        pltpu.VMEM((tq, LANES), F32),
                        pltpu.VMEM((tq, LANES), F32)],
        compiler_params=_cparams(("parallel", "arbitrary")),
        name="dsa_prompt",
    )(iq, small, aq, qtab, ikb, kb, vb, ktab)


def _page_pipeline(pt_ref, copies_of, n_pages):
    b = pl.program_id(0)
    slot = b % 2

    def for_all(bb, sl, act):
        def body(j, _):
            for cp in copies_of(pt_ref[bb, j], sl, j):
                act(cp)
            return 0
        lax.fori_loop(0, n_pages, body, 0)

    @pl.when(b == 0)
    def _():
        for_all(0, 0, lambda cp: cp.start())

    @pl.when(b + 1 < pl.num_programs(0))
    def _():
        for_all(b + 1, 1 - slot, lambda cp: cp.start())

    for_all(b, slot, lambda cp: cp.wait())
    return slot


def _sample_scores_kernel(pt_ref, iq_ref, w_ref, iknew_ref, ckidx_hbm, keys_ref, ibuf, sems,
                          *, n_pages, n_new, cw):
    past = n_pages * PAGE
    total = past + LANES

    def copies_of(pg, sl, j):
        dst = pl.ds(pl.multiple_of(j * PAGE, PAGE), PAGE)
        return (pltpu.make_async_copy(ckidx_hbm.at[pg], ibuf.at[sl, :, dst], sems.at[sl]),)

    slot = _page_pipeline(pt_ref, copies_of, n_pages)

    iq = iq_ref[0]
    w = w_ref[0]

    def scores(dots):
        s = jnp.maximum(dots, 0.0) * w
        return jnp.sum(s.reshape(n_new, H_IDX, s.shape[-1]), axis=1)

    for ch in range(past // cw):
        sc = scores(_dot(iq, ibuf[slot, :, ch * cw:(ch + 1) * cw].astype(BF16)))
        keys_ref[0, :, ch * cw:(ch + 1) * cw] = _score_key(sc, ch * cw + lax.broadcasted_iota(I32, sc.shape, 1))
    sc = scores(_dot_nt(iq, iknew_ref[0]))
    t_i = lax.broadcasted_iota(I32, (n_new, LANES), 0)
    j_i = lax.broadcasted_iota(I32, (n_new, LANES), 1)
    keys_ref[0, :, past:total] = _score_key(jnp.where(j_i <= t_i, sc, -jnp.inf), past + j_i)


def _sample_search_kernel(keys_ref, t_ref, x_ref, cand_sc, aux_sc, cnt_sc, *, nchunk, rows, topk):
    _topk_search(keys_ref, cand_sc, t_ref, x_ref, aux_sc, cnt_sc,
                 nchunk=nchunk, rows=rows, tk=LANES, hb=rows, topk=topk)


def _sample_attend_kernel(pt_ref, q_ref, knew_ref, vnew_ref, keys_ref, t_ref, x_ref, ck_hbm, cv_hbm, o_ref,
                          kbuf, vbuf, sems, lg_sc, *, n_pages, n_new, cw):
    past = n_pages * PAGE
    total = past + LANES
    rows_q = n_new * H_A

    def copies_of(pg, sl, j):
        dst = pl.ds(pl.multiple_of(j * PAGE, PAGE), PAGE)
        return (pltpu.make_async_copy(ck_hbm.at[pg], kbuf.at[sl, dst, :], sems.at[0, sl]),
                pltpu.make_async_copy(cv_hbm.at[pg], vbuf.at[sl, dst, :], sems.at[1, sl]))

    slot = _page_pipeline(pt_ref, copies_of, n_pages)

    r_i = lax.broadcasted_iota(I32, (rows_q, 1), 0)
    r_t = r_i // H_A
    r_h = r_i % H_A
    slope = jnp.zeros((rows_q, 1), F32)
    for hh in range(H_A):
        slope = jnp.where(r_h == hh, ALIBI_SLOPES[hh], slope)
    qposf = (past + r_t).astype(F32)
    q = q_ref[0]

    def sel_bias(lo, width):
        pos = lo + lax.broadcasted_iota(I32, (1, width), 1)
        out = jnp.full((rows_q, width), NEG, F32)
        for t in range(n_new):
            kt = keys_ref[0, t:t + 1, lo:lo + width]
            thr = t_ref[0, t:t + 1, 0:1]
            keep_tie = jnp.logical_and(kt == thr, pos <= x_ref[0, t:t + 1, 0:1])
            chosen = jnp.logical_and(jnp.logical_or(kt > thr, keep_tie), kt > NEGINF_KEY)
            bias_t = jnp.where(chosen, 0.0, NEG)
            out = jnp.where(r_t == t, bias_t, out)
        return out

    def logits(k_chunk, lo, width):
        kposf = (lo + lax.broadcasted_iota(I32, (1, width), 1)).astype(F32)
        sel = sel_bias(lo, width)
        lg = _dot_nt(q, k_chunk) - slope * (qposf - kposf)
        return lg + sel

    for ch in range(past // cw):
        lg_sc[:, ch * cw:(ch + 1) * cw] = logits(kbuf[slot, ch * cw:(ch + 1) * cw, :].astype(BF16), ch * cw, cw)
    lg_sc[:, past:total] = logits(knew_ref[0], past, LANES)

    lg = lg_sc[...]
    m = jnp.max(lg, axis=1, keepdims=True)
    p = jnp.exp(lg - m)
    l = jnp.sum(p, axis=1, keepdims=True)
    pb = p.astype(BF16)
    acc = _dot(pb[:, past:total], vnew_ref[0])
    for ch in range(past // cw):
        acc = acc + _dot(pb[:, ch * cw:(ch + 1) * cw], vbuf[slot, ch * cw:(ch + 1) * cw, :].astype(BF16))
    o_ref[0] = (acc / l).astype(BF16)


def _dsa_sample(page_table, iq32, w32, q16, iknew, knew, vnew, cache_kidx, cache_k, cache_v, n_new):
    db, n_pages = page_table.shape
    past = n_pages * PAGE
    total = past + LANES
    topk = min(TOPK_MAX, (past + n_new) // 4)
    cw = 1024 if past % 1024 == 0 else PAGE
    rows_q = n_new * H_A

    n_rows = db * n_new
    nchunk = total // LANES
    srows = _pick_tile(n_rows, 128)

    def per_b(a):
        return pl.BlockSpec((1,) + a.shape[1:], lambda b, pt: (b, 0, 0))

    hbm = pl.BlockSpec(memory_space=pl.ANY)

    keys = pl.pallas_call(
        functools.partial(_sample_scores_kernel, n_pages=n_pages, n_new=n_new, cw=cw),
        grid_spec=pltpu.PrefetchScalarGridSpec(
            num_scalar_prefetch=1,
            grid=(db,),
            in_specs=[per_b(iq32), per_b(w32), per_b(iknew), hbm],
            out_specs=pl.BlockSpec((1, n_new, total), lambda b, pt: (b, 0, 0)),
            scratch_shapes=[pltpu.VMEM((2, D_IDX, past), F32), pltpu.SemaphoreType.DMA((2,))],
        ),
        out_shape=jax.ShapeDtypeStruct((db, n_new, total), I32),
        compiler_params=_cparams(("arbitrary",)),
        name="sample_scores",
    )(page_table, iq32, w32, iknew, cache_kidx)

    keys_cm = keys.reshape(n_rows, nchunk, LANES).transpose(1, 0, 2)
    thr, cut = pl.pallas_call(
        functools.partial(_sample_search_kernel, nchunk=nchunk, rows=srows, topk=topk),
        grid=(n_rows // srows,),
        in_specs=[pl.BlockSpec((nchunk, srows, LANES), lambda i: (0, i, 0))],
        out_specs=[pl.BlockSpec((srows, LANES), lambda i: (i, 0))] * 2,
        out_shape=[jax.ShapeDtypeStruct((n_rows, LANES), I32)] * 2,
        scratch_shapes=[pltpu.VMEM((srows, LANES), I32), pltpu.VMEM((srows, LANES), F32),
                        pltpu.VMEM((srows, LANES), F32)],
        compiler_params=_cparams(("parallel",)),
        name="sample_search",
    )(keys_cm)

    thr3 = thr.reshape(db, n_new, LANES)
    cut3 = cut.reshape(db, n_new, LANES)
    return pl.pallas_call(
        functools.partial(_sample_attend_kernel, n_pages=n_pages, n_new=n_new, cw=cw),
        grid_spec=pltpu.PrefetchScalarGridSpec(
            num_scalar_prefetch=1,
            grid=(db,),
            in_specs=[per_b(q16), per_b(knew), per_b(vnew), per_b(keys), per_b(thr3), per_b(cut3), hbm, hbm],
            out_specs=pl.BlockSpec((1, rows_q, DH_A), lambda b, pt: (b, 0, 0)),
            scratch_shapes=[pltpu.VMEM((2, past, DH_A), F32), pltpu.VMEM((2, past, DH_A), F32),
                            pltpu.SemaphoreType.DMA((2, 2)), pltpu.VMEM((rows_q, total), F32)],
        ),
        out_shape=jax.ShapeDtypeStruct((db, rows_q, DH_A), BF16),
        compiler_params=_cparams(("arbitrary",)),
        name="sample_attend",
    )(page_table, q16, knew, vnew, keys, thr3, cut3, cache_k, cache_v)


def _outffn_kernel(x_ref, hm_ref, ha_ref, mod_ref, g2_ref, wom_ref, woa_ref, wg_ref, wu_ref, wd_ref, y_ref):
    mix = _dot(hm_ref[...], wom_ref[...]) + _dot(ha_ref[...], woa_ref[...])
    x1 = x_ref[...] + mod_ref[0, 0] * mix
    xn = x1 * lax.rsqrt(jnp.mean(x1 * x1, axis=-1, keepdims=True) + EPS) * g2_ref[...]
    hb = (xn * (1.0 + mod_ref[2, 0]) + mod_ref[1, 0]).astype(BF16)
    g = _dot(hb, wg_ref[...])
    u = _dot(hb, wu_ref[...])
    act = (g * _sigmoid(g) * u).astype(BF16)
    y_ref[...] = x1 + mod_ref[3, 0] * _dot(act, wd_ref[...])


def _outffn(x, hm, ha, mod_out, per_token_mod, rows_per_mod, g2, wom, woa, wg, wu, wd, tm):
    n, d = x.shape
    if per_token_mod:
        mod_spec = pl.BlockSpec((4, 1, tm, d), lambda i: (0, 0, i, 0))
    else:
        tiles = rows_per_mod // tm
        mod_spec = pl.BlockSpec((4, 1, 1, d), lambda i: (0, i // tiles, 0, 0))

    def full(a):
        return pl.BlockSpec(a.shape, lambda i: (0,) * a.ndim, pipeline_mode=pl.Buffered(1))

    def rows(w):
        return pl.BlockSpec((tm, w), lambda i: (i, 0))

    return pl.pallas_call(
        _outffn_kernel,
        grid=(n // tm,),
        in_specs=[rows(d), rows(512), rows(512), mod_spec, full(g2), full(wom), full(woa),
                  full(wg), full(wu), full(wd)],
        out_specs=rows(d),
        out_shape=jax.ShapeDtypeStruct((n, d), F32),
        compiler_params=_cparams(("parallel",)),
        name="outffn",
    )(x, hm, ha, mod_out, g2, wom, woa, wg, wu, wd)


def _pick_tile(n, pref):
    t = pref
    while n % t:
        t //= 2
    return t


def kernel(x_prompt, x_sample, cache_k, cache_v, cache_kidx, state_C, state_n, state_m, page_table,
           c_prompt, c_sample, w_ada, b_ada, g_norm1, w_in, b_igate, b_fgate, mlstm_norm_g,
           q_norm_g, k_norm_g, w_out, g_norm2, w_gate, w_up, w_down):
    bp, seq, d = x_prompt.shape
    db, t_new, _ = x_sample.shape
    n_p, n_s = bp * seq, db * t_new

    o = np.cumsum([0, 512, 512, 512, 512, H_M, H_M, 512, DH_A, DH_A, H_IDX * D_IDX, D_IDX, H_IDX])
    mq, mk, mv, mo, mi, mf, aq, ak, av, iq, ik, iw = [w_in[:, int(o[j]):int(o[j + 1])] for j in range(12)]
    zpad = lambda wdt: jnp.zeros((d, wdt), w_in.dtype)
    w_r = jnp.concatenate([mq, mk, mv, mo, aq, ak, av, iq, ik, zpad(LANES - D_IDX),
                           iw, mi, mf, zpad(LANES - 16)], axis=1).astype(BF16)
    wgt = jnp.concatenate([mi, mf], axis=1).T.astype(BF16)
    bsm = jnp.concatenate([jnp.zeros((H_IDX,), F32), b_igate, b_fgate]).reshape(1, 16)
    brow = jnp.concatenate([b_igate, b_fgate]).reshape(8, 1)
    g1 = g_norm1.reshape(1, d)
    g2 = g_norm2.reshape(1, d)
    qg = q_norm_g.reshape(1, DH_A)
    kg = k_norm_g.reshape(1, DH_A)
    ng = mlstm_norm_g.reshape(1, H_M * DH_M)
    wom = w_out[0:H_M * DH_M].astype(BF16)
    woa = w_out[H_M * DH_M:].astype(BF16)
    wg = w_gate.astype(BF16)
    wu = w_up.astype(BF16)
    wd = w_down.astype(BF16)

    mod = _ada(jnp.concatenate([c_prompt, c_sample], axis=0), w_ada.astype(BF16), b_ada)
    mod_p = mod[:bp].reshape(bp, 6, 1, d).transpose(1, 0, 2, 3)
    mod_s = jnp.repeat(mod[bp:].reshape(db, 6, d), t_new, axis=0)
    mod_s = mod_s.transpose(1, 0, 2).reshape(6, 1, n_s, d)

    tm_p = _pick_tile(seq, 512)
    (mqkv, mo_p, aq_p, k_p, v_p, kidx_p, kb, vb, ikb, iq_p, small_p, grow_p) = _inproj(
        x_prompt.reshape(n_p, d), mod_p[0:2], False, seq, g1, w_r, wgt, bsm, brow, qg, kg, tm_p)

    lc = _pick_tile(seq, 256)
    grow3 = grow_p.reshape(8, n_p // lc, lc).transpose(1, 0, 2)
    hm_p, C_p, n_pst, m_pb = _mlstm(
        mqkv, small_p, grow3, mo_p, ng,
        jnp.zeros((bp, H_M, DH_M, DH_M), F32), jnp.zeros((bp, H_M, DH_M), F32),
        jnp.zeros((bp, H_M, LANES), F32), bp, lc)

    tq = _pick_tile(seq, 512)
    ha_p = _dsa_prompt(iq_p, small_p, aq_p, ikb, kb, vb, bp, seq, tq)

    y_p = _outffn(x_prompt.reshape(n_p, d), hm_p, ha_p, mod_p[2:6], False, seq, g2,
                  wom, woa, wg, wu, wd, tm_p)

    tm_s = _pick_tile(n_s, 128)
    (mqkv_s, mo_s, aq_s, k_s, v_s, kidx_s, kb_s, vb_s, ikb_s, iq_s, small_s, grow_s) = _inproj(
        x_sample.reshape(n_s, d), mod_s[0:2], True, 0, g1, w_r, wgt, bsm, brow, qg, kg, tm_s)

    lp = 16
    pad_tok = lambda a: jnp.pad(a.reshape(db, t_new, a.shape[-1]),
                                ((0, 0), (0, lp - t_new), (0, 0))).reshape(db * lp, a.shape[-1])
    gate_pad = jnp.concatenate([jnp.zeros((H_IDX,), F32), jnp.full((H_M,), -jnp.inf, F32),
                                jnp.zeros((H_M,), F32)])
    small_pad = jnp.concatenate(
        [small_s.reshape(db, t_new, 16), jnp.broadcast_to(gate_pad, (db, lp - t_new, 16))], axis=1
    ).reshape(db * lp, 16)
    grow_pad = jnp.concatenate(
        [grow_s.reshape(8, db, t_new),
         jnp.broadcast_to(gate_pad[H_IDX:].reshape(8, 1, 1), (8, db, lp - t_new))], axis=2
    ).transpose(1, 0, 2)
    hm_s_pad, C_s, n_sst, m_sb = _mlstm(
        pad_tok(mqkv_s), small_pad, grow_pad, pad_tok(mo_s), ng,
        state_C, state_n, jnp.broadcast_to(state_m[:, :, None], (db, H_M, LANES)), db, lp)
    hm_s = hm_s_pad.reshape(db, lp, H_M * DH_M)[:, :t_new].reshape(n_s, H_M * DH_M)

    pad_rows = lambda a: jnp.pad(a.reshape(db, t_new, a.shape[-1]), ((0, 0), (0, LANES - t_new), (0, 0)))
    ha_s = _dsa_sample(
        page_table, iq_s.reshape(db, t_new * H_IDX, D_IDX),
        small_s[:, 0:H_IDX].reshape(db, t_new * H_IDX, 1),
        aq_s.reshape(db, t_new * H_A, DH_A),
        pad_rows(ikb_s), pad_rows(kb_s), pad_rows(vb_s),
        jnp.swapaxes(cache_kidx, 1, 2), cache_k, cache_v, t_new).reshape(n_s, H_A * DH_A)

    y_s = _outffn(x_sample.reshape(n_s, d), hm_s, ha_s, mod_s[2:6], True, 0, g2,
                  wom, woa, wg, wu, wd, tm_s)

    return (y_p.reshape(bp, seq, d), y_s.reshape(db, t_new, d),
            k_p.reshape(bp, seq, DH_A), v_p.reshape(bp, seq, DH_A), kidx_p.reshape(bp, seq, D_IDX),
            C_p, n_pst, m_pb[:, :, 0],
            k_s.reshape(db, t_new, DH_A), v_s.reshape(db, t_new, DH_A), kidx_s.reshape(db, t_new, D_IDX),
            C_s, n_sst, m_sb[:, :, 0])
```

```python
import functools

import jax
import jax.numpy as jnp
import numpy as np
from jax import lax
from jax.experimental import pallas as pl
from jax.experimental.pallas import tpu as pltpu

F32 = jnp.float32
BF16 = jnp.bfloat16
I32 = jnp.int32

H_M = 4
DH_M = 128
H_A = 4
DH_A = 128
H_IDX = 8
D_IDX = 64
TOPK_MAX = 256
PAGE = 128
EPS = 1e-6
INDEX_SCALE = D_IDX ** -0.5 * H_IDX ** -0.5
ALIBI_SLOPES = tuple(float(2.0 ** (-8.0 * (h + 1) / H_A)) for h in range(H_A))

LANES = 128
VMEM_LIMIT = 56 * 1024 * 1024
NEG = -1e30
I32_MIN = -2 ** 31
F32_TINY = float(np.finfo(np.float32).tiny)
NEGINF_KEY = -2139095041
INDEX_BITS = 14

C_MQ, C_MK, C_MV, C_MO, C_AQ, C_AK, C_AV, C_IQ, C_IK, C_SM, C_END = (
    0, 512, 1024, 1536, 2048, 2560, 2688, 2816, 3328, 3456, 3584)


def _cparams(sem):
    return pltpu.CompilerParams(dimension_semantics=sem, vmem_limit_bytes=VMEM_LIMIT)


def _sigmoid(x):
    return 1.0 / (1.0 + jnp.exp(-x))


def _log_sigmoid(x):
    return jnp.minimum(x, 0.0) - jnp.log1p(jnp.exp(-jnp.abs(x)))


def _dot(a, b):
    return jnp.dot(a, b, preferred_element_type=F32)


def _dot_nt(a, b):
    return lax.dot_general(a, b, (((1,), (1,)), ((), ())), preferred_element_type=F32)


def _dot_tn(a, b):
    return lax.dot_general(a, b, (((0,), (0,)), ((), ())), preferred_element_type=F32)


def _split3(x):
    hi = x.astype(BF16)
    r1 = x - hi.astype(F32)
    mid = r1.astype(BF16)
    lo = (r1 - mid.astype(F32)).astype(BF16)
    return hi, mid, lo


def _sort_key(x):
    b = pltpu.bitcast(x, I32)
    return b ^ ((b >> 31) & 0x7FFFFFFF)


def _score_key(score, pos):
    return jnp.where(jnp.abs(score) < F32_TINY, -pos, _sort_key(score))


def _ada_kernel(c_ref, w_ref, b_ref, o_ref):
    c = c_ref[...]
    s = c * _sigmoid(c)
    o_ref[...] = _dot(s.astype(BF16), w_ref[...]) + b_ref[...]


def _ada(c, w_bf, b):
    r, d = c.shape
    n = w_bf.shape[1]
    return pl.pallas_call(
        _ada_kernel,
        grid=(n // d,),
        in_specs=[pl.BlockSpec((r, d), lambda j: (0, 0)),
                  pl.BlockSpec((d, d), lambda j: (0, j)),
                  pl.BlockSpec((1, d), lambda j: (0, j))],
        out_specs=pl.BlockSpec((r, d), lambda j: (0, j)),
        out_shape=jax.ShapeDtypeStruct((r, n), F32),
        compiler_params=_cparams(("arbitrary",)),
        name="ada",
    )(c, w_bf, b.reshape(1, n))


def _inproj_kernel(x_ref, mod_ref, g1_ref, w_ref, wgt_ref, bsm_ref, brow_ref, qg_ref, kg_ref,
                   mqkv_ref, mo_ref, aq_ref, k_ref, v_ref, kidx_ref, kb_ref, vb_ref, ikb_ref,
                   iq_ref, small_ref, grow_ref):
    x = x_ref[...]
    xn = x * lax.rsqrt(jnp.mean(x * x, axis=-1, keepdims=True) + EPS) * g1_ref[...]
    h = xn * (1.0 + mod_ref[1, 0]) + mod_ref[0, 0]
    hb = h.astype(BF16)

    def sec(a, b):
        return _dot(hb, w_ref[:, a:b])

    mqkv_ref[:, 0:512] = sec(C_MQ, C_MK).astype(BF16)
    mqkv_ref[:, 512:1024] = (sec(C_MK, C_MV) * (DH_M ** -0.5)).astype(BF16)
    mqkv_ref[:, 1024:1536] = sec(C_MV, C_MO).astype(BF16)
    mo_ref[...] = sec(C_MO, C_AQ)

    aq = sec(C_AQ, C_AK)
    qg = qg_ref[...]
    for hh in range(H_A):
        a = aq[:, hh * DH_A:(hh + 1) * DH_A]
        a = a * lax.rsqrt(jnp.mean(a * a, axis=-1, keepdims=True) + EPS) * qg
        aq_ref[:, hh * DH_A:(hh + 1) * DH_A] = (a * (DH_A ** -0.5)).astype(BF16)

    ak = sec(C_AK, C_AV)
    ak = ak * lax.rsqrt(jnp.mean(ak * ak, axis=-1, keepdims=True) + EPS) * kg_ref[...]
    k_ref[...] = ak
    kb_ref[...] = ak.astype(BF16)
    av = sec(C_AV, C_IQ)
    v_ref[...] = av
    vb_ref[...] = av.astype(BF16)

    iq_ref[...] = sec(C_IQ, C_IK).astype(BF16)
    ik = sec(C_IK, C_SM)[:, 0:D_IDX]
    kidx_ref[...] = ik
    ikb_ref[...] = ik.astype(BF16)

    sm = sec(C_SM, C_END)[:, 0:16] + bsm_ref[...]
    col = lax.broadcasted_iota(I32, sm.shape, 1)
    small_ref[...] = jnp.where(col < H_IDX, sm * INDEX_SCALE,
                               jnp.where(col < H_IDX + H_M, sm, _log_sigmoid(sm)))

    gr = _dot_nt(wgt_ref[...], hb) + brow_ref[...]
    row = lax.broadcasted_iota(I32, gr.shape, 0)
    grow_ref[...] = jnp.where(row < H_M, gr, _log_sigmoid(gr))


def _inproj(x, mod_in, per_token_mod, rows_per_mod, g1, w_r, wgt, bsm, brow, qg, kg, tm):
    n, d = x.shape
    if per_token_mod:
        mod_spec = pl.BlockSpec((2, 1, tm, d), lambda i: (0, 0, i, 0))
    else:
        tiles = rows_per_mod // tm
        mod_spec = pl.BlockSpec((2, 1, 1, d), lambda i: (0, i // tiles, 0, 0))

    def full(a):
        return pl.BlockSpec(a.shape, lambda i: (0,) * a.ndim)

    def rows(w):
        return pl.BlockSpec((tm, w), lambda i: (i, 0))

    outs = [(1536, BF16), (512, F32), (512, BF16), (128, F32), (128, F32), (D_IDX, F32),
            (128, BF16), (128, BF16), (D_IDX, BF16), (512, BF16), (16, F32)]
    out_shape = [jax.ShapeDtypeStruct((n, w), dt) for w, dt in outs]
    out_specs = [rows(w) for w, _ in outs]
    out_shape.append(jax.ShapeDtypeStruct((8, n), F32))
    out_specs.append(pl.BlockSpec((8, tm), lambda i: (0, i)))
    return pl.pallas_call(
        _inproj_kernel,
        grid=(n // tm,),
        in_specs=[rows(d), mod_spec, full(g1), full(w_r), full(wgt), full(bsm), full(brow),
                  full(qg), full(kg)],
        out_specs=out_specs,
        out_shape=out_shape,
        compiler_params=_cparams(("parallel",)),
        name="inproj",
    )(x, mod_in, g1, w_r, wgt, bsm, brow, qg, kg)


def _mlstm_kernel(q_ref, k_ref, v_ref, small_ref, grow_ref, o_ref, ng_ref, c0_ref, n0_ref, m0_ref,
                  h_ref, c_out_ref, n_out_ref, m_out_ref, state_sc, m_sc, *, L, nc):
    c = pl.program_id(1)

    @pl.when(c == 0)
    def _():
        for hh in range(H_M):
            state_sc[hh, 0:DH_M, :] = c0_ref[0, hh]
            state_sc[hh, DH_M:2 * DH_M, :] = jnp.broadcast_to(n0_ref[0, hh:hh + 1, :], (DH_M, DH_M))
        m_sc[...] = jnp.zeros(m_sc.shape, F32)
        m_sc[0:H_M, :] = m0_ref[0]

    ti = lax.broadcasted_iota(I32, (L, L), 0)
    si = lax.broadcasted_iota(I32, (L, L), 1)
    causal = si <= ti
    tri = jnp.where(causal, 1.0, 0.0).astype(BF16)
    tri_t = jnp.where(ti <= si, 1.0, 0.0).astype(BF16)

    sm = small_ref[...]
    col = lax.broadcasted_iota(I32, sm.shape, 1)
    lf_cols = jnp.where(col >= H_IDX + H_M, sm, 0.0)
    hi, mid, lo = _split3(lf_cols)
    b_cols = _dot(tri, hi) + _dot(tri, mid) + _dot(tri, lo)
    gr = grow_ref[0]
    row = lax.broadcasted_iota(I32, gr.shape, 0)
    lf_rows = jnp.where(row >= H_M, gr, 0.0)
    hi, mid, lo = _split3(lf_rows)
    b_rows = _dot(hi, tri_t) + _dot(mid, tri_t) + _dot(lo, tri_t)

    ones_blk = jnp.ones((L, DH_M), BF16)

    def across(x):
        return x[:, 0:L] if L <= LANES else jnp.concatenate([x] * (L // LANES), axis=1)

    for hh in range(H_M):
        q = q_ref[:, hh * DH_M:(hh + 1) * DH_M]
        k = k_ref[:, hh * DH_M:(hh + 1) * DH_M]
        v = v_ref[:, hh * DH_M:(hh + 1) * DH_M]
        ig_c = jnp.broadcast_to(sm[:, H_IDX + hh:H_IDX + hh + 1], (L, LANES))
        b_c = jnp.broadcast_to(b_cols[:, H_IDX + H_M + hh:H_IDX + H_M + hh + 1], (L, LANES))
        ig_r = gr[hh:hh + 1, :]
        b_r = b_rows[H_M + hh:H_M + hh + 1, :]
        m_prev = m_sc[hh:hh + 1, :]

        dmat = jnp.where(causal, across(b_c) - b_r + ig_r, -jnp.inf)
        inter = b_c + m_prev
        m_t = jnp.maximum(inter, jnp.max(dmat, axis=-1, keepdims=True))
        w = jnp.exp(dmat - across(m_t))
        g = jnp.exp(inter - m_t)
        a = w * _dot_nt(q, k)
        st = state_sc[hh]
        qs = _dot_nt(q, st.astype(BF16))
        num = g * qs[:, 0:DH_M] + _dot(a.astype(BF16), v)
        den = g * qs[:, DH_M:2 * DH_M] + jnp.sum(a, axis=-1, keepdims=True)
        hv = num / jnp.maximum(jnp.abs(den), jnp.exp(-m_t))

        m_new = m_t[L - 1:L, :]
        b_last = b_c[L - 1:L, :]
        wend = jnp.exp(b_last - b_c + ig_c - m_new)
        gend = jnp.exp(b_last + m_prev - m_new)
        kw = (k.astype(F32) * wend).astype(BF16)
        v_aug = jnp.concatenate([v, ones_blk], axis=1)
        state_sc[hh] = gend * st + _dot_tn(v_aug, kw)
        m_sc[hh:hh + 1, :] = m_new

        hn = hv * lax.rsqrt(jnp.mean(hv * hv, axis=-1, keepdims=True) + EPS)
        hn = hn * ng_ref[:, hh * DH_M:(hh + 1) * DH_M]
        hn = hn * _sigmoid(o_ref[:, hh * DH_M:(hh + 1) * DH_M])
        h_ref[:, hh * DH_M:(hh + 1) * DH_M] = hn.astype(BF16)

    @pl.when(c == nc - 1)
    def _():
        for hh in range(H_M):
            c_out_ref[0, hh] = state_sc[hh, 0:DH_M, :]
            n_out_ref[0, hh:hh + 1, :] = state_sc[hh, DH_M:DH_M + 1, :]
        m_out_ref[0] = m_sc[0:H_M, :]


def _mlstm(mqkv, small, grow3, mo, ng, c0, n0, m0b, nb, L):
    n = mqkv.shape[0]
    nc = n // nb // L
    d = H_M * DH_M

    def tok(w, blk=0):
        return pl.BlockSpec((L, w), lambda b, c, blk=blk: (b * nc + c, blk))

    per_b3 = pl.BlockSpec((1, H_M, LANES), lambda b, c: (b, 0, 0))
    per_b4 = pl.BlockSpec((1, H_M, DH_M, DH_M), lambda b, c: (b, 0, 0, 0))
    return pl.pallas_call(
        functools.partial(_mlstm_kernel, L=L, nc=nc),
        grid=(nb, nc),
        in_specs=[tok(d, 0), tok(d, 1), tok(d, 2), tok(16),
                  pl.BlockSpec((1, 8, L), lambda b, c: (b * nc + c, 0, 0)),
                  tok(d), pl.BlockSpec((1, d), lambda b, c: (0, 0)),
                  per_b4, per_b3, per_b3],
        out_specs=[tok(d), per_b4, per_b3, per_b3],
        out_shape=[jax.ShapeDtypeStruct((n, d), BF16),
                   jax.ShapeDtypeStruct((nb, H_M, DH_M, DH_M), F32),
                   jax.ShapeDtypeStruct((nb, H_M, DH_M), F32),
                   jax.ShapeDtypeStruct((nb, H_M, LANES), F32)],
        scratch_shapes=[pltpu.VMEM((H_M, 2 * DH_M, DH_M), F32), pltpu.VMEM((8, LANES), F32)],
        compiler_params=_cparams(("parallel", "arbitrary")),
        name="mlstm",
    )(mqkv, mqkv, mqkv, small, grow3, mo, ng, c0, n0, m0b)


def _topk_search(keys_sc, cand_sc, t_sc, tu_sc, aux_sc, cnt_sc, *, nchunk, rows, tk, hb, topk,
                 last_chunk_first_group=0):
    ncol = tk // LANES
    ngrp = rows // hb
    lane_hb = lax.broadcasted_iota(I32, (hb, LANES), 1)
    ones_mat = jnp.ones((LANES, LANES), BF16)

    def count_rows(hit_fn):
        cnt_sc[...] = jnp.zeros(cnt_sc.shape, F32)

        def body(c, _, first_group=0):
            for g in range(first_group, ngrp):
                acc = cnt_sc[g * hb:(g + 1) * hb, :]
                for j in range(ncol):
                    kj = keys_sc[c, g * hb:(g + 1) * hb, j * LANES:(j + 1) * LANES]
                    acc = acc + jnp.where(hit_fn(c, g, j, kj), 1.0, 0.0)
                cnt_sc[g * hb:(g + 1) * hb, :] = acc
            return 0

        lax.fori_loop(0, nchunk - 1, body, 0)
        body(nchunk - 1, 0, last_chunk_first_group)
        return _dot(cnt_sc[...].astype(BF16), ones_mat)

    def hit_ge(c, g, j, kj):
        return kj >= cand_sc[g * hb:(g + 1) * hb, :]

    tu_sc[...] = jnp.zeros(tu_sc.shape, I32)

    def search_pass(p, _):
        bit = jnp.left_shift(jnp.int32(1), 31 - p)
        cand_sc[...] = (tu_sc[...] | bit) ^ I32_MIN
        tu_sc[...] = jnp.where(count_rows(hit_ge) >= topk, tu_sc[...] | bit, tu_sc[...])
        return 0

    lax.fori_loop(0, 32, search_pass, 0)
    t_sc[...] = tu_sc[...] ^ I32_MIN

    cand_sc[...] = t_sc[...]
    tied = jnp.logical_and(count_rows(hit_ge) > topk, t_sc[...] > NEGINF_KEY)
    any_tied = jnp.max(jnp.where(tied, 1.0, 0.0)) > 0.0
    tu_sc[...] = jnp.full(tu_sc.shape, 2 ** 31 - 1, I32)

    @pl.when(any_tied)
    def _():
        cand_sc[...] = t_sc[...] + 1
        aux_sc[...] = topk - count_rows(hit_ge)

        def hit_tie_below(c, g, j, kj):
            pos = c * tk + j * LANES + lane_hb
            return jnp.where(kj == t_sc[g * hb:(g + 1) * hb, :], pos, 2 ** 30) < cand_sc[g * hb:(g + 1) * hb, :]

        tu_sc[...] = jnp.zeros(tu_sc.shape, I32)

        def cut_pass(p, _):
            bit = jnp.left_shift(jnp.int32(1), INDEX_BITS - 1 - p)
            cand_sc[...] = tu_sc[...] | bit
            tu_sc[...] = jnp.where(count_rows(hit_tie_below) < aux_sc[...], tu_sc[...] | bit, tu_sc[...])
            return 0

        lax.fori_loop(0, INDEX_BITS, cut_pass, 0)

    return any_tied


def _dsa_prompt_kernel(iq_ref, small_ref, aq_ref, qpos_ref, ik_ref, k_ref, v_ref, kpos_ref, o_ref,
                       qidx_sc, wrep_sc, qa_sc, s_sc, keys_sc, lg_sc, p_sc, alpha_sc, m_sc, l_sc, acc_sc,
                       cand_sc, t_sc, tu_sc, aux_sc, cnt_sc, *, tq, tk, topk, rb, hb):
    i = pl.program_id(1)
    q0 = i * tq
    nchunk = (i + 1) * (tq // tk)
    ncol = tk // LANES
    nrb = tq // rb
    lane_hb = lax.broadcasted_iota(I32, (hb, LANES), 1)
    lane_rb = lax.broadcasted_iota(I32, (rb, LANES), 1)
    lane_minus_row = lane_rb - lax.broadcasted_iota(I32, (rb, LANES), 0)

    for h in range(H_IDX):
        qidx_sc[h * tq:(h + 1) * tq, :] = iq_ref[:, h * D_IDX:(h + 1) * D_IDX]
        wrep_sc[h] = jnp.broadcast_to(small_ref[:, h:h + 1], (tq, LANES))
    for hh in range(H_A):
        qa_sc[hh * tq:(hh + 1) * tq, 0:DH_A] = aq_ref[:, hh * DH_A:(hh + 1) * DH_A]
        qa_sc[hh * tq:(hh + 1) * tq, DH_A:2 * DH_A] = qpos_ref[0, hh * tq:(hh + 1) * tq, :]

    row_lo = tq - tk

    def score_body(c, _, lo=0):
        ikc = ik_ref[pl.ds(pl.multiple_of(c * tk, tk), tk), :]
        if lo == 0:
            s_sc[...] = _dot_nt(qidx_sc[...], ikc)
        else:
            for h in range(H_IDX):
                s_sc[h * tq + lo:(h + 1) * tq, :] = _dot_nt(qidx_sc[h * tq + lo:(h + 1) * tq, :], ikc)
            keys_sc[c, 0:lo, :] = jnp.full((lo, tk), NEGINF_KEY, I32)
        bound = q0 - c * tk
        for r in range(lo // rb, nrb):
            accs = [jnp.zeros((rb, LANES), F32) for _ in range(ncol)]
            for h in range(H_IDX):
                wv = wrep_sc[h, r * rb:(r + 1) * rb, :]
                for j in range(ncol):
                    s = s_sc[h * tq + r * rb:h * tq + (r + 1) * rb, j * LANES:(j + 1) * LANES]
                    accs[j] = accs[j] + wv * jnp.maximum(s, 0.0)
            for j in range(ncol):
                acc = jnp.where(lane_minus_row <= bound + (r * rb - j * LANES), accs[j], -jnp.inf)
                keys_sc[c, r * rb:(r + 1) * rb, j * LANES:(j + 1) * LANES] = _score_key(
                    acc, c * tk + j * LANES + lane_rb)
        return 0

    lax.fori_loop(0, nchunk - 1, score_body, 0)
    score_body(nchunk - 1, 0, row_lo)

    any_tied = _topk_search(keys_sc, cand_sc, t_sc, tu_sc, aux_sc, cnt_sc, nchunk=nchunk, rows=tq, tk=tk, hb=hb,
                            topk=topk, last_chunk_first_group=(tq - tk) // hb)

    @pl.when(any_tied)
    def _():
        def drop_body(c, _):
            for g in range(tq // hb):
                tb = t_sc[g * hb:(g + 1) * hb, :]
                xb = tu_sc[g * hb:(g + 1) * hb, :]
                for j in range(ncol):
                    kj = keys_sc[c, g * hb:(g + 1) * hb, j * LANES:(j + 1) * LANES]
                    pos = c * tk + j * LANES + lane_hb
                    late = jnp.where(kj == tb, pos, -1) > xb
                    keys_sc[c, g * hb:(g + 1) * hb, j * LANES:(j + 1) * LANES] = jnp.where(late, kj - 1, kj)
            return 0

        lax.fori_loop(0, nchunk, drop_body, 0)

    t_sc[...] = jnp.maximum(t_sc[...], NEGINF_KEY + 1)
    m_sc[...] = jnp.full(m_sc.shape, NEG, F32)
    l_sc[...] = jnp.zeros(l_sc.shape, F32)
    acc_sc[...] = jnp.zeros(acc_sc.shape, F32)

    def att_body(c, _, lo=0):
        start = pl.multiple_of(c * tk, tk)
        kaug = jnp.concatenate([k_ref[pl.ds(start, tk), :], kpos_ref[pl.ds(start, tk), :]], axis=1)
        head_rows = [slice(hh * tq + lo, (hh + 1) * tq) for hh in range(H_A)]
        if lo == 0:
            lg_sc[...] = _dot_nt(qa_sc[...], kaug)
        else:
            for rows in head_rows:
                lg_sc[rows, :] = _dot_nt(qa_sc[rows, :], kaug)
        for r in range(lo // rb, nrb):
            tb = t_sc[r * rb:(r + 1) * rb, :]
            selb = [jnp.where(keys_sc[c, r * rb:(r + 1) * rb, j * LANES:(j + 1) * LANES] >= tb, 0.0, NEG)
                    for j in range(ncol)]
            for hh in range(H_A):
                rows = slice(hh * tq + r * rb, hh * tq + (r + 1) * rb)
                lgs = [lg_sc[rows, j * LANES:(j + 1) * LANES] + selb[j] for j in range(ncol)]
                mx = lgs[0]
                for j in range(1, ncol):
                    mx = jnp.maximum(mx, lgs[j])
                m_old = m_sc[rows, :]
                m_new = jnp.maximum(m_old, jnp.max(mx, axis=1, keepdims=True))
                alpha = jnp.exp(m_old - m_new)
                ps = [jnp.exp(lg - m_new) for lg in lgs]
                psum = ps[0]
                for j in range(1, ncol):
                    psum = psum + ps[j]
                l_sc[rows, :] = alpha * l_sc[rows, :] + psum
                for j in range(ncol):
                    p_sc[rows, j * LANES:(j + 1) * LANES] = ps[j].astype(BF16)
                alpha_sc[rows, :] = alpha
                m_sc[rows, :] = m_new
        vc = v_ref[pl.ds(start, tk), :]
        if lo == 0:
            acc_sc[...] = alpha_sc[...] * acc_sc[...] + _dot(p_sc[...], vc)
        else:
            for rows in head_rows:
                acc_sc[rows, :] = alpha_sc[rows, :] * acc_sc[rows, :] + _dot(p_sc[rows, :], vc)
        return 0

    lax.fori_loop(0, nchunk - 1, att_body, 0)
    att_body(nchunk - 1, 0, row_lo)
    for hh in range(H_A):
        rows = slice(hh * tq, (hh + 1) * tq)
        l = jnp.sum(l_sc[rows, :], axis=1, keepdims=True)
        o_ref[:, hh * DH_A:(hh + 1) * DH_A] = (acc_sc[rows, :] / l).astype(BF16)


def _alibi_tables(seq, tq):
    pos = np.arange(seq)
    hi, lo = (pos // 64).astype(np.float32), (pos % 64).astype(np.float32)
    ktab = np.zeros((seq, DH_A), np.float32)
    ktab[:, 0], ktab[:, 1], ktab[:, 2], ktab[:, 3] = 64.0 * hi, lo, 1.0, 1.0
    nq = seq // tq
    qtab = np.zeros((nq, H_A, tq, DH_A), np.float32)
    for hh, slope in enumerate(ALIBI_SLOPES):
        qtab[:, hh, :, 0] = slope
        qtab[:, hh, :, 1] = slope
        qtab[:, hh, :, 2] = (-slope * 64.0 * hi).reshape(nq, tq)
        qtab[:, hh, :, 3] = (-slope * lo).reshape(nq, tq)
    return jnp.asarray(qtab.reshape(nq, H_A * tq, DH_A), BF16), jnp.asarray(ktab, BF16)


def _dsa_prompt(iq, small, aq, ikb, kb, vb, nb, seq, tq):
    n = iq.shape[0]
    nq = seq // tq
    topk = min(TOPK_MAX, seq // 4)
    tk = min(256, tq)
    rb = min(64, tq)
    hb = min(128, tq)
    assert seq // LANES <= 256
    qtab, ktab = _alibi_tables(seq, tq)

    def tok(w):
        return pl.BlockSpec((tq, w), lambda b, i: (b * nq + i, 0))

    def per_b(w):
        return pl.BlockSpec((seq, w), lambda b, i: (b, 0))

    return pl.pallas_call(
        functools.partial(_dsa_prompt_kernel, tq=tq, tk=tk, topk=topk, rb=rb, hb=hb),
        grid=(nb, nq),
        in_specs=[tok(512), tok(16), tok(512),
                  pl.BlockSpec((1, H_A * tq, DH_A), lambda b, i: (i, 0, 0)),
                  per_b(D_IDX), per_b(DH_A), per_b(DH_A),
                  pl.BlockSpec((seq, DH_A), lambda b, i: (0, 0))],
        out_specs=tok(512),
        out_shape=jax.ShapeDtypeStruct((n, 512), BF16),
        scratch_shapes=[pltpu.VMEM((H_IDX * tq, D_IDX), BF16),
                        pltpu.VMEM((H_IDX, tq, LANES), F32),
                        pltpu.VMEM((H_A * tq, 2 * DH_A), BF16),
                        pltpu.VMEM((H_IDX * tq, tk), F32),
                        pltpu.VMEM((seq // tk, tq, tk), I32),
                        pltpu.VMEM((H_A * tq, tk), F32),
                        pltpu.VMEM((H_A * tq, tk), BF16),
                        pltpu.VMEM((H_A * tq, LANES), F32),
                        pltpu.VMEM((H_A * tq, LANES), F32),
                        pltpu.VMEM((H_A * tq, LANES), F32),
                        pltpu.VMEM((H_A * tq, DH_A), F32),
                        pltpu.VMEM((tq, LANES), I32),
                        pltpu.VMEM((tq, LANES), I32),
                        pltpu.VMEM((tq, LANES), I32),
                        pltpu.VMEM((tq, LANES), F32),
                        pltpu.VMEM((tq, LANES), F32)],
        compiler_params=_cparams(("parallel", "arbitrary")),
        name="dsa_prompt",
    )(iq, small, aq, qtab, ikb, kb, vb, ktab)


def _page_pipeline(pt_ref, copies_of, n_pages):
    b = pl.program_id(0)
    slot = b % 2

    def for_all(bb, sl, act):
        def body(j, _):
            for cp in copies_of(pt_ref[bb, j], sl, j):
                act(cp)
            return 0
        lax.fori_loop(0, n_pages, body, 0)

    @pl.when(b == 0)
    def _():
        for_all(0, 0, lambda cp: cp.start())

    @pl.when(b + 1 < pl.num_programs(0))
    def _():
        for_all(b + 1, 1 - slot, lambda cp: cp.start())

    for_all(b, slot, lambda cp: cp.wait())
    return slot


def _sample_scores_kernel(pt_ref, iq_ref, w_ref, iknew_ref, ckidx_hbm, keys_ref, ibuf, sems,
                          *, n_pages, n_new, cw):
    past = n_pages * PAGE
    total = past + LANES

    def copies_of(pg, sl, j):
        dst = pl.ds(pl.multiple_of(j * PAGE, PAGE), PAGE)
        return (pltpu.make_async_copy(ckidx_hbm.at[pg], ibuf.at[sl, :, dst], sems.at[sl]),)

    slot = _page_pipeline(pt_ref, copies_of, n_pages)

    iq = iq_ref[0]
    w = w_ref[0]

    def scores(dots):
        s = jnp.maximum(dots, 0.0) * w
        return jnp.sum(s.reshape(n_new, H_IDX, s.shape[-1]), axis=1)

    for ch in range(past // cw):
        sc = scores(_dot(iq, ibuf[slot, :, ch * cw:(ch + 1) * cw].astype(BF16)))
        keys_ref[0, :, ch * cw:(ch + 1) * cw] = _score_key(sc, ch * cw + lax.broadcasted_iota(I32, sc.shape, 1))
    sc = scores(_dot_nt(iq, iknew_ref[0]))
    t_i = lax.broadcasted_iota(I32, (n_new, LANES), 0)
    j_i = lax.broadcasted_iota(I32, (n_new, LANES), 1)
    keys_ref[0, :, past:total] = _score_key(jnp.where(j_i <= t_i, sc, -jnp.inf), past + j_i)


def _sample_search_kernel(keys_ref, t_ref, x_ref, cand_sc, aux_sc, cnt_sc, *, nchunk, rows, topk):
    _topk_search(keys_ref, cand_sc, t_ref, x_ref, aux_sc, cnt_sc,
                 nchunk=nchunk, rows=rows, tk=LANES, hb=rows, topk=topk)


def _sample_attend_kernel(pt_ref, q_ref, knew_ref, vnew_ref, keys_ref, t_ref, x_ref, ck_hbm, cv_hbm, o_ref,
                          kbuf, vbuf, sems, lg_sc, *, n_pages, n_new, cw):
    past = n_pages * PAGE
    total = past + LANES
    rows_q = n_new * H_A

    def copies_of(pg, sl, j):
        dst = pl.ds(pl.multiple_of(j * PAGE, PAGE), PAGE)
        return (pltpu.make_async_copy(ck_hbm.at[pg], kbuf.at[sl, dst, :], sems.at[0, sl]),
                pltpu.make_async_copy(cv_hbm.at[pg], vbuf.at[sl, dst, :], sems.at[1, sl]))

    slot = _page_pipeline(pt_ref, copies_of, n_pages)

    r_i = lax.broadcasted_iota(I32, (rows_q, 1), 0)
    r_t = r_i // H_A
    r_h = r_i % H_A
    slope = jnp.zeros((rows_q, 1), F32)
    for hh in range(H_A):
        slope = jnp.where(r_h == hh, ALIBI_SLOPES[hh], slope)
    qposf = (past + r_t).astype(F32)
    q = q_ref[0]

    def sel_bias(lo, width):
        pos = lo + lax.broadcasted_iota(I32, (1, width), 1)
        out = jnp.full((rows_q, width), NEG, F32)
        for t in range(n_new):
            kt = keys_ref[0, t:t + 1, lo:lo + width]
            thr = t_ref[0, t:t + 1, 0:1]
            keep_tie = jnp.logical_and(kt == thr, pos <= x_ref[0, t:t + 1, 0:1])
            chosen = jnp.logical_and(jnp.logical_or(kt > thr, keep_tie), kt > NEGINF_KEY)
            bias_t = jnp.where(chosen, 0.0, NEG)
            out = jnp.where(r_t == t, bias_t, out)
        return out

    def logits(k_chunk, lo, width):
        kposf = (lo + lax.broadcasted_iota(I32, (1, width), 1)).astype(F32)
        sel = sel_bias(lo, width)
        lg = _dot_nt(q, k_chunk) - slope * (qposf - kposf)
        return lg + sel

    for ch in range(past // cw):
        lg_sc[:, ch * cw:(ch + 1) * cw] = logits(kbuf[slot, ch * cw:(ch + 1) * cw, :].astype(BF16), ch * cw, cw)
    lg_sc[:, past:total] = logits(knew_ref[0], past, LANES)

    lg = lg_sc[...]
    m = jnp.max(lg, axis=1, keepdims=True)
    p = jnp.exp(lg - m)
    l = jnp.sum(p, axis=1, keepdims=True)
    pb = p.astype(BF16)
    acc = _dot(pb[:, past:total], vnew_ref[0])
    for ch in range(past // cw):
        acc = acc + _dot(pb[:, ch * cw:(ch + 1) * cw], vbuf[slot, ch * cw:(ch + 1) * cw, :].astype(BF16))
    o_ref[0] = (acc / l).astype(BF16)


def _dsa_sample(page_table, iq32, w32, q16, iknew, knew, vnew, cache_kidx, cache_k, cache_v, n_new):
    db, n_pages = page_table.shape
    past = n_pages * PAGE
    total = past + LANES
    topk = min(TOPK_MAX, (past + n_new) // 4)
    cw = 1024 if past % 1024 == 0 else PAGE
    rows_q = n_new * H_A

    n_rows = db * n_new
    nchunk = total // LANES
    srows = _pick_tile(n_rows, 128)

    def per_b(a):
        return pl.BlockSpec((1,) + a.shape[1:], lambda b, pt: (b, 0, 0))

    hbm = pl.BlockSpec(memory_space=pl.ANY)

    keys = pl.pallas_call(
        functools.partial(_sample_scores_kernel, n_pages=n_pages, n_new=n_new, cw=cw),
        grid_spec=pltpu.PrefetchScalarGridSpec(
            num_scalar_prefetch=1,
            grid=(db,),
            in_specs=[per_b(iq32), per_b(w32), per_b(iknew), hbm],
            out_specs=pl.BlockSpec((1, n_new, total), lambda b, pt: (b, 0, 0)),
            scratch_shapes=[pltpu.VMEM((2, D_IDX, past), F32), pltpu.SemaphoreType.DMA((2,))],
        ),
        out_shape=jax.ShapeDtypeStruct((db, n_new, total), I32),
        compiler_params=_cparams(("arbitrary",)),
        name="sample_scores",
    )(page_table, iq32, w32, iknew, cache_kidx)

    keys_cm = keys.reshape(n_rows, nchunk, LANES).transpose(1, 0, 2)
    thr, cut = pl.pallas_call(
        functools.partial(_sample_search_kernel, nchunk=nchunk, rows=srows, topk=topk),
        grid=(n_rows // srows,),
        in_specs=[pl.BlockSpec((nchunk, srows, LANES), lambda i: (0, i, 0))],
        out_specs=[pl.BlockSpec((srows, LANES), lambda i: (i, 0))] * 2,
        out_shape=[jax.ShapeDtypeStruct((n_rows, LANES), I32)] * 2,
        scratch_shapes=[pltpu.VMEM((srows, LANES), I32), pltpu.VMEM((srows, LANES), F32),
                        pltpu.VMEM((srows, LANES), F32)],
        compiler_params=_cparams(("parallel",)),
        name="sample_search",
    )(keys_cm)

    thr3 = thr.reshape(db, n_new, LANES)
    cut3 = cut.reshape(db, n_new, LANES)
    return pl.pallas_call(
        functools.partial(_sample_attend_kernel, n_pages=n_pages, n_new=n_new, cw=cw),
        grid_spec=pltpu.PrefetchScalarGridSpec(
            num_scalar_prefetch=1,
            grid=(db,),
            in_specs=[per_b(q16), per_b(knew), per_b(vnew), per_b(keys), per_b(thr3), per_b(cut3), hbm, hbm],
            out_specs=pl.BlockSpec((1, rows_q, DH_A), lambda b, pt: (b, 0, 0)),
            scratch_shapes=[pltpu.VMEM((2, past, DH_A), F32), pltpu.VMEM((2, past, DH_A), F32),
                            pltpu.SemaphoreType.DMA((2, 2)), pltpu.VMEM((rows_q, total), F32)],
        ),
        out_shape=jax.ShapeDtypeStruct((db, rows_q, DH_A), BF16),
        compiler_params=_cparams(("arbitrary",)),
        name="sample_attend",
    )(page_table, q16, knew, vnew, keys, thr3, cut3, cache_k, cache_v)


def _outffn_kernel(x_ref, hm_ref, ha_ref, mod_ref, g2_ref, wom_ref, woa_ref, wg_ref, wu_ref, wd_ref, y_ref):
    mix = _dot(hm_ref[...], wom_ref[...]) + _dot(ha_ref[...], woa_ref[...])
    x1 = x_ref[...] + mod_ref[0, 0] * mix
    xn = x1 * lax.rsqrt(jnp.mean(x1 * x1, axis=-1, keepdims=True) + EPS) * g2_ref[...]
    hb = (xn * (1.0 + mod_ref[2, 0]) + mod_ref[1, 0]).astype(BF16)
    g = _dot(hb, wg_ref[...])
    u = _dot(hb, wu_ref[...])
    act = (g * _sigmoid(g) * u).astype(BF16)
    y_ref[...] = x1 + mod_ref[3, 0] * _dot(act, wd_ref[...])


def _outffn(x, hm, ha, mod_out, per_token_mod, rows_per_mod, g2, wom, woa, wg, wu, wd, tm):
    n, d = x.shape
    if per_token_mod:
        mod_spec = pl.BlockSpec((4, 1, tm, d), lambda i: (0, 0, i, 0))
    else:
        tiles = rows_per_mod // tm
        mod_spec = pl.BlockSpec((4, 1, 1, d), lambda i: (0, i // tiles, 0, 0))

    def full(a):
        return pl.BlockSpec(a.shape, lambda i: (0,) * a.ndim, pipeline_mode=pl.Buffered(1))

    def rows(w):
        return pl.BlockSpec((tm, w), lambda i: (i, 0))

    return pl.pallas_call(
        _outffn_kernel,
        grid=(n // tm,),
        in_specs=[rows(d), rows(512), rows(512), mod_spec, full(g2), full(wom), full(woa),
                  full(wg), full(wu), full(wd)],
        out_specs=rows(d),
        out_shape=jax.ShapeDtypeStruct((n, d), F32),
        compiler_params=_cparams(("parallel",)),
        name="outffn",
    )(x, hm, ha, mod_out, g2, wom, woa, wg, wu, wd)


def _pick_tile(n, pref):
    t = pref
    while n % t:
        t //= 2
    return t


def kernel(x_prompt, x_sample, cache_k, cache_v, cache_kidx, state_C, state_n, state_m, page_table,
           c_prompt, c_sample, w_ada, b_ada, g_norm1, w_in, b_igate, b_fgate, mlstm_norm_g,
           q_norm_g, k_norm_g, w_out, g_norm2, w_gate, w_up, w_down):
    bp, seq, d = x_prompt.shape
    db, t_new, _ = x_sample.shape
    n_p, n_s = bp * seq, db * t_new

    o = np.cumsum([0, 512, 512, 512, 512, H_M, H_M, 512, DH_A, DH_A, H_IDX * D_IDX, D_IDX, H_IDX])
    mq, mk, mv, mo, mi, mf, aq, ak, av, iq, ik, iw = [w_in[:, int(o[j]):int(o[j + 1])] for j in range(12)]
    zpad = lambda wdt: jnp.zeros((d, wdt), w_in.dtype)
    w_r = jnp.concatenate([mq, mk, mv, mo, aq, ak, av, iq, ik, zpad(LANES - D_IDX),
                           iw, mi, mf, zpad(LANES - 16)], axis=1).astype(BF16)
    wgt = jnp.concatenate([mi, mf], axis=1).T.astype(BF16)
    bsm = jnp.concatenate([jnp.zeros((H_IDX,), F32), b_igate, b_fgate]).reshape(1, 16)
    brow = jnp.concatenate([b_igate, b_fgate]).reshape(8, 1)
    g1 = g_norm1.reshape(1, d)
    g2 = g_norm2.reshape(1, d)
    qg = q_norm_g.reshape(1, DH_A)
    kg = k_norm_g.reshape(1, DH_A)
    ng = mlstm_norm_g.reshape(1, H_M * DH_M)
    wom = w_out[0:H_M * DH_M].astype(BF16)
    woa = w_out[H_M * DH_M:].astype(BF16)
    wg = w_gate.astype(BF16)
    wu = w_up.astype(BF16)
    wd = w_down.astype(BF16)

    mod = _ada(jnp.concatenate([c_prompt, c_sample], axis=0), w_ada.astype(BF16), b_ada)
    mod_p = mod[:bp].reshape(bp, 6, 1, d).transpose(1, 0, 2, 3)
    mod_s = jnp.repeat(mod[bp:].reshape(db, 6, d), t_new, axis=0)
    mod_s = mod_s.transpose(1, 0, 2).reshape(6, 1, n_s, d)

    tm_p = _pick_tile(seq, 512)
    (mqkv, mo_p, aq_p, k_p, v_p, kidx_p, kb, vb, ikb, iq_p, small_p, grow_p) = _inproj(
        x_prompt.reshape(n_p, d), mod_p[0:2], False, seq, g1, w_r, wgt, bsm, brow, qg, kg, tm_p)

    lc = _pick_tile(seq, 256)
    grow3 = grow_p.reshape(8, n_p // lc, lc).transpose(1, 0, 2)
    hm_p, C_p, n_pst, m_pb = _mlstm(
        mqkv, small_p, grow3, mo_p, ng,
        jnp.zeros((bp, H_M, DH_M, DH_M), F32), jnp.zeros((bp, H_M, DH_M), F32),
        jnp.zeros((bp, H_M, LANES), F32), bp, lc)

    tq = _pick_tile(seq, 512)
    ha_p = _dsa_prompt(iq_p, small_p, aq_p, ikb, kb, vb, bp, seq, tq)

    y_p = _outffn(x_prompt.reshape(n_p, d), hm_p, ha_p, mod_p[2:6], False, seq, g2,
                  wom, woa, wg, wu, wd, tm_p)

    tm_s = _pick_tile(n_s, 128)
    (mqkv_s, mo_s, aq_s, k_s, v_s, kidx_s, kb_s, vb_s, ikb_s, iq_s, small_s, grow_s) = _inproj(
        x_sample.reshape(n_s, d), mod_s[0:2], True, 0, g1, w_r, wgt, bsm, brow, qg, kg, tm_s)

    lp = 16
    pad_tok = lambda a: jnp.pad(a.reshape(db, t_new, a.shape[-1]),
                                ((0, 0), (0, lp - t_new), (0, 0))).reshape(db * lp, a.shape[-1])
    gate_pad = jnp.concatenate([jnp.zeros((H_IDX,), F32), jnp.full((H_M,), -jnp.inf, F32),
                                jnp.zeros((H_M,), F32)])
    small_pad = jnp.concatenate(
        [small_s.reshape(db, t_new, 16), jnp.broadcast_to(gate_pad, (db, lp - t_new, 16))], axis=1
    ).reshape(db * lp, 16)
    grow_pad = jnp.concatenate(
        [grow_s.reshape(8, db, t_new),
         jnp.broadcast_to(gate_pad[H_IDX:].reshape(8, 1, 1), (8, db, lp - t_new))], axis=2
    ).transpose(1, 0, 2)
    hm_s_pad, C_s, n_sst, m_sb = _mlstm(
        pad_tok(mqkv_s), small_pad, grow_pad, pad_tok(mo_s), ng,
        state_C, state_n, jnp.broadcast_to(state_m[:, :, None], (db, H_M, LANES)), db, lp)
    hm_s = hm_s_pad.reshape(db, lp, H_M * DH_M)[:, :t_new].reshape(n_s, H_M * DH_M)

    pad_rows = lambda a: jnp.pad(a.reshape(db, t_new, a.shape[-1]), ((0, 0), (0, LANES - t_new), (0, 0)))
    ha_s = _dsa_sample(
        page_table, iq_s.reshape(db, t_new * H_IDX, D_IDX),
        small_s[:, 0:H_IDX].reshape(db, t_new * H_IDX, 1),
        aq_s.reshape(db, t_new * H_A, DH_A),
        pad_rows(ikb_s), pad_rows(kb_s), pad_rows(vb_s),
        jnp.swapaxes(cache_kidx, 1, 2), cache_k, cache_v, t_new).reshape(n_s, H_A * DH_A)

    y_s = _outffn(x_sample.reshape(n_s, d), hm_s, ha_s, mod_s[2:6], True, 0, g2,
                  wom, woa, wg, wu, wd, tm_s)

    return (y_p.reshape(bp, seq, d), y_s.reshape(db, t_new, d),
            k_p.reshape(bp, seq, DH_A), v_p.reshape(bp, seq, DH_A), kidx_p.reshape(bp, seq, D_IDX),
            C_p, n_pst, m_pb[:, :, 0],
            k_s.reshape(db, t_new, DH_A), v_s.reshape(db, t_new, DH_A), kidx_s.reshape(db, t_new, D_IDX),
            C_s, n_sst, m_sb[:, :, 0])
```

```python
import functools

import jax
import jax.numpy as jnp
import numpy as np
from jax import lax
from jax.experimental import pallas as pl
from jax.experimental.pallas import tpu as pltpu

F32 = jnp.float32
BF16 = jnp.bfloat16
I32 = jnp.int32

H_M = 4
DH_M = 128
H_A = 4
DH_A = 128
H_IDX = 8
D_IDX = 64
TOPK_MAX = 256
PAGE = 128
EPS = 1e-6
INDEX_SCALE = D_IDX ** -0.5 * H_IDX ** -0.5
ALIBI_SLOPES = tuple(float(2.0 ** (-8.0 * (h + 1) / H_A)) for h in range(H_A))

LANES = 128
VMEM_LIMIT = 56 * 1024 * 1024
NEG = -1e30
I32_MIN = -2 ** 31
F32_TINY = float(np.finfo(np.float32).tiny)
NEGINF_KEY = -2139095041
INDEX_BITS = 14

C_MQ, C_MK, C_MV, C_MO, C_AQ, C_AK, C_AV, C_IQ, C_IK, C_SM, C_END = (
    0, 512, 1024, 1536, 2048, 2560, 2688, 2816, 3328, 3456, 3584)


def _cparams(sem):
    return pltpu.CompilerParams(dimension_semantics=sem, vmem_limit_bytes=VMEM_LIMIT)


def _sigmoid(x):
    return 1.0 / (1.0 + jnp.exp(-x))


def _log_sigmoid(x):
    return jnp.minimum(x, 0.0) - jnp.log1p(jnp.exp(-jnp.abs(x)))


def _dot(a, b):
    return jnp.dot(a, b, preferred_element_type=F32)


def _dot_nt(a, b):
    return lax.dot_general(a, b, (((1,), (1,)), ((), ())), preferred_element_type=F32)


def _dot_tn(a, b):
    return lax.dot_general(a, b, (((0,), (0,)), ((), ())), preferred_element_type=F32)


def _split3(x):
    hi = x.astype(BF16)
    r1 = x - hi.astype(F32)
    mid = r1.astype(BF16)
    lo = (r1 - mid.astype(F32)).astype(BF16)
    return hi, mid, lo


def _sort_key(x):
    b = pltpu.bitcast(x, I32)
    return b ^ ((b >> 31) & 0x7FFFFFFF)


def _score_key(score, pos):
    return jnp.where(jnp.abs(score) < F32_TINY, -pos, _sort_key(score))


def _ada_kernel(c_ref, w_ref, b_ref, o_ref):
    c = c_ref[...]
    s = c * _sigmoid(c)
    o_ref[...] = _dot(s.astype(BF16), w_ref[...]) + b_ref[...]


def _ada(c, w_bf, b):
    r, d = c.shape
    n = w_bf.shape[1]
    return pl.pallas_call(
        _ada_kernel,
        grid=(n // d,),
        in_specs=[pl.BlockSpec((r, d), lambda j: (0, 0)),
                  pl.BlockSpec((d, d), lambda j: (0, j)),
                  pl.BlockSpec((1, d), lambda j: (0, j))],
        out_specs=pl.BlockSpec((r, d), lambda j: (0, j)),
        out_shape=jax.ShapeDtypeStruct((r, n), F32),
        compiler_params=_cparams(("arbitrary",)),
        name="ada",
    )(c, w_bf, b.reshape(1, n))


def _inproj_kernel(x_ref, mod_ref, g1_ref, w_ref, wgt_ref, bsm_ref, brow_ref, qg_ref, kg_ref,
                   mqkv_ref, mo_ref, aq_ref, k_ref, v_ref, kidx_ref, kb_ref, vb_ref, ikb_ref,
                   iq_ref, small_ref, grow_ref):
    x = x_ref[...]
    xn = x * lax.rsqrt(jnp.mean(x * x, axis=-1, keepdims=True) + EPS) * g1_ref[...]
    h = xn * (1.0 + mod_ref[1, 0]) + mod_ref[0, 0]
    hb = h.astype(BF16)

    def sec(a, b):
        return _dot(hb, w_ref[:, a:b])

    mqkv_ref[:, 0:512] = sec(C_MQ, C_MK).astype(BF16)
    mqkv_ref[:, 512:1024] = (sec(C_MK, C_MV) * (DH_M ** -0.5)).astype(BF16)
    mqkv_ref[:, 1024:1536] = sec(C_MV, C_MO).astype(BF16)
    mo_ref[...] = sec(C_MO, C_AQ)

    aq = sec(C_AQ, C_AK)
    qg = qg_ref[...]
    for hh in range(H_A):
        a = aq[:, hh * DH_A:(hh + 1) * DH_A]
        a = a * lax.rsqrt(jnp.mean(a * a, axis=-1, keepdims=True) + EPS) * qg
        aq_ref[:, hh * DH_A:(hh + 1) * DH_A] = (a * (DH_A ** -0.5)).astype(BF16)

    ak = sec(C_AK, C_AV)
    ak = ak * lax.rsqrt(jnp.mean(ak * ak, axis=-1, keepdims=True) + EPS) * kg_ref[...]
    k_ref[...] = ak
    kb_ref[...] = ak.astype(BF16)
    av = sec(C_AV, C_IQ)
    v_ref[...] = av
    vb_ref[...] = av.astype(BF16)

    iq_ref[...] = sec(C_IQ, C_IK).astype(BF16)
    ik = sec(C_IK, C_SM)[:, 0:D_IDX]
    kidx_ref[...] = ik
    ikb_ref[...] = ik.astype(BF16)

    sm = sec(C_SM, C_END)[:, 0:16] + bsm_ref[...]
    col = lax.broadcasted_iota(I32, sm.shape, 1)
    small_ref[...] = jnp.where(col < H_IDX, sm * INDEX_SCALE,
                               jnp.where(col < H_IDX + H_M, sm, _log_sigmoid(sm)))

    gr = _dot_nt(wgt_ref[...], hb) + brow_ref[...]
    row = lax.broadcasted_iota(I32, gr.shape, 0)
    grow_ref[...] = jnp.where(row < H_M, gr, _log_sigmoid(gr))


def _inproj(x, mod_in, per_token_mod, rows_per_mod, g1, w_r, wgt, bsm, brow, qg, kg, tm):
    n, d = x.shape
    if per_token_mod:
        mod_spec = pl.BlockSpec((2, 1, tm, d), lambda i: (0, 0, i, 0))
    else:
        tiles = rows_per_mod // tm
        mod_spec = pl.BlockSpec((2, 1, 1, d), lambda i: (0, i // tiles, 0, 0))

    def full(a):
        return pl.BlockSpec(a.shape, lambda i: (0,) * a.ndim)

    def rows(w):
        return pl.BlockSpec((tm, w), lambda i: (i, 0))

    outs = [(1536, BF16), (512, F32), (512, BF16), (128, F32), (128, F32), (D_IDX, F32),
            (128, BF16), (128, BF16), (D_IDX, BF16), (512, BF16), (16, F32)]
    out_shape = [jax.ShapeDtypeStruct((n, w), dt) for w, dt in outs]
    out_specs = [rows(w) for w, _ in outs]
    out_shape.append(jax.ShapeDtypeStruct((8, n), F32))
    out_specs.append(pl.BlockSpec((8, tm), lambda i: (0, i)))
    return pl.pallas_call(
        _inproj_kernel,
        grid=(n // tm,),
        in_specs=[rows(d), mod_spec, full(g1), full(w_r), full(wgt), full(bsm), full(brow),
                  full(qg), full(kg)],
        out_specs=out_specs,
        out_shape=out_shape,
        compiler_params=_cparams(("parallel",)),
        name="inproj",
    )(x, mod_in, g1, w_r, wgt, bsm, brow, qg, kg)


def _mlstm_kernel(q_ref, k_ref, v_ref, small_ref, grow_ref, o_ref, ng_ref, c0_ref, n0_ref, m0_ref,
                  h_ref, c_out_ref, n_out_ref, m_out_ref, state_sc, m_sc, *, L, nc):
    c = pl.program_id(1)

    @pl.when(c == 0)
    def _():
        for hh in range(H_M):
            state_sc[hh, 0:DH_M, :] = c0_ref[0, hh]
            state_sc[hh, DH_M:2 * DH_M, :] = jnp.broadcast_to(n0_ref[0, hh:hh + 1, :], (DH_M, DH_M))
        m_sc[...] = jnp.zeros(m_sc.shape, F32)
        m_sc[0:H_M, :] = m0_ref[0]

    ti = lax.broadcasted_iota(I32, (L, L), 0)
    si = lax.broadcasted_iota(I32, (L, L), 1)
    causal = si <= ti
    tri = jnp.where(causal, 1.0, 0.0).astype(BF16)
    tri_t = jnp.where(ti <= si, 1.0, 0.0).astype(BF16)

    sm = small_ref[...]
    col = lax.broadcasted_iota(I32, sm.shape, 1)
    lf_cols = jnp.where(col >= H_IDX + H_M, sm, 0.0)
    hi, mid, lo = _split3(lf_cols)
    b_cols = _dot(tri, hi) + _dot(tri, mid) + _dot(tri, lo)
    gr = grow_ref[0]
    row = lax.broadcasted_iota(I32, gr.shape, 0)
    lf_rows = jnp.where(row >= H_M, gr, 0.0)
    hi, mid, lo = _split3(lf_rows)
    b_rows = _dot(hi, tri_t) + _dot(mid, tri_t) + _dot(lo, tri_t)

    ones_blk = jnp.ones((L, DH_M), BF16)

    def across(x):
        return x[:, 0:L] if L <= LANES else jnp.concatenate([x] * (L // LANES), axis=1)

    for hh in range(H_M):
        q = q_ref[:, hh * DH_M:(hh + 1) * DH_M]
        k = k_ref[:, hh * DH_M:(hh + 1) * DH_M]
        v = v_ref[:, hh * DH_M:(hh + 1) * DH_M]
        ig_c = jnp.broadcast_to(sm[:, H_IDX + hh:H_IDX + hh + 1], (L, LANES))
        b_c = jnp.broadcast_to(b_cols[:, H_IDX + H_M + hh:H_IDX + H_M + hh + 1], (L, LANES))
        ig_r = gr[hh:hh + 1, :]
        b_r = b_rows[H_M + hh:H_M + hh + 1, :]
        m_prev = m_sc[hh:hh + 1, :]

        dmat = jnp.where(causal, across(b_c) - b_r + ig_r, -jnp.inf)
        inter = b_c + m_prev
        m_t = jnp.maximum(inter, jnp.max(dmat, axis=-1, keepdims=True))
        w = jnp.exp(dmat - across(m_t))
        g = jnp.exp(inter - m_t)
        a = w * _dot_nt(q, k)
        st = state_sc[hh]
        qs = _dot_nt(q, st.astype(BF16))
        num = g * qs[:, 0:DH_M] + _dot(a.astype(BF16), v)
        den = g * qs[:, DH_M:2 * DH_M] + jnp.sum(a, axis=-1, keepdims=True)
        hv = num / jnp.maximum(jnp.abs(den), jnp.exp(-m_t))

        m_new = m_t[L - 1:L, :]
        b_last = b_c[L - 1:L, :]
        wend = jnp.exp(b_last - b_c + ig_c - m_new)
        gend = jnp.exp(b_last + m_prev - m_new)
        kw = (k.astype(F32) * wend).astype(BF16)
        v_aug = jnp.concatenate([v, ones_blk], axis=1)
        state_sc[hh] = gend * st + _dot_tn(v_aug, kw)
        m_sc[hh:hh + 1, :] = m_new

        hn = hv * lax.rsqrt(jnp.mean(hv * hv, axis=-1, keepdims=True) + EPS)
        hn = hn * ng_ref[:, hh * DH_M:(hh + 1) * DH_M]
        hn = hn * _sigmoid(o_ref[:, hh * DH_M:(hh + 1) * DH_M])
        h_ref[:, hh * DH_M:(hh + 1) * DH_M] = hn.astype(BF16)

    @pl.when(c == nc - 1)
    def _():
        for hh in range(H_M):
            c_out_ref[0, hh] = state_sc[hh, 0:DH_M, :]
            n_out_ref[0, hh:hh + 1, :] = state_sc[hh, DH_M:DH_M + 1, :]
        m_out_ref[0] = m_sc[0:H_M, :]


def _mlstm(mqkv, small, grow3, mo, ng, c0, n0, m0b, nb, L):
    n = mqkv.shape[0]
    nc = n // nb // L
    d = H_M * DH_M

    def tok(w, blk=0):
        return pl.BlockSpec((L, w), lambda b, c, blk=blk: (b * nc + c, blk))

    per_b3 = pl.BlockSpec((1, H_M, LANES), lambda b, c: (b, 0, 0))
    per_b4 = pl.BlockSpec((1, H_M, DH_M, DH_M), lambda b, c: (b, 0, 0, 0))
    return pl.pallas_call(
        functools.partial(_mlstm_kernel, L=L, nc=nc),
        grid=(nb, nc),
        in_specs=[tok(d, 0), tok(d, 1), tok(d, 2), tok(16),
                  pl.BlockSpec((1, 8, L), lambda b, c: (b * nc + c, 0, 0)),
                  tok(d), pl.BlockSpec((1, d), lambda b, c: (0, 0)),
                  per_b4, per_b3, per_b3],
        out_specs=[tok(d), per_b4, per_b3, per_b3],
        out_shape=[jax.ShapeDtypeStruct((n, d), BF16),
                   jax.ShapeDtypeStruct((nb, H_M, DH_M, DH_M), F32),
                   jax.ShapeDtypeStruct((nb, H_M, DH_M), F32),
                   jax.ShapeDtypeStruct((nb, H_M, LANES), F32)],
        scratch_shapes=[pltpu.VMEM((H_M, 2 * DH_M, DH_M), F32), pltpu.VMEM((8, LANES), F32)],
        compiler_params=_cparams(("parallel", "arbitrary")),
        name="mlstm",
    )(mqkv, mqkv, mqkv, small, grow3, mo, ng, c0, n0, m0b)


def _topk_search(keys_sc, cand_sc, t_sc, tu_sc, aux_sc, cnt_sc, *, nchunk, rows, tk, hb, topk,
                 last_chunk_first_group=0):
    ncol = tk // LANES
    ngrp = rows // hb
    lane_hb = lax.broadcasted_iota(I32, (hb, LANES), 1)
    ones_mat = jnp.ones((LANES, LANES), BF16)

    def count_rows(hit_fn):
        cnt_sc[...] = jnp.zeros(cnt_sc.shape, F32)

        def body(c, _, first_group=0):
            for g in range(first_group, ngrp):
                acc = cnt_sc[g * hb:(g + 1) * hb, :]
                for j in range(ncol):
                    kj = keys_sc[c, g * hb:(g + 1) * hb, j * LANES:(j + 1) * LANES]
                    acc = acc + jnp.where(hit_fn(c, g, j, kj), 1.0, 0.0)
                cnt_sc[g * hb:(g + 1) * hb, :] = acc
            return 0

        lax.fori_loop(0, nchunk - 1, body, 0)
        body(nchunk - 1, 0, last_chunk_first_group)
        return jnp.sum(cnt_sc[...], axis=1, keepdims=True)

    def hit_ge(c, g, j, kj):
        return kj >= cand_sc[g * hb:(g + 1) * hb, :]

    tu_sc[...] = jnp.zeros(tu_sc.shape, I32)

    def search_pass(p, _):
        bit = jnp.left_shift(jnp.int32(1), 31 - p)
        cand_sc[...] = (tu_sc[...] | bit) ^ I32_MIN
        tu_sc[...] = jnp.where(count_rows(hit_ge) >= topk, tu_sc[...] | bit, tu_sc[...])
        return 0

    lax.fori_loop(0, 32, search_pass, 0)
    t_sc[...] = tu_sc[...] ^ I32_MIN

    cand_sc[...] = t_sc[...]
    tied = jnp.logical_and(count_rows(hit_ge) > topk, t_sc[...] > NEGINF_KEY)
    any_tied = jnp.max(jnp.where(tied, 1.0, 0.0)) > 0.0
    tu_sc[...] = jnp.full(tu_sc.shape, 2 ** 31 - 1, I32)

    @pl.when(any_tied)
    def _():
        cand_sc[...] = t_sc[...] + 1
        aux_sc[...] = jnp.broadcast_to(topk - count_rows(hit_ge), aux_sc.shape)

        def hit_tie_below(c, g, j, kj):
            pos = c * tk + j * LANES + lane_hb
            return jnp.where(kj == t_sc[g * hb:(g + 1) * hb, :], pos, 2 ** 30) < cand_sc[g * hb:(g + 1) * hb, :]

        tu_sc[...] = jnp.zeros(tu_sc.shape, I32)

        def cut_pass(p, _):
            bit = jnp.left_shift(jnp.int32(1), INDEX_BITS - 1 - p)
            cand_sc[...] = tu_sc[...] | bit
            tu_sc[...] = jnp.where(count_rows(hit_tie_below) < aux_sc[...], tu_sc[...] | bit, tu_sc[...])
            return 0

        lax.fori_loop(0, INDEX_BITS, cut_pass, 0)

    return any_tied


def _dsa_prompt_kernel(iq_ref, small_ref, aq_ref, qpos_ref, ik_ref, k_ref, v_ref, kpos_ref, o_ref,
                       qidx_sc, wrep_sc, qa_sc, s_sc, keys_sc, lg_sc, p_sc, alpha_sc, m_sc, l_sc, acc_sc,
                       cand_sc, t_sc, tu_sc, aux_sc, cnt_sc, *, tq, tk, topk, rb, hb):
    i = pl.program_id(1)
    q0 = i * tq
    nchunk = (i + 1) * (tq // tk)
    ncol = tk // LANES
    nrb = tq // rb
    lane_hb = lax.broadcasted_iota(I32, (hb, LANES), 1)
    lane_rb = lax.broadcasted_iota(I32, (rb, LANES), 1)
    lane_minus_row = lane_rb - lax.broadcasted_iota(I32, (rb, LANES), 0)

    for h in range(H_IDX):
        qidx_sc[h * tq:(h + 1) * tq, :] = iq_ref[:, h * D_IDX:(h + 1) * D_IDX]
        wrep_sc[h] = jnp.broadcast_to(small_ref[:, h:h + 1], (tq, LANES))
    for hh in range(H_A):
        qa_sc[hh * tq:(hh + 1) * tq, 0:DH_A] = aq_ref[:, hh * DH_A:(hh + 1) * DH_A]
        qa_sc[hh * tq:(hh + 1) * tq, DH_A:2 * DH_A] = qpos_ref[0, hh * tq:(hh + 1) * tq, :]

    row_lo = tq - tk

    def score_body(c, _, lo=0):
        ikc = ik_ref[pl.ds(pl.multiple_of(c * tk, tk), tk), :]
        if lo == 0:
            s_sc[...] = _dot_nt(qidx_sc[...], ikc)
        else:
            for h in range(H_IDX):
                s_sc[h * tq + lo:(h + 1) * tq, :] = _dot_nt(qidx_sc[h * tq + lo:(h + 1) * tq, :], ikc)
            keys_sc[c, 0:lo, :] = jnp.full((lo, tk), NEGINF_KEY, I32)
        bound = q0 - c * tk
        for r in range(lo // rb, nrb):
            accs = [jnp.zeros((rb, LANES), F32) for _ in range(ncol)]
            for h in range(H_IDX):
                wv = wrep_sc[h, r * rb:(r + 1) * rb, :]
                for j in range(ncol):
                    s = s_sc[h * tq + r * rb:h * tq + (r + 1) * rb, j * LANES:(j + 1) * LANES]
                    accs[j] = accs[j] + wv * jnp.maximum(s, 0.0)
            for j in range(ncol):
                acc = jnp.where(lane_minus_row <= bound + (r * rb - j * LANES), accs[j], -jnp.inf)
                keys_sc[c, r * rb:(r + 1) * rb, j * LANES:(j + 1) * LANES] = _score_key(
                    acc, c * tk + j * LANES + lane_rb)
        return 0

    lax.fori_loop(0, nchunk - 1, score_body, 0)
    score_body(nchunk - 1, 0, row_lo)

    any_tied = _topk_search(keys_sc, cand_sc, t_sc, tu_sc, aux_sc, cnt_sc, nchunk=nchunk, rows=tq, tk=tk, hb=hb,
                            topk=topk, last_chunk_first_group=(tq - tk) // hb)

    @pl.when(any_tied)
    def _():
        def drop_body(c, _):
            for g in range(tq // hb):
                tb = t_sc[g * hb:(g + 1) * hb, :]
                xb = tu_sc[g * hb:(g + 1) * hb, :]
                for j in range(ncol):
                    kj = keys_sc[c, g * hb:(g + 1) * hb, j * LANES:(j + 1) * LANES]
                    pos = c * tk + j * LANES + lane_hb
                    late = jnp.where(kj == tb, pos, -1) > xb
                    keys_sc[c, g * hb:(g + 1) * hb, j * LANES:(j + 1) * LANES] = jnp.where(late, kj - 1, kj)
            return 0

        lax.fori_loop(0, nchunk, drop_body, 0)

    t_sc[...] = jnp.maximum(t_sc[...], NEGINF_KEY + 1)
    m_sc[...] = jnp.full(m_sc.shape, NEG, F32)
    l_sc[...] = jnp.zeros(l_sc.shape, F32)
    acc_sc[...] = jnp.zeros(acc_sc.shape, F32)

    def att_body(c, _, lo=0):
        start = pl.multiple_of(c * tk, tk)
        kaug = jnp.concatenate([k_ref[pl.ds(start, tk), :], kpos_ref[pl.ds(start, tk), :]], axis=1)
        head_rows = [slice(hh * tq + lo, (hh + 1) * tq) for hh in range(H_A)]
        if lo == 0:
            lg_sc[...] = _dot_nt(qa_sc[...], kaug)
        else:
            for rows in head_rows:
                lg_sc[rows, :] = _dot_nt(qa_sc[rows, :], kaug)
        for r in range(lo // rb, nrb):
            tb = t_sc[r * rb:(r + 1) * rb, :]
            selb = [jnp.where(keys_sc[c, r * rb:(r + 1) * rb, j * LANES:(j + 1) * LANES] >= tb, 0.0, NEG)
                    for j in range(ncol)]
            for hh in range(H_A):
                rows = slice(hh * tq + r * rb, hh * tq + (r + 1) * rb)
                lgs = [lg_sc[rows, j * LANES:(j + 1) * LANES] + selb[j] for j in range(ncol)]
                mx = lgs[0]
                for j in range(1, ncol):
                    mx = jnp.maximum(mx, lgs[j])
                m_old = m_sc[rows, :]
                m_new = jnp.maximum(m_old, jnp.max(mx, axis=1, keepdims=True))
                alpha = jnp.exp(m_old - m_new)
                ps = [jnp.exp(lg - m_new) for lg in lgs]
                psum = ps[0]
                for j in range(1, ncol):
                    psum = psum + ps[j]
                l_sc[rows, :] = alpha * l_sc[rows, :] + psum
                for j in range(ncol):
                    p_sc[rows, j * LANES:(j + 1) * LANES] = ps[j].astype(BF16)
                alpha_sc[rows, :] = alpha
                m_sc[rows, :] = m_new
        vc = v_ref[pl.ds(start, tk), :]
        if lo == 0:
            acc_sc[...] = alpha_sc[...] * acc_sc[...] + _dot(p_sc[...], vc)
        else:
            for rows in head_rows:
                acc_sc[rows, :] = alpha_sc[rows, :] * acc_sc[rows, :] + _dot(p_sc[rows, :], vc)
        return 0

    lax.fori_loop(0, nchunk - 1, att_body, 0)
    att_body(nchunk - 1, 0, row_lo)
    for hh in range(H_A):
        rows = slice(hh * tq, (hh + 1) * tq)
        l = jnp.sum(l_sc[rows, :], axis=1, keepdims=True)
        o_ref[:, hh * DH_A:(hh + 1) * DH_A] = (acc_sc[rows, :] / l).astype(BF16)


def _alibi_tables(seq, tq):
    pos = np.arange(seq)
    hi, lo = (pos // 64).astype(np.float32), (pos % 64).astype(np.float32)
    ktab = np.zeros((seq, DH_A), np.float32)
    ktab[:, 0], ktab[:, 1], ktab[:, 2], ktab[:, 3] = 64.0 * hi, lo, 1.0, 1.0
    nq = seq // tq
    qtab = np.zeros((nq, H_A, tq, DH_A), np.float32)
    for hh, slope in enumerate(ALIBI_SLOPES):
        qtab[:, hh, :, 0] = slope
        qtab[:, hh, :, 1] = slope
        qtab[:, hh, :, 2] = (-slope * 64.0 * hi).reshape(nq, tq)
        qtab[:, hh, :, 3] = (-slope * lo).reshape(nq, tq)
    return jnp.asarray(qtab.reshape(nq, H_A * tq, DH_A), BF16), jnp.asarray(ktab, BF16)


def _dsa_prompt(iq, small, aq, ikb, kb, vb, nb, seq, tq):
    n = iq.shape[0]
    nq = seq // tq
    topk = min(TOPK_MAX, seq // 4)
    tk = min(256, tq)
    rb = min(64, tq)
    hb = min(128, tq)
    assert seq // LANES <= 256
    qtab, ktab = _alibi_tables(seq, tq)

    def tok(w):
        return pl.BlockSpec((tq, w), lambda b, i: (b * nq + i, 0))

    def per_b(w):
        return pl.BlockSpec((seq, w), lambda b, i: (b, 0))

    return pl.pallas_call(
        functools.partial(_dsa_prompt_kernel, tq=tq, tk=tk, topk=topk, rb=rb, hb=hb),
        grid=(nb, nq),
        in_specs=[tok(512), tok(16), tok(512),
                  pl.BlockSpec((1, H_A * tq, DH_A), lambda b, i: (i, 0, 0)),
                  per_b(D_IDX), per_b(DH_A), per_b(DH_A),
                  pl.BlockSpec((seq, DH_A), lambda b, i: (0, 0))],
        out_specs=tok(512),
        out_shape=jax.ShapeDtypeStruct((n, 512), BF16),
        scratch_shapes=[pltpu.VMEM((H_IDX * tq, D_IDX), BF16),
                        pltpu.VMEM((H_IDX, tq, LANES), F32),
                        pltpu.VMEM((H_A * tq, 2 * DH_A), BF16),
                        pltpu.VMEM((H_IDX * tq, tk), F32),
                        pltpu.VMEM((seq // tk, tq, tk), I32),
                        pltpu.VMEM((H_A * tq, tk), F32),
                        pltpu.VMEM((H_A * tq, tk), BF16),
                        pltpu.VMEM((H_A * tq, LANES), F32),
                        pltpu.VMEM((H_A * tq, LANES), F32),
                        pltpu.VMEM((H_A * tq, LANES), F32),
                        pltpu.VMEM((H_A * tq, DH_A), F32),
                        pltpu.VMEM((tq, LANES), I32),
                        pltpu.VMEM((tq, LANES), I32),
                        pltpu.VMEM((tq, LANES), I32),
                        pltpu.VMEM((tq, LANES), F32),
                        pltpu.VMEM((tq, LANES), F32)],
        compiler_params=_cparams(("parallel", "arbitrary")),
        name="dsa_prompt",
    )(iq, small, aq, qtab, ikb, kb, vb, ktab)


def _page_pipeline(pt_ref, copies_of, n_pages):
    b = pl.program_id(0)
    slot = b % 2

    def for_all(bb, sl, act):
        def body(j, _):
            for cp in copies_of(pt_ref[bb, j], sl, j):
                act(cp)
            return 0
        lax.fori_loop(0, n_pages, body, 0, unroll=8 if n_pages % 8 == 0 else 1)

    @pl.when(b == 0)
    def _():
        for_all(0, 0, lambda cp: cp.start())

    @pl.when(b + 1 < pl.num_programs(0))
    def _():
        for_all(b + 1, 1 - slot, lambda cp: cp.start())

    for_all(b, slot, lambda cp: cp.wait())
    return slot


def _sample_scores_kernel(pt_ref, iq_ref, w_ref, iknew_ref, ckidx_hbm, keys_ref, ibuf, sems,
                          *, n_pages, n_new, cw):
    past = n_pages * PAGE
    total = past + LANES

    def copies_of(pg, sl, j):
        dst = pl.ds(pl.multiple_of(j * PAGE, PAGE), PAGE)
        return (pltpu.make_async_copy(ckidx_hbm.at[pg], ibuf.at[sl, :, dst], sems.at[sl]),)

    slot = _page_pipeline(pt_ref, copies_of, n_pages)

    iq = iq_ref[0]
    w = w_ref[0]

    def scores(dots):
        s = jnp.maximum(dots, 0.0) * w
        return jnp.sum(s.reshape(n_new, H_IDX, s.shape[-1]), axis=1)

    for ch in range(past // cw):
        sc = scores(_dot(iq, ibuf[slot, :, ch * cw:(ch + 1) * cw].astype(BF16)))
        keys_ref[0, :, ch * cw:(ch + 1) * cw] = _score_key(sc, ch * cw + lax.broadcasted_iota(I32, sc.shape, 1))
    sc = scores(_dot_nt(iq, iknew_ref[0]))
    t_i = lax.broadcasted_iota(I32, (n_new, LANES), 0)
    j_i = lax.broadcasted_iota(I32, (n_new, LANES), 1)
    keys_ref[0, :, past:total] = _score_key(jnp.where(j_i <= t_i, sc, -jnp.inf), past + j_i)


def _sample_search_kernel(keys_ref, t_ref, x_ref, cand_sc, aux_sc, cnt_sc, *, nchunk, rows, topk):
    _topk_search(keys_ref, cand_sc, t_ref, x_ref, aux_sc, cnt_sc,
                 nchunk=nchunk, rows=rows, tk=LANES, hb=rows, topk=topk)


def _sample_attend_kernel(pt_ref, q_ref, knew_ref, vnew_ref, keys_ref, t_ref, x_ref, ck_hbm, cv_hbm, o_ref,
                          kbuf, vbuf, sems, lg_sc, *, n_pages, n_new, cw):
    past = n_pages * PAGE
    total = past + LANES
    rows_q = n_new * H_A

    def copies_of(pg, sl, j):
        dst = pl.ds(pl.multiple_of(j * PAGE, PAGE), PAGE)
        return (pltpu.make_async_copy(ck_hbm.at[pg], kbuf.at[sl, dst, :], sems.at[0, sl]),
                pltpu.make_async_copy(cv_hbm.at[pg], vbuf.at[sl, dst, :], sems.at[1, sl]))

    slot = _page_pipeline(pt_ref, copies_of, n_pages)

    r_i = lax.broadcasted_iota(I32, (rows_q, 1), 0)
    r_t = r_i // H_A
    r_h = r_i % H_A
    slope = jnp.zeros((rows_q, 1), F32)
    for hh in range(H_A):
        slope = jnp.where(r_h == hh, ALIBI_SLOPES[hh], slope)
    qposf = (past + r_t).astype(F32)
    q = q_ref[0]

    def sel_bias(lo, width):
        pos = lo + lax.broadcasted_iota(I32, (1, width), 1)
        out = jnp.full((rows_q, width), NEG, F32)
        for t in range(n_new):
            kt = keys_ref[0, t:t + 1, lo:lo + width]
            thr = t_ref[0, t:t + 1, 0:1]
            keep_tie = jnp.logical_and(kt == thr, pos <= x_ref[0, t:t + 1, 0:1])
            chosen = jnp.logical_and(jnp.logical_or(kt > thr, keep_tie), kt > NEGINF_KEY)
            bias_t = jnp.where(chosen, 0.0, NEG)
            out = jnp.where(r_t == t, bias_t, out)
        return out

    def logits(k_chunk, lo, width):
        kposf = (lo + lax.broadcasted_iota(I32, (1, width), 1)).astype(F32)
        sel = sel_bias(lo, width)
        lg = _dot_nt(q, k_chunk) - slope * (qposf - kposf)
        return lg + sel

    for ch in range(past // cw):
        lg_sc[:, ch * cw:(ch + 1) * cw] = logits(kbuf[slot, ch * cw:(ch + 1) * cw, :].astype(BF16), ch * cw, cw)
    lg_sc[:, past:total] = logits(knew_ref[0], past, LANES)

    lg = lg_sc[...]
    m = jnp.max(lg, axis=1, keepdims=True)
    p = jnp.exp(lg - m)
    l = jnp.sum(p, axis=1, keepdims=True)
    pb = p.astype(BF16)
    acc = _dot(pb[:, past:total], vnew_ref[0])
    for ch in range(past // cw):
        acc = acc + _dot(pb[:, ch * cw:(ch + 1) * cw], vbuf[slot, ch * cw:(ch + 1) * cw, :].astype(BF16))
    o_ref[0] = (acc / l).astype(BF16)


def _dsa_sample(page_table, iq32, w32, q16, iknew, knew, vnew, cache_kidx, cache_k, cache_v, n_new):
    db, n_pages = page_table.shape
    past = n_pages * PAGE
    total = past + LANES
    topk = min(TOPK_MAX, (past + n_new) // 4)
    cw = 1024 if past % 1024 == 0 else PAGE
    rows_q = n_new * H_A

    n_rows = db * n_new
    nchunk = total // LANES
    srows = _pick_tile(n_rows, 128)

    def per_b(a):
        return pl.BlockSpec((1,) + a.shape[1:], lambda b, pt: (b, 0, 0))

    hbm = pl.BlockSpec(memory_space=pl.ANY)

    keys = pl.pallas_call(
        functools.partial(_sample_scores_kernel, n_pages=n_pages, n_new=n_new, cw=cw),
        grid_spec=pltpu.PrefetchScalarGridSpec(
            num_scalar_prefetch=1,
            grid=(db,),
            in_specs=[per_b(iq32), per_b(w32), per_b(iknew), hbm],
            out_specs=pl.BlockSpec((1, n_new, total), lambda b, pt: (b, 0, 0)),
            scratch_shapes=[pltpu.VMEM((2, D_IDX, past), F32), pltpu.SemaphoreType.DMA((2,))],
        ),
        out_shape=jax.ShapeDtypeStruct((db, n_new, total), I32),
        compiler_params=_cparams(("arbitrary",)),
        name="sample_scores",
    )(page_table, iq32, w32, iknew, cache_kidx)

    keys_cm = keys.reshape(n_rows, nchunk, LANES).transpose(1, 0, 2)
    thr, cut = pl.pallas_call(
        functools.partial(_sample_search_kernel, nchunk=nchunk, rows=srows, topk=topk),
        grid=(n_rows // srows,),
        in_specs=[pl.BlockSpec((nchunk, srows, LANES), lambda i: (0, i, 0))],
        out_specs=[pl.BlockSpec((srows, LANES), lambda i: (i, 0))] * 2,
        out_shape=[jax.ShapeDtypeStruct((n_rows, LANES), I32)] * 2,
        scratch_shapes=[pltpu.VMEM((srows, LANES), I32), pltpu.VMEM((srows, LANES), F32),
                        pltpu.VMEM((srows, LANES), F32)],
        compiler_params=_cparams(("parallel",)),
        name="sample_search",
    )(keys_cm)

    thr3 = thr.reshape(db, n_new, LANES)
    cut3 = cut.reshape(db, n_new, LANES)
    return pl.pallas_call(
        functools.partial(_sample_attend_kernel, n_pages=n_pages, n_new=n_new, cw=cw),
        grid_spec=pltpu.PrefetchScalarGridSpec(
            num_scalar_prefetch=1,
            grid=(db,),
            in_specs=[per_b(q16), per_b(knew), per_b(vnew), per_b(keys), per_b(thr3), per_b(cut3), hbm, hbm],
            out_specs=pl.BlockSpec((1, rows_q, DH_A), lambda b, pt: (b, 0, 0)),
            scratch_shapes=[pltpu.VMEM((2, past, DH_A), F32), pltpu.VMEM((2, past, DH_A), F32),
                            pltpu.SemaphoreType.DMA((2, 2)), pltpu.VMEM((rows_q, total), F32)],
        ),
        out_shape=jax.ShapeDtypeStruct((db, rows_q, DH_A), BF16),
        compiler_params=_cparams(("arbitrary",)),
        name="sample_attend",
    )(page_table, q16, knew, vnew, keys, thr3, cut3, cache_k, cache_v)


def _outffn_kernel(x_ref, hm_ref, ha_ref, mod_ref, g2_ref, wom_ref, woa_ref, wg_ref, wu_ref, wd_ref, y_ref):
    mix = _dot(hm_ref[...], wom_ref[...]) + _dot(ha_ref[...], woa_ref[...])
    x1 = x_ref[...] + mod_ref[0, 0] * mix
    xn = x1 * lax.rsqrt(jnp.mean(x1 * x1, axis=-1, keepdims=True) + EPS) * g2_ref[...]
    hb = (xn * (1.0 + mod_ref[2, 0]) + mod_ref[1, 0]).astype(BF16)
    g = _dot(hb, wg_ref[...])
    u = _dot(hb, wu_ref[...])
    act = (g * _sigmoid(g) * u).astype(BF16)
    y_ref[...] = x1 + mod_ref[3, 0] * _dot(act, wd_ref[...])


def _outffn(x, hm, ha, mod_out, per_token_mod, rows_per_mod, g2, wom, woa, wg, wu, wd, tm):
    n, d = x.shape
    if per_token_mod:
        mod_spec = pl.BlockSpec((4, 1, tm, d), lambda i: (0, 0, i, 0))
    else:
        tiles = rows_per_mod // tm
        mod_spec = pl.BlockSpec((4, 1, 1, d), lambda i: (0, i // tiles, 0, 0))

    def full(a):
        return pl.BlockSpec(a.shape, lambda i: (0,) * a.ndim, pipeline_mode=pl.Buffered(1))

    def rows(w):
        return pl.BlockSpec((tm, w), lambda i: (i, 0))

    return pl.pallas_call(
        _outffn_kernel,
        grid=(n // tm,),
        in_specs=[rows(d), rows(512), rows(512), mod_spec, full(g2), full(wom), full(woa),
                  full(wg), full(wu), full(wd)],
        out_specs=rows(d),
        out_shape=jax.ShapeDtypeStruct((n, d), F32),
        compiler_params=_cparams(("parallel",)),
        name="outffn",
    )(x, hm, ha, mod_out, g2, wom, woa, wg, wu, wd)


def _pick_tile(n, pref):
    t = pref
    while n % t:
        t //= 2
    return t


def kernel(x_prompt, x_sample, cache_k, cache_v, cache_kidx, state_C, state_n, state_m, page_table,
           c_prompt, c_sample, w_ada, b_ada, g_norm1, w_in, b_igate, b_fgate, mlstm_norm_g,
           q_norm_g, k_norm_g, w_out, g_norm2, w_gate, w_up, w_down):
    bp, seq, d = x_prompt.shape
    db, t_new, _ = x_sample.shape
    n_p, n_s = bp * seq, db * t_new

    o = np.cumsum([0, 512, 512, 512, 512, H_M, H_M, 512, DH_A, DH_A, H_IDX * D_IDX, D_IDX, H_IDX])
    mq, mk, mv, mo, mi, mf, aq, ak, av, iq, ik, iw = [w_in[:, int(o[j]):int(o[j + 1])] for j in range(12)]
    zpad = lambda wdt: jnp.zeros((d, wdt), w_in.dtype)
    w_r = jnp.concatenate([mq, mk, mv, mo, aq, ak, av, iq, ik, zpad(LANES - D_IDX),
                           iw, mi, mf, zpad(LANES - 16)], axis=1).astype(BF16)
    wgt = jnp.concatenate([mi, mf], axis=1).T.astype(BF16)
    bsm = jnp.concatenate([jnp.zeros((H_IDX,), F32), b_igate, b_fgate]).reshape(1, 16)
    brow = jnp.concatenate([b_igate, b_fgate]).reshape(8, 1)
    g1 = g_norm1.reshape(1, d)
    g2 = g_norm2.reshape(1, d)
    qg = q_norm_g.reshape(1, DH_A)
    kg = k_norm_g.reshape(1, DH_A)
    ng = mlstm_norm_g.reshape(1, H_M * DH_M)
    wom = w_out[0:H_M * DH_M].astype(BF16)
    woa = w_out[H_M * DH_M:].astype(BF16)
    wg = w_gate.astype(BF16)
    wu = w_up.astype(BF16)
    wd = w_down.astype(BF16)

    mod = _ada(jnp.concatenate([c_prompt, c_sample], axis=0), w_ada.astype(BF16), b_ada)
    mod_p = mod[:bp].reshape(bp, 6, 1, d).transpose(1, 0, 2, 3)
    mod_s = jnp.repeat(mod[bp:].reshape(db, 6, d), t_new, axis=0)
    mod_s = mod_s.transpose(1, 0, 2).reshape(6, 1, n_s, d)

    tm_p = _pick_tile(seq, 512)
    (mqkv, mo_p, aq_p, k_p, v_p, kidx_p, kb, vb, ikb, iq_p, small_p, grow_p) = _inproj(
        x_prompt.reshape(n_p, d), mod_p[0:2], False, seq, g1, w_r, wgt, bsm, brow, qg, kg, tm_p)

    lc = _pick_tile(seq, 256)
    grow3 = grow_p.reshape(8, n_p // lc, lc).transpose(1, 0, 2)
    hm_p, C_p, n_pst, m_pb = _mlstm(
        mqkv, small_p, grow3, mo_p, ng,
        jnp.zeros((bp, H_M, DH_M, DH_M), F32), jnp.zeros((bp, H_M, DH_M), F32),
        jnp.zeros((bp, H_M, LANES), F32), bp, lc)

    tq = _pick_tile(seq, 512)
    ha_p = _dsa_prompt(iq_p, small_p, aq_p, ikb, kb, vb, bp, seq, tq)

    y_p = _outffn(x_prompt.reshape(n_p, d), hm_p, ha_p, mod_p[2:6], False, seq, g2,
                  wom, woa, wg, wu, wd, tm_p)

    tm_s = _pick_tile(n_s, 128)
    (mqkv_s, mo_s, aq_s, k_s, v_s, kidx_s, kb_s, vb_s, ikb_s, iq_s, small_s, grow_s) = _inproj(
        x_sample.reshape(n_s, d), mod_s[0:2], True, 0, g1, w_r, wgt, bsm, brow, qg, kg, tm_s)

    lp = 16
    pad_tok = lambda a: jnp.pad(a.reshape(db, t_new, a.shape[-1]),
                                ((0, 0), (0, lp - t_new), (0, 0))).reshape(db * lp, a.shape[-1])
    gate_pad = jnp.concatenate([jnp.zeros((H_IDX,), F32), jnp.full((H_M,), -jnp.inf, F32),
                                jnp.zeros((H_M,), F32)])
    small_pad = jnp.concatenate(
        [small_s.reshape(db, t_new, 16), jnp.broadcast_to(gate_pad, (db, lp - t_new, 16))], axis=1
    ).reshape(db * lp, 16)
    grow_pad = jnp.concatenate(
        [grow_s.reshape(8, db, t_new),
         jnp.broadcast_to(gate_pad[H_IDX:].reshape(8, 1, 1), (8, db, lp - t_new))], axis=2
    ).transpose(1, 0, 2)
    hm_s_pad, C_s, n_sst, m_sb = _mlstm(
        pad_tok(mqkv_s), small_pad, grow_pad, pad_tok(mo_s), ng,
        state_C, state_n, jnp.broadcast_to(state_m[:, :, None], (db, H_M, LANES)), db, lp)
    hm_s = hm_s_pad.reshape(db, lp, H_M * DH_M)[:, :t_new].reshape(n_s, H_M * DH_M)

    pad_rows = lambda a: jnp.pad(a.reshape(db, t_new, a.shape[-1]), ((0, 0), (0, LANES - t_new), (0, 0)))
    ha_s = _dsa_sample(
        page_table, iq_s.reshape(db, t_new * H_IDX, D_IDX),
        small_s[:, 0:H_IDX].reshape(db, t_new * H_IDX, 1),
        aq_s.reshape(db, t_new * H_A, DH_A),
        pad_rows(ikb_s), pad_rows(kb_s), pad_rows(vb_s),
        jnp.swapaxes(cache_kidx, 1, 2), cache_k, cache_v, t_new).reshape(n_s, H_A * DH_A)

    y_s = _outffn(x_sample.reshape(n_s, d), hm_s, ha_s, mod_s[2:6], True, 0, g2,
                  wom, woa, wg, wu, wd, tm_s)

    return (y_p.reshape(bp, seq, d), y_s.reshape(db, t_new, d),
            k_p.reshape(bp, seq, DH_A), v_p.reshape(bp, seq, DH_A), kidx_p.reshape(bp, seq, D_IDX),
            C_p, n_pst, m_pb[:, :, 0],
            k_s.reshape(db, t_new, DH_A), v_s.reshape(db, t_new, DH_A), kidx_s.reshape(db, t_new, D_IDX),
            C_s, n_sst, m_sb[:, :, 0])
```

```python
import functools

import jax
import jax.numpy as jnp
import numpy as np
from jax import lax
from jax.experimental import pallas as pl
from jax.experimental.pallas import tpu as pltpu

F32 = jnp.float32
BF16 = jnp.bfloat16
I32 = jnp.int32

H_M = 4
DH_M = 128
H_A = 4
DH_A = 128
H_IDX = 8
D_IDX = 64
TOPK_MAX = 256
PAGE = 128
EPS = 1e-6
INDEX_SCALE = D_IDX ** -0.5 * H_IDX ** -0.5
ALIBI_SLOPES = tuple(float(2.0 ** (-8.0 * (h + 1) / H_A)) for h in range(H_A))

LANES = 128
VMEM_LIMIT = 56 * 1024 * 1024
NEG = -1e30
I32_MIN = -2 ** 31
F32_TINY = float(np.finfo(np.float32).tiny)
NEGINF_KEY = -2139095041
INDEX_BITS = 14

C_MQ, C_MK, C_MV, C_MO, C_AQ, C_AK, C_AV, C_IQ, C_IK, C_SM, C_END = (
    0, 512, 1024, 1536, 2048, 2560, 2688, 2816, 3328, 3456, 3584)


def _cparams(sem):
    return pltpu.CompilerParams(dimension_semantics=sem, vmem_limit_bytes=VMEM_LIMIT)


def _sigmoid(x):
    return 1.0 / (1.0 + jnp.exp(-x))


def _log_sigmoid(x):
    return jnp.minimum(x, 0.0) - jnp.log1p(jnp.exp(-jnp.abs(x)))


def _dot(a, b):
    return jnp.dot(a, b, preferred_element_type=F32)


def _dot_nt(a, b):
    return lax.dot_general(a, b, (((1,), (1,)), ((), ())), preferred_element_type=F32)


def _dot_tn(a, b):
    return lax.dot_general(a, b, (((0,), (0,)), ((), ())), preferred_element_type=F32)


def _split3(x):
    hi = x.astype(BF16)
    r1 = x - hi.astype(F32)
    mid = r1.astype(BF16)
    lo = (r1 - mid.astype(F32)).astype(BF16)
    return hi, mid, lo


def _sort_key(x):
    b = pltpu.bitcast(x, I32)
    return b ^ ((b >> 31) & 0x7FFFFFFF)


def _score_key(score, pos):
    return jnp.where(jnp.abs(score) < F32_TINY, -pos, _sort_key(score))


def _ada_kernel(c_ref, w_ref, b_ref, o_ref):
    c = c_ref[...]
    s = c * _sigmoid(c)
    o_ref[...] = _dot(s.astype(BF16), w_ref[...]) + b_ref[...]


def _ada(c, w_bf, b):
    r, d = c.shape
    n = w_bf.shape[1]
    return pl.pallas_call(
        _ada_kernel,
        grid=(n // d,),
        in_specs=[pl.BlockSpec((r, d), lambda j: (0, 0)),
                  pl.BlockSpec((d, d), lambda j: (0, j)),
                  pl.BlockSpec((1, d), lambda j: (0, j))],
        out_specs=pl.BlockSpec((r, d), lambda j: (0, j)),
        out_shape=jax.ShapeDtypeStruct((r, n), F32),
        compiler_params=_cparams(("arbitrary",)),
        name="ada",
    )(c, w_bf, b.reshape(1, n))


def _inproj_kernel(x_ref, mod_ref, g1_ref, w_ref, wgt_ref, bsm_ref, brow_ref, qg_ref, kg_ref,
                   mqkv_ref, mo_ref, aq_ref, k_ref, v_ref, kidx_ref, kb_ref, vb_ref, ikb_ref,
                   iq_ref, small_ref, grow_ref):
    x = x_ref[...]
    xn = x * lax.rsqrt(jnp.mean(x * x, axis=-1, keepdims=True) + EPS) * g1_ref[...]
    h = xn * (1.0 + mod_ref[1, 0]) + mod_ref[0, 0]
    hb = h.astype(BF16)

    def sec(a, b):
        return _dot(hb, w_ref[:, a:b])

    mqkv_ref[:, 0:512] = sec(C_MQ, C_MK).astype(BF16)
    mqkv_ref[:, 512:1024] = (sec(C_MK, C_MV) * (DH_M ** -0.5)).astype(BF16)
    mqkv_ref[:, 1024:1536] = sec(C_MV, C_MO).astype(BF16)
    mo_ref[...] = sec(C_MO, C_AQ)

    aq = sec(C_AQ, C_AK)
    qg = qg_ref[...]
    for hh in range(H_A):
        a = aq[:, hh * DH_A:(hh + 1) * DH_A]
        a = a * lax.rsqrt(jnp.mean(a * a, axis=-1, keepdims=True) + EPS) * qg
        aq_ref[:, hh * DH_A:(hh + 1) * DH_A] = (a * (DH_A ** -0.5)).astype(BF16)

    ak = sec(C_AK, C_AV)
    ak = ak * lax.rsqrt(jnp.mean(ak * ak, axis=-1, keepdims=True) + EPS) * kg_ref[...]
    k_ref[...] = ak
    kb_ref[...] = ak.astype(BF16)
    av = sec(C_AV, C_IQ)
    v_ref[...] = av
    vb_ref[...] = av.astype(BF16)

    iq_ref[...] = sec(C_IQ, C_IK).astype(BF16)
    ik = sec(C_IK, C_SM)[:, 0:D_IDX]
    kidx_ref[...] = ik
    ikb_ref[...] = ik.astype(BF16)

    sm = sec(C_SM, C_END)[:, 0:16] + bsm_ref[...]
    col = lax.broadcasted_iota(I32, sm.shape, 1)
    small_ref[...] = jnp.where(col < H_IDX, sm * INDEX_SCALE,
                               jnp.where(col < H_IDX + H_M, sm, _log_sigmoid(sm)))

    gr = _dot_nt(wgt_ref[...], hb) + brow_ref[...]
    row = lax.broadcasted_iota(I32, gr.shape, 0)
    grow_ref[...] = jnp.where(row < H_M, gr, _log_sigmoid(gr))


def _inproj(x, mod_in, per_token_mod, rows_per_mod, g1, w_r, wgt, bsm, brow, qg, kg, tm):
    n, d = x.shape
    if per_token_mod:
        mod_spec = pl.BlockSpec((2, 1, tm, d), lambda i: (0, 0, i, 0))
    else:
        tiles = rows_per_mod // tm
        mod_spec = pl.BlockSpec((2, 1, 1, d), lambda i: (0, i // tiles, 0, 0))

    def full(a):
        return pl.BlockSpec(a.shape, lambda i: (0,) * a.ndim)

    def rows(w):
        return pl.BlockSpec((tm, w), lambda i: (i, 0))

    outs = [(1536, BF16), (512, F32), (512, BF16), (128, F32), (128, F32), (D_IDX, F32),
            (128, BF16), (128, BF16), (D_IDX, BF16), (512, BF16), (16, F32)]
    out_shape = [jax.ShapeDtypeStruct((n, w), dt) for w, dt in outs]
    out_specs = [rows(w) for w, _ in outs]
    out_shape.append(jax.ShapeDtypeStruct((8, n), F32))
    out_specs.append(pl.BlockSpec((8, tm), lambda i: (0, i)))
    return pl.pallas_call(
        _inproj_kernel,
        grid=(n // tm,),
        in_specs=[rows(d), mod_spec, full(g1), full(w_r), full(wgt), full(bsm), full(brow),
                  full(qg), full(kg)],
        out_specs=out_specs,
        out_shape=out_shape,
        compiler_params=_cparams(("parallel",)),
        name="inproj",
    )(x, mod_in, g1, w_r, wgt, bsm, brow, qg, kg)


def _mlstm_kernel(q_ref, k_ref, v_ref, small_ref, grow_ref, o_ref, ng_ref, c0_ref, n0_ref, m0_ref,
                  h_ref, c_out_ref, n_out_ref, m_out_ref, state_sc, m_sc, *, L, nc):
    c = pl.program_id(1)

    @pl.when(c == 0)
    def _():
        for hh in range(H_M):
            state_sc[hh, 0:DH_M, :] = c0_ref[0, hh]
            state_sc[hh, DH_M:2 * DH_M, :] = jnp.broadcast_to(n0_ref[0, hh:hh + 1, :], (DH_M, DH_M))
        m_sc[...] = jnp.zeros(m_sc.shape, F32)
        m_sc[0:H_M, :] = m0_ref[0]

    ti = lax.broadcasted_iota(I32, (L, L), 0)
    si = lax.broadcasted_iota(I32, (L, L), 1)
    causal = si <= ti
    tri = jnp.where(causal, 1.0, 0.0).astype(BF16)
    tri_t = jnp.where(ti <= si, 1.0, 0.0).astype(BF16)

    sm = small_ref[...]
    col = lax.broadcasted_iota(I32, sm.shape, 1)
    lf_cols = jnp.where(col >= H_IDX + H_M, sm, 0.0)
    hi, mid, lo = _split3(lf_cols)
    b_cols = _dot(tri, hi) + _dot(tri, mid) + _dot(tri, lo)
    gr = grow_ref[0]
    row = lax.broadcasted_iota(I32, gr.shape, 0)
    lf_rows = jnp.where(row >= H_M, gr, 0.0)
    hi, mid, lo = _split3(lf_rows)
    b_rows = _dot(hi, tri_t) + _dot(mid, tri_t) + _dot(lo, tri_t)

    ones_blk = jnp.ones((L, DH_M), BF16)

    def across(x):
        return x[:, 0:L] if L <= LANES else jnp.concatenate([x] * (L // LANES), axis=1)

    for hh in range(H_M):
        q = q_ref[:, hh * DH_M:(hh + 1) * DH_M]
        k = k_ref[:, hh * DH_M:(hh + 1) * DH_M]
        v = v_ref[:, hh * DH_M:(hh + 1) * DH_M]
        ig_c = jnp.broadcast_to(sm[:, H_IDX + hh:H_IDX + hh + 1], (L, LANES))
        b_c = jnp.broadcast_to(b_cols[:, H_IDX + H_M + hh:H_IDX + H_M + hh + 1], (L, LANES))
        ig_r = gr[hh:hh + 1, :]
        b_r = b_rows[H_M + hh:H_M + hh + 1, :]
        m_prev = m_sc[hh:hh + 1, :]

        dmat = jnp.where(causal, across(b_c) - b_r + ig_r, -jnp.inf)
        inter = b_c + m_prev
        m_t = jnp.maximum(inter, jnp.max(dmat, axis=-1, keepdims=True))
        w = jnp.exp(dmat - across(m_t))
        g = jnp.exp(inter - m_t)
        a = w * _dot_nt(q, k)
        st = state_sc[hh]
        qs = _dot_nt(q, st.astype(BF16))
        num = g * qs[:, 0:DH_M] + _dot(a.astype(BF16), v)
        den = g * qs[:, DH_M:2 * DH_M] + jnp.sum(a, axis=-1, keepdims=True)
        hv = num / jnp.maximum(jnp.abs(den), jnp.exp(-m_t))

        m_new = m_t[L - 1:L, :]
        b_last = b_c[L - 1:L, :]
        wend = jnp.exp(b_last - b_c + ig_c - m_new)
        gend = jnp.exp(b_last + m_prev - m_new)
        kw = (k.astype(F32) * wend).astype(BF16)
        v_aug = jnp.concatenate([v, ones_blk], axis=1)
        state_sc[hh] = gend * st + _dot_tn(v_aug, kw)
        m_sc[hh:hh + 1, :] = m_new

        hn = hv * lax.rsqrt(jnp.mean(hv * hv, axis=-1, keepdims=True) + EPS)
        hn = hn * ng_ref[:, hh * DH_M:(hh + 1) * DH_M]
        hn = hn * _sigmoid(o_ref[:, hh * DH_M:(hh + 1) * DH_M])
        h_ref[:, hh * DH_M:(hh + 1) * DH_M] = hn.astype(BF16)

    @pl.when(c == nc - 1)
    def _():
        for hh in range(H_M):
            c_out_ref[0, hh] = state_sc[hh, 0:DH_M, :]
            n_out_ref[0, hh:hh + 1, :] = state_sc[hh, DH_M:DH_M + 1, :]
        m_out_ref[0] = m_sc[0:H_M, :]


def _mlstm(mqkv, small, grow3, mo, ng, c0, n0, m0b, nb, L):
    n = mqkv.shape[0]
    nc = n // nb // L
    d = H_M * DH_M

    def tok(w, blk=0):
        return pl.BlockSpec((L, w), lambda b, c, blk=blk: (b * nc + c, blk))

    per_b3 = pl.BlockSpec((1, H_M, LANES), lambda b, c: (b, 0, 0))
    per_b4 = pl.BlockSpec((1, H_M, DH_M, DH_M), lambda b, c: (b, 0, 0, 0))
    return pl.pallas_call(
        functools.partial(_mlstm_kernel, L=L, nc=nc),
        grid=(nb, nc),
        in_specs=[tok(d, 0), tok(d, 1), tok(d, 2), tok(16),
                  pl.BlockSpec((1, 8, L), lambda b, c: (b * nc + c, 0, 0)),
                  tok(d), pl.BlockSpec((1, d), lambda b, c: (0, 0)),
                  per_b4, per_b3, per_b3],
        out_specs=[tok(d), per_b4, per_b3, per_b3],
        out_shape=[jax.ShapeDtypeStruct((n, d), BF16),
                   jax.ShapeDtypeStruct((nb, H_M, DH_M, DH_M), F32),
                   jax.ShapeDtypeStruct((nb, H_M, DH_M), F32),
                   jax.ShapeDtypeStruct((nb, H_M, LANES), F32)],
        scratch_shapes=[pltpu.VMEM((H_M, 2 * DH_M, DH_M), F32), pltpu.VMEM((8, LANES), F32)],
        compiler_params=_cparams(("parallel", "arbitrary")),
        name="mlstm",
    )(mqkv, mqkv, mqkv, small, grow3, mo, ng, c0, n0, m0b)


def _transpose32(words):
    a = list(words)
    j, m = 16, 0x0000FFFF
    while j:
        k = 0
        while k < 32:
            t = (a[k] ^ lax.shift_right_logical(a[k + j], jnp.int32(j))) & m
            a[k] = a[k] ^ t
            a[k + j] = a[k + j] ^ (t << j)
            k = (k + j + 1) & ~j
        j >>= 1
        m = (m ^ (m << j)) & 0xFFFFFFFF
    return a


def _topk_search(keys_sc, cand_sc, t_sc, tu_sc, aux_sc, cnt_sc, *, nchunk, rows, tk, hb, topk,
                 last_chunk_first_group=0, planes=None, nchunk_max=None):
    ncol = tk // LANES
    ngrp = rows // hb
    lane_hb = lax.broadcasted_iota(I32, (hb, LANES), 1)

    def count_rows(hit_fn):
        cnt_sc[...] = jnp.zeros(cnt_sc.shape, F32)

        def body(c, _, first_group=0):
            for g in range(first_group, ngrp):
                acc = cnt_sc[g * hb:(g + 1) * hb, :]
                for j in range(ncol):
                    kj = keys_sc[c, g * hb:(g + 1) * hb, j * LANES:(j + 1) * LANES]
                    acc = acc + jnp.where(hit_fn(c, g, j, kj), 1.0, 0.0)
                cnt_sc[g * hb:(g + 1) * hb, :] = acc
            return 0

        lax.fori_loop(0, nchunk - 1, body, 0)
        body(nchunk - 1, 0, last_chunk_first_group)
        return jnp.sum(cnt_sc[...], axis=1, keepdims=True)

    def hit_ge(c, g, j, kj):
        return kj >= cand_sc[g * hb:(g + 1) * hb, :]

    tu_sc[...] = jnp.zeros(tu_sc.shape, I32)
    if planes is None:
        def search_pass(p, _):
            bit = jnp.left_shift(jnp.int32(1), 31 - p)
            cand_sc[...] = (tu_sc[...] | bit) ^ I32_MIN
            tu_sc[...] = jnp.where(count_rows(hit_ge) >= topk, tu_sc[...] | bit, tu_sc[...])
            return 0

        lax.fori_loop(0, 32, search_pass, 0)
        t_sc[...] = tu_sc[...] ^ I32_MIN
        cand_sc[...] = t_sc[...]
        n_ge = count_rows(hit_ge)
    else:
        planes_sc, alive_sc, above_sc = planes
        nblk = nchunk_max * ncol
        nvalid = nchunk * ncol

        def tile_body(rt, _):
            r0 = pl.multiple_of(rt * 8, 8)
            words = []
            for j in range(32):
                if j < nblk:
                    kj = keys_sc[j // ncol, pl.ds(r0, 8), (j % ncol) * LANES:(j % ncol + 1) * LANES]
                    kj = jnp.where(j < nvalid, kj, NEGINF_KEY)
                else:
                    kj = jnp.full((8, LANES), NEGINF_KEY, I32)
                words.append(kj ^ I32_MIN)
            for q, word in enumerate(_transpose32(words)):
                planes_sc[q, pl.ds(r0, 8), :] = word
            return 0

        lax.fori_loop(0, rows // 8, tile_body, 0)
        alive_sc[...] = jnp.full(alive_sc.shape, -1, I32)
        above_sc[...] = jnp.zeros(above_sc.shape, F32)

        def select_pass(q, _):
            alive = alive_sc[...]
            ones = alive & planes_sc[q]
            n_ones = jnp.sum(lax.population_count(ones).astype(F32), axis=1, keepdims=True)
            above = above_sc[...]
            take = above + n_ones >= topk
            alive_sc[...] = jnp.where(take, ones, alive ^ ones)
            above_sc[...] = jnp.where(take, above, above + n_ones)
            tu_sc[...] = jnp.where(take, tu_sc[...] | jnp.left_shift(jnp.int32(1), 31 - q), tu_sc[...])
            return 0

        lax.fori_loop(0, 32, select_pass, 0)
        t_sc[...] = tu_sc[...] ^ I32_MIN
        n_ge = above_sc[...] + jnp.sum(lax.population_count(alive_sc[...]).astype(F32), axis=1, keepdims=True)

    tied = jnp.logical_and(n_ge > topk, t_sc[...] > NEGINF_KEY)
    any_tied = jnp.max(jnp.where(tied, 1.0, 0.0)) > 0.0
    tu_sc[...] = jnp.full(tu_sc.shape, 2 ** 31 - 1, I32)

    @pl.when(any_tied)
    def _():
        cand_sc[...] = t_sc[...] + 1
        aux_sc[...] = jnp.broadcast_to(topk - count_rows(hit_ge), aux_sc.shape)

        def hit_tie_below(c, g, j, kj):
            pos = c * tk + j * LANES + lane_hb
            return jnp.where(kj == t_sc[g * hb:(g + 1) * hb, :], pos, 2 ** 30) < cand_sc[g * hb:(g + 1) * hb, :]

        tu_sc[...] = jnp.zeros(tu_sc.shape, I32)

        def cut_pass(p, _):
            bit = jnp.left_shift(jnp.int32(1), INDEX_BITS - 1 - p)
            cand_sc[...] = tu_sc[...] | bit
            tu_sc[...] = jnp.where(count_rows(hit_tie_below) < aux_sc[...], tu_sc[...] | bit, tu_sc[...])
            return 0

        lax.fori_loop(0, INDEX_BITS, cut_pass, 0)

    return any_tied


def _dsa_prompt_kernel(iq_ref, small_ref, aq_ref, qpos_ref, ik_ref, k_ref, v_ref, kpos_ref, o_ref,
                       qidx_sc, wrep_sc, qa_sc, s_sc, keys_sc, lg_sc, p_sc, alpha_sc, m_sc, l_sc, acc_sc,
                       cand_sc, t_sc, tu_sc, aux_sc, cnt_sc, planes_sc, alive_sc, above_sc,
                       *, tq, tk, topk, rb, hb, use_planes):
    i = pl.program_id(1)
    q0 = i * tq
    nchunk = (i + 1) * (tq // tk)
    ncol = tk // LANES
    nrb = tq // rb
    lane_hb = lax.broadcasted_iota(I32, (hb, LANES), 1)
    lane_rb = lax.broadcasted_iota(I32, (rb, LANES), 1)
    lane_minus_row = lane_rb - lax.broadcasted_iota(I32, (rb, LANES), 0)

    for h in range(H_IDX):
        qidx_sc[h * tq:(h + 1) * tq, :] = iq_ref[:, h * D_IDX:(h + 1) * D_IDX]
        wrep_sc[h] = jnp.broadcast_to(small_ref[:, h:h + 1], (tq, LANES))
    for hh in range(H_A):
        qa_sc[hh * tq:(hh + 1) * tq, 0:DH_A] = aq_ref[:, hh * DH_A:(hh + 1) * DH_A]
        qa_sc[hh * tq:(hh + 1) * tq, DH_A:2 * DH_A] = qpos_ref[0, hh * tq:(hh + 1) * tq, :]

    row_lo = tq - tk

    def score_body(c, _, lo=0):
        ikc = ik_ref[pl.ds(pl.multiple_of(c * tk, tk), tk), :]
        if lo == 0:
            s_sc[...] = _dot_nt(qidx_sc[...], ikc)
        else:
            for h in range(H_IDX):
                s_sc[h * tq + lo:(h + 1) * tq, :] = _dot_nt(qidx_sc[h * tq + lo:(h + 1) * tq, :], ikc)
            keys_sc[c, 0:lo, :] = jnp.full((lo, tk), NEGINF_KEY, I32)
        bound = q0 - c * tk
        for r in range(lo // rb, nrb):
            accs = [jnp.zeros((rb, LANES), F32) for _ in range(ncol)]
            for h in range(H_IDX):
                wv = wrep_sc[h, r * rb:(r + 1) * rb, :]
                for j in range(ncol):
                    s = s_sc[h * tq + r * rb:h * tq + (r + 1) * rb, j * LANES:(j + 1) * LANES]
                    accs[j] = accs[j] + wv * jnp.maximum(s, 0.0)
            for j in range(ncol):
                acc = jnp.where(lane_minus_row <= bound + (r * rb - j * LANES), accs[j], -jnp.inf)
                keys_sc[c, r * rb:(r + 1) * rb, j * LANES:(j + 1) * LANES] = _score_key(
                    acc, c * tk + j * LANES + lane_rb)
        return 0

    lax.fori_loop(0, nchunk - 1, score_body, 0)
    score_body(nchunk - 1, 0, row_lo)

    any_tied = _topk_search(keys_sc, cand_sc, t_sc, tu_sc, aux_sc, cnt_sc, nchunk=nchunk, rows=tq, tk=tk, hb=hb,
                            topk=topk, last_chunk_first_group=(tq - tk) // hb,
                            planes=(planes_sc, alive_sc, above_sc) if use_planes else None,
                            nchunk_max=keys_sc.shape[0])

    @pl.when(any_tied)
    def _():
        def drop_body(c, _):
            for g in range(tq // hb):
                tb = t_sc[g * hb:(g + 1) * hb, :]
                xb = tu_sc[g * hb:(g + 1) * hb, :]
                for j in range(ncol):
                    kj = keys_sc[c, g * hb:(g + 1) * hb, j * LANES:(j + 1) * LANES]
                    pos = c * tk + j * LANES + lane_hb
                    late = jnp.where(kj == tb, pos, -1) > xb
                    keys_sc[c, g * hb:(g + 1) * hb, j * LANES:(j + 1) * LANES] = jnp.where(late, kj - 1, kj)
            return 0

        lax.fori_loop(0, nchunk, drop_body, 0)

    t_sc[...] = jnp.maximum(t_sc[...], NEGINF_KEY + 1)
    m_sc[...] = jnp.full(m_sc.shape, NEG, F32)
    l_sc[...] = jnp.zeros(l_sc.shape, F32)
    acc_sc[...] = jnp.zeros(acc_sc.shape, F32)

    def att_body(c, _, lo=0):
        start = pl.multiple_of(c * tk, tk)
        kaug = jnp.concatenate([k_ref[pl.ds(start, tk), :], kpos_ref[pl.ds(start, tk), :]], axis=1)
        head_rows = [slice(hh * tq + lo, (hh + 1) * tq) for hh in range(H_A)]
        if lo == 0:
            lg_sc[...] = _dot_nt(qa_sc[...], kaug)
        else:
            for rows in head_rows:
                lg_sc[rows, :] = _dot_nt(qa_sc[rows, :], kaug)
        for r in range(lo // rb, nrb):
            tb = t_sc[r * rb:(r + 1) * rb, :]
            selb = [jnp.where(keys_sc[c, r * rb:(r + 1) * rb, j * LANES:(j + 1) * LANES] >= tb, 0.0, NEG)
                    for j in range(ncol)]
            for hh in range(H_A):
                rows = slice(hh * tq + r * rb, hh * tq + (r + 1) * rb)
                lgs = [lg_sc[rows, j * LANES:(j + 1) * LANES] + selb[j] for j in range(ncol)]
                mx = lgs[0]
                for j in range(1, ncol):
                    mx = jnp.maximum(mx, lgs[j])
                m_old = m_sc[rows, :]
                m_new = jnp.maximum(m_old, jnp.max(mx, axis=1, keepdims=True))
                alpha = jnp.exp(m_old - m_new)
                ps = [jnp.exp(lg - m_new) for lg in lgs]
                psum = ps[0]
                for j in range(1, ncol):
                    psum = psum + ps[j]
                l_sc[rows, :] = alpha * l_sc[rows, :] + psum
                for j in range(ncol):
                    p_sc[rows, j * LANES:(j + 1) * LANES] = ps[j].astype(BF16)
                alpha_sc[rows, :] = alpha
                m_sc[rows, :] = m_new
        vc = v_ref[pl.ds(start, tk), :]
        if lo == 0:
            acc_sc[...] = alpha_sc[...] * acc_sc[...] + _dot(p_sc[...], vc)
        else:
            for rows in head_rows:
                acc_sc[rows, :] = alpha_sc[rows, :] * acc_sc[rows, :] + _dot(p_sc[rows, :], vc)
        return 0

    lax.fori_loop(0, nchunk - 1, att_body, 0)
    att_body(nchunk - 1, 0, row_lo)
    for hh in range(H_A):
        rows = slice(hh * tq, (hh + 1) * tq)
        l = jnp.sum(l_sc[rows, :], axis=1, keepdims=True)
        o_ref[:, hh * DH_A:(hh + 1) * DH_A] = (acc_sc[rows, :] / l).astype(BF16)


def _alibi_tables(seq, tq):
    pos = np.arange(seq)
    hi, lo = (pos // 64).astype(np.float32), (pos % 64).astype(np.float32)
    ktab = np.zeros((seq, DH_A), np.float32)
    ktab[:, 0], ktab[:, 1], ktab[:, 2], ktab[:, 3] = 64.0 * hi, lo, 1.0, 1.0
    nq = seq // tq
    qtab = np.zeros((nq, H_A, tq, DH_A), np.float32)
    for hh, slope in enumerate(ALIBI_SLOPES):
        qtab[:, hh, :, 0] = slope
        qtab[:, hh, :, 1] = slope
        qtab[:, hh, :, 2] = (-slope * 64.0 * hi).reshape(nq, tq)
        qtab[:, hh, :, 3] = (-slope * lo).reshape(nq, tq)
    return jnp.asarray(qtab.reshape(nq, H_A * tq, DH_A), BF16), jnp.asarray(ktab, BF16)


def _dsa_prompt(iq, small, aq, ikb, kb, vb, nb, seq, tq):
    n = iq.shape[0]
    nq = seq // tq
    topk = min(TOPK_MAX, seq // 4)
    tk = min(256, tq)
    rb = min(64, tq)
    hb = min(128, tq)
    use_planes = seq // LANES <= 32
    qtab, ktab = _alibi_tables(seq, tq)

    def tok(w):
        return pl.BlockSpec((tq, w), lambda b, i: (b * nq + i, 0))

    def per_b(w):
        return pl.BlockSpec((seq, w), lambda b, i: (b, 0))

    return pl.pallas_call(
        functools.partial(_dsa_prompt_kernel, tq=tq, tk=tk, topk=topk, rb=rb, hb=hb, use_planes=use_planes),
        grid=(nb, nq),
        in_specs=[tok(512), tok(16), tok(512),
                  pl.BlockSpec((1, H_A * tq, DH_A), lambda b, i: (i, 0, 0)),
                  per_b(D_IDX), per_b(DH_A), per_b(DH_A),
                  pl.BlockSpec((seq, DH_A), lambda b, i: (0, 0))],
        out_specs=tok(512),
        out_shape=jax.ShapeDtypeStruct((n, 512), BF16),
        scratch_shapes=[pltpu.VMEM((H_IDX * tq, D_IDX), BF16),
                        pltpu.VMEM((H_IDX, tq, LANES), F32),
                        pltpu.VMEM((H_A * tq, 2 * DH_A), BF16),
                        pltpu.VMEM((H_IDX * tq, tk), F32),
                        pltpu.VMEM((seq // tk, tq, tk), I32),
                        pltpu.VMEM((H_A * tq, tk), F32),
                        pltpu.VMEM((H_A * tq, tk), BF16),
                        pltpu.VMEM((H_A * tq, LANES), F32),
                        pltpu.VMEM((H_A * tq, LANES), F32),
                        pltpu.VMEM((H_A * tq, LANES), F32),
                        pltpu.VMEM((H_A * tq, DH_A), F32),
                        pltpu.VMEM((tq, LANES), I32),
                        pltpu.VMEM((tq, LANES), I32),
                        pltpu.VMEM((tq, LANES), I32),
                        pltpu.VMEM((tq, LANES), F32),
                        pltpu.VMEM((tq, LANES), F32),
                        pltpu.VMEM((32 if use_planes else 1, tq, LANES), I32),
                        pltpu.VMEM((tq, LANES), I32),
                        pltpu.VMEM((tq, LANES), F32)],
        compiler_params=_cparams(("parallel", "arbitrary")),
        name="dsa_prompt",
    )(iq, small, aq, qtab, ikb, kb, vb, ktab)


def _page_pipeline(pt_ref, copies_of, n_pages):
    b = pl.program_id(0)
    slot = b % 2

    def for_all(bb, sl, act):
        def body(j, _):
            for cp in copies_of(pt_ref[bb, j], sl, j):
                act(cp)
            return 0
        lax.fori_loop(0, n_pages, body, 0, unroll=8 if n_pages % 8 == 0 else 1)

    @pl.when(b == 0)
    def _():
        for_all(0, 0, lambda cp: cp.start())

    @pl.when(b + 1 < pl.num_programs(0))
    def _():
        for_all(b + 1, 1 - slot, lambda cp: cp.start())

    for_all(b, slot, lambda cp: cp.wait())
    return slot


def _sample_scores_kernel(pt_ref, iq_ref, w_ref, iknew_ref, ckidx_hbm, keys_ref, ibuf, sems,
                          *, n_pages, n_new, cw):
    past = n_pages * PAGE
    total = past + LANES

    def copies_of(pg, sl, j):
        dst = pl.ds(pl.multiple_of(j * PAGE, PAGE), PAGE)
        return (pltpu.make_async_copy(ckidx_hbm.at[pg], ibuf.at[sl, :, dst], sems.at[sl]),)

    slot = _page_pipeline(pt_ref, copies_of, n_pages)

    iq = iq_ref[0]
    w = w_ref[0]

    def scores(dots):
        s = jnp.maximum(dots, 0.0) * w
        return jnp.sum(s.reshape(n_new, H_IDX, s.shape[-1]), axis=1)

    for ch in range(past // cw):
        sc = scores(_dot(iq, ibuf[slot, :, ch * cw:(ch + 1) * cw].astype(BF16)))
        keys_ref[0, :, ch * cw:(ch + 1) * cw] = _score_key(sc, ch * cw + lax.broadcasted_iota(I32, sc.shape, 1))
    sc = scores(_dot_nt(iq, iknew_ref[0]))
    t_i = lax.broadcasted_iota(I32, (n_new, LANES), 0)
    j_i = lax.broadcasted_iota(I32, (n_new, LANES), 1)
    keys_ref[0, :, past:total] = _score_key(jnp.where(j_i <= t_i, sc, -jnp.inf), past + j_i)


def _sample_search_kernel(keys_ref, t_ref, x_ref, cand_sc, aux_sc, cnt_sc, *, nchunk, rows, topk):
    _topk_search(keys_ref, cand_sc, t_ref, x_ref, aux_sc, cnt_sc,
                 nchunk=nchunk, rows=rows, tk=LANES, hb=rows, topk=topk)


def _sample_attend_kernel(pt_ref, q_ref, knew_ref, vnew_ref, keys_ref, t_ref, x_ref, ck_hbm, cv_hbm, o_ref,
                          kbuf, vbuf, sems, lg_sc, *, n_pages, n_new, cw):
    past = n_pages * PAGE
    total = past + LANES
    rows_q = n_new * H_A

    def copies_of(pg, sl, j):
        dst = pl.ds(pl.multiple_of(j * PAGE, PAGE), PAGE)
        return (pltpu.make_async_copy(ck_hbm.at[pg], kbuf.at[sl, dst, :], sems.at[0, sl]),
                pltpu.make_async_copy(cv_hbm.at[pg], vbuf.at[sl, dst, :], sems.at[1, sl]))

    slot = _page_pipeline(pt_ref, copies_of, n_pages)

    r_i = lax.broadcasted_iota(I32, (rows_q, 1), 0)
    r_t = r_i // H_A
    r_h = r_i % H_A
    slope = jnp.zeros((rows_q, 1), F32)
    for hh in range(H_A):
        slope = jnp.where(r_h == hh, ALIBI_SLOPES[hh], slope)
    qposf = (past + r_t).astype(F32)
    q = q_ref[0]

    def sel_bias(lo, width):
        pos = lo + lax.broadcasted_iota(I32, (1, width), 1)
        out = jnp.full((rows_q, width), NEG, F32)
        for t in range(n_new):
            kt = keys_ref[0, t:t + 1, lo:lo + width]
            thr = t_ref[0, t:t + 1, 0:1]
            keep_tie = jnp.logical_and(kt == thr, pos <= x_ref[0, t:t + 1, 0:1])
            chosen = jnp.logical_and(jnp.logical_or(kt > thr, keep_tie), kt > NEGINF_KEY)
            bias_t = jnp.where(chosen, 0.0, NEG)
            out = jnp.where(r_t == t, bias_t, out)
        return out

    def logits(k_chunk, lo, width):
        kposf = (lo + lax.broadcasted_iota(I32, (1, width), 1)).astype(F32)
        sel = sel_bias(lo, width)
        lg = _dot_nt(q, k_chunk) - slope * (qposf - kposf)
        return lg + sel

    for ch in range(past // cw):
        lg_sc[:, ch * cw:(ch + 1) * cw] = logits(kbuf[slot, ch * cw:(ch + 1) * cw, :].astype(BF16), ch * cw, cw)
    lg_sc[:, past:total] = logits(knew_ref[0], past, LANES)

    lg = lg_sc[...]
    m = jnp.max(lg, axis=1, keepdims=True)
    p = jnp.exp(lg - m)
    l = jnp.sum(p, axis=1, keepdims=True)
    pb = p.astype(BF16)
    acc = _dot(pb[:, past:total], vnew_ref[0])
    for ch in range(past // cw):
        acc = acc + _dot(pb[:, ch * cw:(ch + 1) * cw], vbuf[slot, ch * cw:(ch + 1) * cw, :].astype(BF16))
    o_ref[0] = (acc / l).astype(BF16)


def _dsa_sample(page_table, iq32, w32, q16, iknew, knew, vnew, cache_kidx, cache_k, cache_v, n_new):
    db, n_pages = page_table.shape
    past = n_pages * PAGE
    total = past + LANES
    topk = min(TOPK_MAX, (past + n_new) // 4)
    cw = 1024 if past % 1024 == 0 else PAGE
    rows_q = n_new * H_A

    n_rows = db * n_new
    nchunk = total // LANES
    srows = _pick_tile(n_rows, 128)

    def per_b(a):
        return pl.BlockSpec((1,) + a.shape[1:], lambda b, pt: (b, 0, 0))

    hbm = pl.BlockSpec(memory_space=pl.ANY)

    keys = pl.pallas_call(
        functools.partial(_sample_scores_kernel, n_pages=n_pages, n_new=n_new, cw=cw),
        grid_spec=pltpu.PrefetchScalarGridSpec(
            num_scalar_prefetch=1,
            grid=(db,),
            in_specs=[per_b(iq32), per_b(w32), per_b(iknew), hbm],
            out_specs=pl.BlockSpec((1, n_new, total), lambda b, pt: (b, 0, 0)),
            scratch_shapes=[pltpu.VMEM((2, D_IDX, past), F32), pltpu.SemaphoreType.DMA((2,))],
        ),
        out_shape=jax.ShapeDtypeStruct((db, n_new, total), I32),
        compiler_params=_cparams(("arbitrary",)),
        name="sample_scores",
    )(page_table, iq32, w32, iknew, cache_kidx)

    keys_cm = keys.reshape(n_rows, nchunk, LANES).transpose(1, 0, 2)
    thr, cut = pl.pallas_call(
        functools.partial(_sample_search_kernel, nchunk=nchunk, rows=srows, topk=topk),
        grid=(n_rows // srows,),
        in_specs=[pl.BlockSpec((nchunk, srows, LANES), lambda i: (0, i, 0))],
        out_specs=[pl.BlockSpec((srows, LANES), lambda i: (i, 0))] * 2,
        out_shape=[jax.ShapeDtypeStruct((n_rows, LANES), I32)] * 2,
        scratch_shapes=[pltpu.VMEM((srows, LANES), I32), pltpu.VMEM((srows, LANES), F32),
                        pltpu.VMEM((srows, LANES), F32)],
        compiler_params=_cparams(("parallel",)),
        name="sample_search",
    )(keys_cm)

    thr3 = thr.reshape(db, n_new, LANES)
    cut3 = cut.reshape(db, n_new, LANES)
    return pl.pallas_call(
        functools.partial(_sample_attend_kernel, n_pages=n_pages, n_new=n_new, cw=cw),
        grid_spec=pltpu.PrefetchScalarGridSpec(
            num_scalar_prefetch=1,
            grid=(db,),
            in_specs=[per_b(q16), per_b(knew), per_b(vnew), per_b(keys), per_b(thr3), per_b(cut3), hbm, hbm],
            out_specs=pl.BlockSpec((1, rows_q, DH_A), lambda b, pt: (b, 0, 0)),
            scratch_shapes=[pltpu.VMEM((2, past, DH_A), F32), pltpu.VMEM((2, past, DH_A), F32),
                            pltpu.SemaphoreType.DMA((2, 2)), pltpu.VMEM((rows_q, total), F32)],
        ),
        out_shape=jax.ShapeDtypeStruct((db, rows_q, DH_A), BF16),
        compiler_params=_cparams(("arbitrary",)),
        name="sample_attend",
    )(page_table, q16, knew, vnew, keys, thr3, cut3, cache_k, cache_v)


def _outffn_kernel(x_ref, hm_ref, ha_ref, mod_ref, g2_ref, wom_ref, woa_ref, wg_ref, wu_ref, wd_ref, y_ref):
    mix = _dot(hm_ref[...], wom_ref[...]) + _dot(ha_ref[...], woa_ref[...])
    x1 = x_ref[...] + mod_ref[0, 0] * mix
    xn = x1 * lax.rsqrt(jnp.mean(x1 * x1, axis=-1, keepdims=True) + EPS) * g2_ref[...]
    hb = (xn * (1.0 + mod_ref[2, 0]) + mod_ref[1, 0]).astype(BF16)
    g = _dot(hb, wg_ref[...])
    u = _dot(hb, wu_ref[...])
    act = (g * _sigmoid(g) * u).astype(BF16)
    y_ref[...] = x1 + mod_ref[3, 0] * _dot(act, wd_ref[...])


def _outffn(x, hm, ha, mod_out, per_token_mod, rows_per_mod, g2, wom, woa, wg, wu, wd, tm):
    n, d = x.shape
    if per_token_mod:
        mod_spec = pl.BlockSpec((4, 1, tm, d), lambda i: (0, 0, i, 0))
    else:
        tiles = rows_per_mod // tm
        mod_spec = pl.BlockSpec((4, 1, 1, d), lambda i: (0, i // tiles, 0, 0))

    def full(a):
        return pl.BlockSpec(a.shape, lambda i: (0,) * a.ndim, pipeline_mode=pl.Buffered(1))

    def rows(w):
        return pl.BlockSpec((tm, w), lambda i: (i, 0))

    return pl.pallas_call(
        _outffn_kernel,
        grid=(n // tm,),
        in_specs=[rows(d), rows(512), rows(512), mod_spec, full(g2), full(wom), full(woa),
                  full(wg), full(wu), full(wd)],
        out_specs=rows(d),
        out_shape=jax.ShapeDtypeStruct((n, d), F32),
        compiler_params=_cparams(("parallel",)),
        name="outffn",
    )(x, hm, ha, mod_out, g2, wom, woa, wg, wu, wd)


def _pick_tile(n, pref):
    t = pref
    while n % t:
        t //= 2
    return t


def kernel(x_prompt, x_sample, cache_k, cache_v, cache_kidx, state_C, state_n, state_m, page_table,
           c_prompt, c_sample, w_ada, b_ada, g_norm1, w_in, b_igate, b_fgate, mlstm_norm_g,
           q_norm_g, k_norm_g, w_out, g_norm2, w_gate, w_up, w_down):
    bp, seq, d = x_prompt.shape
    db, t_new, _ = x_sample.shape
    n_p, n_s = bp * seq, db * t_new

    o = np.cumsum([0, 512, 512, 512, 512, H_M, H_M, 512, DH_A, DH_A, H_IDX * D_IDX, D_IDX, H_IDX])
    mq, mk, mv, mo, mi, mf, aq, ak, av, iq, ik, iw = [w_in[:, int(o[j]):int(o[j + 1])] for j in range(12)]
    zpad = lambda wdt: jnp.zeros((d, wdt), w_in.dtype)
    w_r = jnp.concatenate([mq, mk, mv, mo, aq, ak, av, iq, ik, zpad(LANES - D_IDX),
                           iw, mi, mf, zpad(LANES - 16)], axis=1).astype(BF16)
    wgt = jnp.concatenate([mi, mf], axis=1).T.astype(BF16)
    bsm = jnp.concatenate([jnp.zeros((H_IDX,), F32), b_igate, b_fgate]).reshape(1, 16)
    brow = jnp.concatenate([b_igate, b_fgate]).reshape(8, 1)
    g1 = g_norm1.reshape(1, d)
    g2 = g_norm2.reshape(1, d)
    qg = q_norm_g.reshape(1, DH_A)
    kg = k_norm_g.reshape(1, DH_A)
    ng = mlstm_norm_g.reshape(1, H_M * DH_M)
    wom = w_out[0:H_M * DH_M].astype(BF16)
    woa = w_out[H_M * DH_M:].astype(BF16)
    wg = w_gate.astype(BF16)
    wu = w_up.astype(BF16)
    wd = w_down.astype(BF16)

    mod = _ada(jnp.concatenate([c_prompt, c_sample], axis=0), w_ada.astype(BF16), b_ada)
    mod_p = mod[:bp].reshape(bp, 6, 1, d).transpose(1, 0, 2, 3)
    mod_s = jnp.repeat(mod[bp:].reshape(db, 6, d), t_new, axis=0)
    mod_s = mod_s.transpose(1, 0, 2).reshape(6, 1, n_s, d)

    tm_p = _pick_tile(seq, 512)
    (mqkv, mo_p, aq_p, k_p, v_p, kidx_p, kb, vb, ikb, iq_p, small_p, grow_p) = _inproj(
        x_prompt.reshape(n_p, d), mod_p[0:2], False, seq, g1, w_r, wgt, bsm, brow, qg, kg, tm_p)

    lc = _pick_tile(seq, 256)
    grow3 = grow_p.reshape(8, n_p // lc, lc).transpose(1, 0, 2)
    hm_p, C_p, n_pst, m_pb = _mlstm(
        mqkv, small_p, grow3, mo_p, ng,
        jnp.zeros((bp, H_M, DH_M, DH_M), F32), jnp.zeros((bp, H_M, DH_M), F32),
        jnp.zeros((bp, H_M, LANES), F32), bp, lc)

    tq = _pick_tile(seq, 512)
    ha_p = _dsa_prompt(iq_p, small_p, aq_p, ikb, kb, vb, bp, seq, tq)

    y_p = _outffn(x_prompt.reshape(n_p, d), hm_p, ha_p, mod_p[2:6], False, seq, g2,
                  wom, woa, wg, wu, wd, tm_p)

    tm_s = _pick_tile(n_s, 128)
    (mqkv_s, mo_s, aq_s, k_s, v_s, kidx_s, kb_s, vb_s, ikb_s, iq_s, small_s, grow_s) = _inproj(
        x_sample.reshape(n_s, d), mod_s[0:2], True, 0, g1, w_r, wgt, bsm, brow, qg, kg, tm_s)

    lp = 16
    pad_tok = lambda a: jnp.pad(a.reshape(db, t_new, a.shape[-1]),
                                ((0, 0), (0, lp - t_new), (0, 0))).reshape(db * lp, a.shape[-1])
    gate_pad = jnp.concatenate([jnp.zeros((H_IDX,), F32), jnp.full((H_M,), -jnp.inf, F32),
                                jnp.zeros((H_M,), F32)])
    small_pad = jnp.concatenate(
        [small_s.reshape(db, t_new, 16), jnp.broadcast_to(gate_pad, (db, lp - t_new, 16))], axis=1
    ).reshape(db * lp, 16)
    grow_pad = jnp.concatenate(
        [grow_s.reshape(8, db, t_new),
         jnp.broadcast_to(gate_pad[H_IDX:].reshape(8, 1, 1), (8, db, lp - t_new))], axis=2
    ).transpose(1, 0, 2)
    hm_s_pad, C_s, n_sst, m_sb = _mlstm(
        pad_tok(mqkv_s), small_pad, grow_pad, pad_tok(mo_s), ng,
        state_C, state_n, jnp.broadcast_to(state_m[:, :, None], (db, H_M, LANES)), db, lp)
    hm_s = hm_s_pad.reshape(db, lp, H_M * DH_M)[:, :t_new].reshape(n_s, H_M * DH_M)

    pad_rows = lambda a: jnp.pad(a.reshape(db, t_new, a.shape[-1]), ((0, 0), (0, LANES - t_new), (0, 0)))
    ha_s = _dsa_sample(
        page_table, iq_s.reshape(db, t_new * H_IDX, D_IDX),
        small_s[:, 0:H_IDX].reshape(db, t_new * H_IDX, 1),
        aq_s.reshape(db, t_new * H_A, DH_A),
        pad_rows(ikb_s), pad_rows(kb_s), pad_rows(vb_s),
        jnp.swapaxes(cache_kidx, 1, 2), cache_k, cache_v, t_new).reshape(n_s, H_A * DH_A)

    y_s = _outffn(x_sample.reshape(n_s, d), hm_s, ha_s, mod_s[2:6], True, 0, g2,
                  wom, woa, wg, wu, wd, tm_s)

    return (y_p.reshape(bp, seq, d), y_s.reshape(db, t_new, d),
            k_p.reshape(bp, seq, DH_A), v_p.reshape(bp, seq, DH_A), kidx_p.reshape(bp, seq, D_IDX),
            C_p, n_pst, m_pb[:, :, 0],
            k_s.reshape(db, t_new, DH_A), v_s.reshape(db, t_new, DH_A), kidx_s.reshape(db, t_new, D_IDX),
            C_s, n_sst, m_sb[:, :, 0])
```

```python
import functools

import jax
import jax.numpy as jnp
import numpy as np
from jax import lax
from jax.experimental import pallas as pl
from jax.experimental.pallas import tpu as pltpu

F32 = jnp.float32
BF16 = jnp.bfloat16
I32 = jnp.int32

H_M = 4
DH_M = 128
H_A = 4
DH_A = 128
H_IDX = 8
D_IDX = 64
TOPK_MAX = 256
PAGE = 128
EPS = 1e-6
INDEX_SCALE = D_IDX ** -0.5 * H_IDX ** -0.5
ALIBI_SLOPES = tuple(float(2.0 ** (-8.0 * (h + 1) / H_A)) for h in range(H_A))

LANES = 128
VMEM_LIMIT = 56 * 1024 * 1024
NEG = -1e30
I32_MIN = -2 ** 31
F32_TINY = float(np.finfo(np.float32).tiny)
NEGINF_KEY = -2139095041
INDEX_BITS = 14

C_MQ, C_MK, C_MV, C_MO, C_AQ, C_AK, C_AV, C_IQ, C_IK, C_SM, C_END = (
    0, 512, 1024, 1536, 2048, 2560, 2688, 2816, 3328, 3456, 3584)


def _cparams(sem):
    return pltpu.CompilerParams(dimension_semantics=sem, vmem_limit_bytes=VMEM_LIMIT)


def _sigmoid(x):
    return 1.0 / (1.0 + jnp.exp(-x))


def _log_sigmoid(x):
    return jnp.minimum(x, 0.0) - jnp.log1p(jnp.exp(-jnp.abs(x)))


def _dot(a, b):
    return jnp.dot(a, b, preferred_element_type=F32)


def _dot_nt(a, b):
    return lax.dot_general(a, b, (((1,), (1,)), ((), ())), preferred_element_type=F32)


def _dot_tn(a, b):
    return lax.dot_general(a, b, (((0,), (0,)), ((), ())), preferred_element_type=F32)


def _split3(x):
    hi = x.astype(BF16)
    r1 = x - hi.astype(F32)
    mid = r1.astype(BF16)
    lo = (r1 - mid.astype(F32)).astype(BF16)
    return hi, mid, lo


def _sort_key(x):
    b = pltpu.bitcast(x, I32)
    return b ^ ((b >> 31) & 0x7FFFFFFF)


def _score_key(score, pos):
    return jnp.where(jnp.abs(score) < F32_TINY, -pos, _sort_key(score))


def _ada_kernel(c_ref, w_ref, b_ref, o_ref):
    c = c_ref[...]
    s = c * _sigmoid(c)
    o_ref[...] = _dot(s.astype(BF16), w_ref[...]) + b_ref[...]


def _ada(c, w_bf, b):
    r, d = c.shape
    n = w_bf.shape[1]
    return pl.pallas_call(
        _ada_kernel,
        grid=(n // d,),
        in_specs=[pl.BlockSpec((r, d), lambda j: (0, 0)),
                  pl.BlockSpec((d, d), lambda j: (0, j)),
                  pl.BlockSpec((1, d), lambda j: (0, j))],
        out_specs=pl.BlockSpec((r, d), lambda j: (0, j)),
        out_shape=jax.ShapeDtypeStruct((r, n), F32),
        compiler_params=_cparams(("arbitrary",)),
        name="ada",
    )(c, w_bf, b.reshape(1, n))


def _inproj_kernel(x_ref, mod_ref, g1_ref, w_ref, wgt_ref, bsm_ref, brow_ref, qg_ref, kg_ref,
                   mqkv_ref, mo_ref, aq_ref, k_ref, v_ref, kidx_ref, kb_ref, vb_ref, ikb_ref,
                   iq_ref, small_ref, grow_ref):
    x = x_ref[...]
    xn = x * lax.rsqrt(jnp.mean(x * x, axis=-1, keepdims=True) + EPS) * g1_ref[...]
    h = xn * (1.0 + mod_ref[1, 0]) + mod_ref[0, 0]
    hb = h.astype(BF16)

    def sec(a, b):
        return _dot(hb, w_ref[:, a:b])

    mqkv_ref[:, 0:512] = sec(C_MQ, C_MK).astype(BF16)
    mqkv_ref[:, 512:1024] = (sec(C_MK, C_MV) * (DH_M ** -0.5)).astype(BF16)
    mqkv_ref[:, 1024:1536] = sec(C_MV, C_MO).astype(BF16)
    mo_ref[...] = sec(C_MO, C_AQ)

    aq = sec(C_AQ, C_AK)
    qg = qg_ref[...]
    for hh in range(H_A):
        a = aq[:, hh * DH_A:(hh + 1) * DH_A]
        a = a * lax.rsqrt(jnp.mean(a * a, axis=-1, keepdims=True) + EPS) * qg
        aq_ref[:, hh * DH_A:(hh + 1) * DH_A] = (a * (DH_A ** -0.5)).astype(BF16)

    ak = sec(C_AK, C_AV)
    ak = ak * lax.rsqrt(jnp.mean(ak * ak, axis=-1, keepdims=True) + EPS) * kg_ref[...]
    k_ref[...] = ak
    kb_ref[...] = ak.astype(BF16)
    av = sec(C_AV, C_IQ)
    v_ref[...] = av
    vb_ref[...] = av.astype(BF16)

    iq_ref[...] = sec(C_IQ, C_IK).astype(BF16)
    ik = sec(C_IK, C_SM)[:, 0:D_IDX]
    kidx_ref[...] = ik
    ikb_ref[...] = ik.astype(BF16)

    sm = sec(C_SM, C_END)[:, 0:16] + bsm_ref[...]
    col = lax.broadcasted_iota(I32, sm.shape, 1)
    small_ref[...] = jnp.where(col < H_IDX, sm * INDEX_SCALE,
                               jnp.where(col < H_IDX + H_M, sm, _log_sigmoid(sm)))

    gr = _dot_nt(wgt_ref[...], hb) + brow_ref[...]
    row = lax.broadcasted_iota(I32, gr.shape, 0)
    grow_ref[...] = jnp.where(row < H_M, gr, _log_sigmoid(gr))


def _inproj(x, mod_in, per_token_mod, rows_per_mod, g1, w_r, wgt, bsm, brow, qg, kg, tm):
    n, d = x.shape
    if per_token_mod:
        mod_spec = pl.BlockSpec((2, 1, tm, d), lambda i: (0, 0, i, 0))
    else:
        tiles = rows_per_mod // tm
        mod_spec = pl.BlockSpec((2, 1, 1, d), lambda i: (0, i // tiles, 0, 0))

    def full(a):
        return pl.BlockSpec(a.shape, lambda i: (0,) * a.ndim)

    def rows(w):
        return pl.BlockSpec((tm, w), lambda i: (i, 0))

    outs = [(1536, BF16), (512, F32), (512, BF16), (128, F32), (128, F32), (D_IDX, F32),
            (128, BF16), (128, BF16), (D_IDX, BF16), (512, BF16), (16, F32)]
    out_shape = [jax.ShapeDtypeStruct((n, w), dt) for w, dt in outs]
    out_specs = [rows(w) for w, _ in outs]
    out_shape.append(jax.ShapeDtypeStruct((8, n), F32))
    out_specs.append(pl.BlockSpec((8, tm), lambda i: (0, i)))
    return pl.pallas_call(
        _inproj_kernel,
        grid=(n // tm,),
        in_specs=[rows(d), mod_spec, full(g1), full(w_r), full(wgt), full(bsm), full(brow),
                  full(qg), full(kg)],
        out_specs=out_specs,
        out_shape=out_shape,
        compiler_params=_cparams(("parallel",)),
        name="inproj",
    )(x, mod_in, g1, w_r, wgt, bsm, brow, qg, kg)


def _mlstm_kernel(q_ref, k_ref, v_ref, small_ref, grow_ref, o_ref, ng_ref, c0_ref, n0_ref, m0_ref,
                  h_ref, c_out_ref, n_out_ref, m_out_ref, state_sc, m_sc, *, L, nc):
    c = pl.program_id(1)

    @pl.when(c == 0)
    def _():
        for hh in range(H_M):
            state_sc[hh, 0:DH_M, :] = c0_ref[0, hh]
            state_sc[hh, DH_M:2 * DH_M, :] = jnp.broadcast_to(n0_ref[0, hh:hh + 1, :], (DH_M, DH_M))
        m_sc[...] = jnp.zeros(m_sc.shape, F32)
        m_sc[0:H_M, :] = m0_ref[0]

    ti = lax.broadcasted_iota(I32, (L, L), 0)
    si = lax.broadcasted_iota(I32, (L, L), 1)
    causal = si <= ti
    tri = jnp.where(causal, 1.0, 0.0).astype(BF16)
    tri_t = jnp.where(ti <= si, 1.0, 0.0).astype(BF16)

    sm = small_ref[...]
    col = lax.broadcasted_iota(I32, sm.shape, 1)
    lf_cols = jnp.where(col >= H_IDX + H_M, sm, 0.0)
    hi, mid, lo = _split3(lf_cols)
    b_cols = _dot(tri, hi) + _dot(tri, mid) + _dot(tri, lo)
    gr = grow_ref[0]
    row = lax.broadcasted_iota(I32, gr.shape, 0)
    lf_rows = jnp.where(row >= H_M, gr, 0.0)
    hi, mid, lo = _split3(lf_rows)
    b_rows = _dot(hi, tri_t) + _dot(mid, tri_t) + _dot(lo, tri_t)

    ones_blk = jnp.ones((L, DH_M), BF16)

    def across(x):
        return x[:, 0:L] if L <= LANES else jnp.concatenate([x] * (L // LANES), axis=1)

    for hh in range(H_M):
        q = q_ref[:, hh * DH_M:(hh + 1) * DH_M]
        k = k_ref[:, hh * DH_M:(hh + 1) * DH_M]
        v = v_ref[:, hh * DH_M:(hh + 1) * DH_M]
        ig_c = jnp.broadcast_to(sm[:, H_IDX + hh:H_IDX + hh + 1], (L, LANES))
        b_c = jnp.broadcast_to(b_cols[:, H_IDX + H_M + hh:H_IDX + H_M + hh + 1], (L, LANES))
        ig_r = gr[hh:hh + 1, :]
        b_r = b_rows[H_M + hh:H_M + hh + 1, :]
        m_prev = m_sc[hh:hh + 1, :]

        dmat = jnp.where(causal, across(b_c) - b_r + ig_r, -jnp.inf)
        inter = b_c + m_prev
        m_t = jnp.maximum(inter, jnp.max(dmat, axis=-1, keepdims=True))
        w = jnp.exp(dmat - across(m_t))
        g = jnp.exp(inter - m_t)
        a = w * _dot_nt(q, k)
        st = state_sc[hh]
        qs = _dot_nt(q, st.astype(BF16))
        num = g * qs[:, 0:DH_M] + _dot(a.astype(BF16), v)
        den = g * qs[:, DH_M:2 * DH_M] + jnp.sum(a, axis=-1, keepdims=True)
        hv = num / jnp.maximum(jnp.abs(den), jnp.exp(-m_t))

        m_new = m_t[L - 1:L, :]
        b_last = b_c[L - 1:L, :]
        wend = jnp.exp(b_last - b_c + ig_c - m_new)
        gend = jnp.exp(b_last + m_prev - m_new)
        kw = (k.astype(F32) * wend).astype(BF16)
        v_aug = jnp.concatenate([v, ones_blk], axis=1)
        state_sc[hh] = gend * st + _dot_tn(v_aug, kw)
        m_sc[hh:hh + 1, :] = m_new

        hn = hv * lax.rsqrt(jnp.mean(hv * hv, axis=-1, keepdims=True) + EPS)
        hn = hn * ng_ref[:, hh * DH_M:(hh + 1) * DH_M]
        hn = hn * _sigmoid(o_ref[:, hh * DH_M:(hh + 1) * DH_M])
        h_ref[:, hh * DH_M:(hh + 1) * DH_M] = hn.astype(BF16)

    @pl.when(c == nc - 1)
    def _():
        for hh in range(H_M):
            c_out_ref[0, hh] = state_sc[hh, 0:DH_M, :]
            n_out_ref[0, hh:hh + 1, :] = state_sc[hh, DH_M:DH_M + 1, :]
        m_out_ref[0] = m_sc[0:H_M, :]


def _mlstm(mqkv, small, grow3, mo, ng, c0, n0, m0b, nb, L):
    n = mqkv.shape[0]
    nc = n // nb // L
    d = H_M * DH_M

    def tok(w, blk=0):
        return pl.BlockSpec((L, w), lambda b, c, blk=blk: (b * nc + c, blk))

    per_b3 = pl.BlockSpec((1, H_M, LANES), lambda b, c: (b, 0, 0))
    per_b4 = pl.BlockSpec((1, H_M, DH_M, DH_M), lambda b, c: (b, 0, 0, 0))
    return pl.pallas_call(
        functools.partial(_mlstm_kernel, L=L, nc=nc),
        grid=(nb, nc),
        in_specs=[tok(d, 0), tok(d, 1), tok(d, 2), tok(16),
                  pl.BlockSpec((1, 8, L), lambda b, c: (b * nc + c, 0, 0)),
                  tok(d), pl.BlockSpec((1, d), lambda b, c: (0, 0)),
                  per_b4, per_b3, per_b3],
        out_specs=[tok(d), per_b4, per_b3, per_b3],
        out_shape=[jax.ShapeDtypeStruct((n, d), BF16),
                   jax.ShapeDtypeStruct((nb, H_M, DH_M, DH_M), F32),
                   jax.ShapeDtypeStruct((nb, H_M, DH_M), F32),
                   jax.ShapeDtypeStruct((nb, H_M, LANES), F32)],
        scratch_shapes=[pltpu.VMEM((H_M, 2 * DH_M, DH_M), F32), pltpu.VMEM((8, LANES), F32)],
        compiler_params=_cparams(("parallel", "arbitrary")),
        name="mlstm",
    )(mqkv, mqkv, mqkv, small, grow3, mo, ng, c0, n0, m0b)


def _transpose32(words):
    a = list(words)
    j, m = 16, 0x0000FFFF
    while j:
        k = 0
        while k < 32:
            t = (a[k] ^ lax.shift_right_logical(a[k + j], jnp.int32(j))) & m
            a[k] = a[k] ^ t
            a[k + j] = a[k + j] ^ (t << j)
            k = (k + j + 1) & ~j
        j >>= 1
        m = (m ^ (m << j)) & 0xFFFFFFFF
    return a


def _topk_search(keys_sc, cand_sc, t_sc, tu_sc, aux_sc, cnt_sc, *, nchunk, rows, tk, hb, topk,
                 last_chunk_first_group=0, planes=None, nchunk_max=None):
    ncol = tk // LANES
    ngrp = rows // hb
    lane_hb = lax.broadcasted_iota(I32, (hb, LANES), 1)

    def count_rows(hit_fn):
        cnt_sc[...] = jnp.zeros(cnt_sc.shape, F32)

        def body(c, _, first_group=0):
            for g in range(first_group, ngrp):
                acc = cnt_sc[g * hb:(g + 1) * hb, :]
                for j in range(ncol):
                    kj = keys_sc[c, g * hb:(g + 1) * hb, j * LANES:(j + 1) * LANES]
                    acc = acc + jnp.where(hit_fn(c, g, j, kj), 1.0, 0.0)
                cnt_sc[g * hb:(g + 1) * hb, :] = acc
            return 0

        lax.fori_loop(0, nchunk - 1, body, 0)
        body(nchunk - 1, 0, last_chunk_first_group)
        return jnp.sum(cnt_sc[...], axis=1, keepdims=True)

    def hit_ge(c, g, j, kj):
        return kj >= cand_sc[g * hb:(g + 1) * hb, :]

    tu_sc[...] = jnp.zeros(tu_sc.shape, I32)
    if planes is None:
        def search_pass(p, _):
            bit = jnp.left_shift(jnp.int32(1), 31 - p)
            cand_sc[...] = (tu_sc[...] | bit) ^ I32_MIN
            tu_sc[...] = jnp.where(count_rows(hit_ge) >= topk, tu_sc[...] | bit, tu_sc[...])
            return 0

        lax.fori_loop(0, 32, search_pass, 0)
        t_sc[...] = tu_sc[...] ^ I32_MIN
        cand_sc[...] = t_sc[...]
        n_ge = count_rows(hit_ge)
    else:
        planes_sc, alive_sc, above_sc = planes
        nblk = nchunk_max * ncol
        nvalid = nchunk * ncol

        def tile_body(rt, _):
            r0 = pl.multiple_of(rt * 8, 8)
            words = []
            for j in range(32):
                if j < nblk:
                    kj = keys_sc[j // ncol, pl.ds(r0, 8), (j % ncol) * LANES:(j % ncol + 1) * LANES]
                    kj = jnp.where(j < nvalid, kj, NEGINF_KEY)
                else:
                    kj = jnp.full((8, LANES), NEGINF_KEY, I32)
                words.append(kj)
            for q, word in enumerate(_transpose32(words)):
                planes_sc[q, pl.ds(r0, 8), :] = ~word if q == 0 else word
            return 0

        lax.fori_loop(0, rows // 8, tile_body, 0)
        alive_sc[...] = jnp.full(alive_sc.shape, -1, I32)
        above_sc[...] = jnp.zeros(above_sc.shape, F32)

        def select_pass(q, _):
            alive = alive_sc[...]
            ones = alive & planes_sc[q]
            n_ones = jnp.sum(lax.population_count(ones).astype(F32), axis=1, keepdims=True)
            above = above_sc[...]
            take = above + n_ones >= topk
            alive_sc[...] = jnp.where(take, ones, alive ^ ones)
            above_sc[...] = jnp.where(take, above, above + n_ones)
            tu_sc[...] = jnp.where(take, tu_sc[...] | jnp.left_shift(jnp.int32(1), 31 - q), tu_sc[...])
            return 0

        lax.fori_loop(0, 32, select_pass, 0)
        t_sc[...] = tu_sc[...] ^ I32_MIN
        n_ge = above_sc[...] + jnp.sum(lax.population_count(alive_sc[...]).astype(F32), axis=1, keepdims=True)

    tied = jnp.logical_and(n_ge > topk, t_sc[...] > NEGINF_KEY)
    any_tied = jnp.max(jnp.where(tied, 1.0, 0.0)) > 0.0
    tu_sc[...] = jnp.full(tu_sc.shape, 2 ** 31 - 1, I32)

    @pl.when(any_tied)
    def _():
        cand_sc[...] = t_sc[...] + 1
        aux_sc[...] = jnp.broadcast_to(topk - count_rows(hit_ge), aux_sc.shape)

        def hit_tie_below(c, g, j, kj):
            pos = c * tk + j * LANES + lane_hb
            return jnp.where(kj == t_sc[g * hb:(g + 1) * hb, :], pos, 2 ** 30) < cand_sc[g * hb:(g + 1) * hb, :]

        tu_sc[...] = jnp.zeros(tu_sc.shape, I32)

        def cut_pass(p, _):
            bit = jnp.left_shift(jnp.int32(1), INDEX_BITS - 1 - p)
            cand_sc[...] = tu_sc[...] | bit
            tu_sc[...] = jnp.where(count_rows(hit_tie_below) < aux_sc[...], tu_sc[...] | bit, tu_sc[...])
            return 0

        lax.fori_loop(0, INDEX_BITS, cut_pass, 0)

    return any_tied


def _dsa_prompt_kernel(iq_ref, small_ref, aq_ref, qpos_ref, ik_ref, k_ref, v_ref, kpos_ref, o_ref,
                       qidx_sc, wrep_sc, qa_sc, s_sc, keys_sc, lg_sc, p_sc, alpha_sc, m_sc, l_sc, acc_sc,
                       cand_sc, t_sc, tu_sc, aux_sc, cnt_sc, planes_sc, alive_sc, above_sc,
                       *, tq, tk, topk, rb, hb, use_planes):
    i = pl.program_id(1)
    q0 = i * tq
    nchunk = (i + 1) * (tq // tk)
    ncol = tk // LANES
    nrb = tq // rb
    lane_hb = lax.broadcasted_iota(I32, (hb, LANES), 1)
    lane_rb = lax.broadcasted_iota(I32, (rb, LANES), 1)
    lane_minus_row = lane_rb - lax.broadcasted_iota(I32, (rb, LANES), 0)

    for h in range(H_IDX):
        qidx_sc[h * tq:(h + 1) * tq, :] = iq_ref[:, h * D_IDX:(h + 1) * D_IDX]
        wrep_sc[h] = jnp.broadcast_to(small_ref[:, h:h + 1], (tq, LANES))
    for hh in range(H_A):
        qa_sc[hh * tq:(hh + 1) * tq, 0:DH_A] = aq_ref[:, hh * DH_A:(hh + 1) * DH_A]
        qa_sc[hh * tq:(hh + 1) * tq, DH_A:2 * DH_A] = qpos_ref[0, hh * tq:(hh + 1) * tq, :]

    row_lo = tq - tk

    def score_body(c, _, lo=0):
        ikc = ik_ref[pl.ds(pl.multiple_of(c * tk, tk), tk), :]
        if lo == 0:
            s_sc[...] = _dot_nt(qidx_sc[...], ikc)
        else:
            for h in range(H_IDX):
                s_sc[h * tq + lo:(h + 1) * tq, :] = _dot_nt(qidx_sc[h * tq + lo:(h + 1) * tq, :], ikc)
            keys_sc[c, 0:lo, :] = jnp.full((lo, tk), NEGINF_KEY, I32)
        bound = q0 - c * tk
        for r in range(lo // rb, nrb):
            accs = [jnp.zeros((rb, LANES), F32) for _ in range(ncol)]
            for h in range(H_IDX):
                wv = wrep_sc[h, r * rb:(r + 1) * rb, :]
                for j in range(ncol):
                    s = s_sc[h * tq + r * rb:h * tq + (r + 1) * rb, j * LANES:(j + 1) * LANES]
                    accs[j] = accs[j] + wv * jnp.maximum(s, 0.0)
            for j in range(ncol):
                acc = jnp.where(lane_minus_row <= bound + (r * rb - j * LANES), accs[j], -jnp.inf)
                keys_sc[c, r * rb:(r + 1) * rb, j * LANES:(j + 1) * LANES] = _score_key(
                    acc, c * tk + j * LANES + lane_rb)
        return 0

    lax.fori_loop(0, nchunk - 1, score_body, 0)
    score_body(nchunk - 1, 0, row_lo)

    any_tied = _topk_search(keys_sc, cand_sc, t_sc, tu_sc, aux_sc, cnt_sc, nchunk=nchunk, rows=tq, tk=tk, hb=hb,
                            topk=topk, last_chunk_first_group=(tq - tk) // hb,
                            planes=(planes_sc, alive_sc, above_sc) if use_planes else None,
                            nchunk_max=keys_sc.shape[0])

    @pl.when(any_tied)
    def _():
        def drop_body(c, _):
            for g in range(tq // hb):
                tb = t_sc[g * hb:(g + 1) * hb, :]
                xb = tu_sc[g * hb:(g + 1) * hb, :]
                for j in range(ncol):
                    kj = keys_sc[c, g * hb:(g + 1) * hb, j * LANES:(j + 1) * LANES]
                    pos = c * tk + j * LANES + lane_hb
                    late = jnp.where(kj == tb, pos, -1) > xb
                    keys_sc[c, g * hb:(g + 1) * hb, j * LANES:(j + 1) * LANES] = jnp.where(late, kj - 1, kj)
            return 0

        lax.fori_loop(0, nchunk, drop_body, 0)

    t_sc[...] = jnp.maximum(t_sc[...], NEGINF_KEY + 1)
    m_sc[...] = jnp.full(m_sc.shape, NEG, F32)
    l_sc[...] = jnp.zeros(l_sc.shape, F32)
    acc_sc[...] = jnp.zeros(acc_sc.shape, F32)

    def att_body(c, _, lo=0):
        start = pl.multiple_of(c * tk, tk)
        kaug = jnp.concatenate([k_ref[pl.ds(start, tk), :], kpos_ref[pl.ds(start, tk), :]], axis=1)
        head_rows = [slice(hh * tq + lo, (hh + 1) * tq) for hh in range(H_A)]
        if lo == 0:
            lg_sc[...] = _dot_nt(qa_sc[...], kaug)
        else:
            for rows in head_rows:
                lg_sc[rows, :] = _dot_nt(qa_sc[rows, :], kaug)
        for r in range(lo // rb, nrb):
            tb = t_sc[r * rb:(r + 1) * rb, :]
            selb = [jnp.where(keys_sc[c, r * rb:(r + 1) * rb, j * LANES:(j + 1) * LANES] >= tb, 0.0, NEG)
                    for j in range(ncol)]
            for hh in range(H_A):
                rows = slice(hh * tq + r * rb, hh * tq + (r + 1) * rb)
                lgs = [lg_sc[rows, j * LANES:(j + 1) * LANES] + selb[j] for j in range(ncol)]
                mx = lgs[0]
                for j in range(1, ncol):
                    mx = jnp.maximum(mx, lgs[j])
                m_old = m_sc[rows, :]
                m_new = jnp.maximum(m_old, jnp.max(mx, axis=1, keepdims=True))
                alpha = jnp.exp(m_old - m_new)
                ps = [jnp.exp(lg - m_new) for lg in lgs]
                psum = ps[0]
                for j in range(1, ncol):
                    psum = psum + ps[j]
                l_sc[rows, :] = alpha * l_sc[rows, :] + psum
                for j in range(ncol):
                    p_sc[rows, j * LANES:(j + 1) * LANES] = ps[j].astype(BF16)
                alpha_sc[rows, :] = alpha
                m_sc[rows, :] = m_new
        vc = v_ref[pl.ds(start, tk), :]
        if lo == 0:
            acc_sc[...] = alpha_sc[...] * acc_sc[...] + _dot(p_sc[...], vc)
        else:
            for rows in head_rows:
                acc_sc[rows, :] = alpha_sc[rows, :] * acc_sc[rows, :] + _dot(p_sc[rows, :], vc)
        return 0

    lax.fori_loop(0, nchunk - 1, att_body, 0)
    att_body(nchunk - 1, 0, row_lo)
    for hh in range(H_A):
        rows = slice(hh * tq, (hh + 1) * tq)
        l = jnp.sum(l_sc[rows, :], axis=1, keepdims=True)
        o_ref[:, hh * DH_A:(hh + 1) * DH_A] = (acc_sc[rows, :] / l).astype(BF16)


def _alibi_tables(seq, tq):
    pos = np.arange(seq)
    hi, lo = (pos // 64).astype(np.float32), (pos % 64).astype(np.float32)
    ktab = np.zeros((seq, DH_A), np.float32)
    ktab[:, 0], ktab[:, 1], ktab[:, 2], ktab[:, 3] = 64.0 * hi, lo, 1.0, 1.0
    nq = seq // tq
    qtab = np.zeros((nq, H_A, tq, DH_A), np.float32)
    for hh, slope in enumerate(ALIBI_SLOPES):
        qtab[:, hh, :, 0] = slope
        qtab[:, hh, :, 1] = slope
        qtab[:, hh, :, 2] = (-slope * 64.0 * hi).reshape(nq, tq)
        qtab[:, hh, :, 3] = (-slope * lo).reshape(nq, tq)
    return jnp.asarray(qtab.reshape(nq, H_A * tq, DH_A), BF16), jnp.asarray(ktab, BF16)


def _dsa_prompt(iq, small, aq, ikb, kb, vb, nb, seq, tq):
    n = iq.shape[0]
    nq = seq // tq
    topk = min(TOPK_MAX, seq // 4)
    tk = min(256, tq)
    rb = min(64, tq)
    hb = min(128, tq)
    use_planes = seq // LANES <= 32
    qtab, ktab = _alibi_tables(seq, tq)

    def tok(w):
        return pl.BlockSpec((tq, w), lambda b, i: (b * nq + i, 0))

    def per_b(w):
        return pl.BlockSpec((seq, w), lambda b, i: (b, 0))

    return pl.pallas_call(
        functools.partial(_dsa_prompt_kernel, tq=tq, tk=tk, topk=topk, rb=rb, hb=hb, use_planes=use_planes),
        grid=(nb, nq),
        in_specs=[tok(512), tok(16), tok(512),
                  pl.BlockSpec((1, H_A * tq, DH_A), lambda b, i: (i, 0, 0)),
                  per_b(D_IDX), per_b(DH_A), per_b(DH_A),
                  pl.BlockSpec((seq, DH_A), lambda b, i: (0, 0))],
        out_specs=tok(512),
        out_shape=jax.ShapeDtypeStruct((n, 512), BF16),
        scratch_shapes=[pltpu.VMEM((H_IDX * tq, D_IDX), BF16),
                        pltpu.VMEM((H_IDX, tq, LANES), F32),
                        pltpu.VMEM((H_A * tq, 2 * DH_A), BF16),
                        pltpu.VMEM((H_IDX * tq, tk), F32),
                        pltpu.VMEM((seq // tk, tq, tk), I32),
                        pltpu.VMEM((H_A * tq, tk), F32),
                        pltpu.VMEM((H_A * tq, tk), BF16),
                        pltpu.VMEM((H_A * tq, LANES), F32),
                        pltpu.VMEM((H_A * tq, LANES), F32),
                        pltpu.VMEM((H_A * tq, LANES), F32),
                        pltpu.VMEM((H_A * tq, DH_A), F32),
                        pltpu.VMEM((tq, LANES), I32),
                        pltpu.VMEM((tq, LANES), I32),
                        pltpu.VMEM((tq, LANES), I32),
                        pltpu.VMEM((tq, LANES), F32),
                        pltpu.VMEM((tq, LANES), F32),
                        pltpu.VMEM((32 if use_planes else 1, tq, LANES), I32),
                        pltpu.VMEM((tq, LANES), I32),
                        pltpu.VMEM((tq, LANES), F32)],
        compiler_params=_cparams(("parallel", "arbitrary")),
        name="dsa_prompt",
    )(iq, small, aq, qtab, ikb, kb, vb, ktab)


def _page_pipeline(pt_ref, copies_of, n_pages):
    b = pl.program_id(0)
    slot = b % 2

    def for_all(bb, sl, act):
        def body(j, _):
            for cp in copies_of(pt_ref[bb, j], sl, j):
                act(cp)
            return 0
        lax.fori_loop(0, n_pages, body, 0, unroll=8 if n_pages % 8 == 0 else 1)

    @pl.when(b == 0)
    def _():
        for_all(0, 0, lambda cp: cp.start())

    @pl.when(b + 1 < pl.num_programs(0))
    def _():
        for_all(b + 1, 1 - slot, lambda cp: cp.start())

    for_all(b, slot, lambda cp: cp.wait())
    return slot


def _sample_scores_kernel(pt_ref, iq_ref, w_ref, iknew_ref, ckidx_hbm, keys_ref, ibuf, sems,
                          *, n_pages, n_new, cw):
    past = n_pages * PAGE
    total = past + LANES

    def copies_of(pg, sl, j):
        dst = pl.ds(pl.multiple_of(j * PAGE, PAGE), PAGE)
        return (pltpu.make_async_copy(ckidx_hbm.at[pg], ibuf.at[sl, :, dst], sems.at[sl]),)

    slot = _page_pipeline(pt_ref, copies_of, n_pages)

    iq = iq_ref[0]
    w = w_ref[0]

    def scores(dots):
        s = jnp.maximum(dots, 0.0) * w
        return jnp.sum(s.reshape(n_new, H_IDX, s.shape[-1]), axis=1)

    for ch in range(past // cw):
        sc = scores(_dot(iq, ibuf[slot, :, ch * cw:(ch + 1) * cw].astype(BF16)))
        keys_ref[0, :, ch * cw:(ch + 1) * cw] = _score_key(sc, ch * cw + lax.broadcasted_iota(I32, sc.shape, 1))
    sc = scores(_dot_nt(iq, iknew_ref[0]))
    t_i = lax.broadcasted_iota(I32, (n_new, LANES), 0)
    j_i = lax.broadcasted_iota(I32, (n_new, LANES), 1)
    keys_ref[0, :, past:total] = _score_key(jnp.where(j_i <= t_i, sc, -jnp.inf), past + j_i)


def _sample_search_kernel(keys_ref, t_ref, x_ref, cand_sc, aux_sc, cnt_sc, *, nchunk, rows, topk):
    _topk_search(keys_ref, cand_sc, t_ref, x_ref, aux_sc, cnt_sc,
                 nchunk=nchunk, rows=rows, tk=LANES, hb=rows, topk=topk)


def _sample_attend_kernel(pt_ref, q_ref, knew_ref, vnew_ref, keys_ref, t_ref, x_ref, ck_hbm, cv_hbm, o_ref,
                          kbuf, vbuf, sems, lg_sc, *, n_pages, n_new, cw):
    past = n_pages * PAGE
    total = past + LANES
    rows_q = n_new * H_A

    def copies_of(pg, sl, j):
        dst = pl.ds(pl.multiple_of(j * PAGE, PAGE), PAGE)
        return (pltpu.make_async_copy(ck_hbm.at[pg], kbuf.at[sl, dst, :], sems.at[0, sl]),
                pltpu.make_async_copy(cv_hbm.at[pg], vbuf.at[sl, dst, :], sems.at[1, sl]))

    slot = _page_pipeline(pt_ref, copies_of, n_pages)

    r_i = lax.broadcasted_iota(I32, (rows_q, 1), 0)
    r_t = r_i // H_A
    r_h = r_i % H_A
    slope = jnp.zeros((rows_q, 1), F32)
    for hh in range(H_A):
        slope = jnp.where(r_h == hh, ALIBI_SLOPES[hh], slope)
    qposf = (past + r_t).astype(F32)
    q = q_ref[0]

    def sel_bias(lo, width):
        pos = lo + lax.broadcasted_iota(I32, (1, width), 1)
        out = jnp.full((rows_q, width), NEG, F32)
        for t in range(n_new):
            kt = keys_ref[0, t:t + 1, lo:lo + width]
            thr = t_ref[0, t:t + 1, 0:1]
            keep_tie = jnp.logical_and(kt == thr, pos <= x_ref[0, t:t + 1, 0:1])
            chosen = jnp.logical_and(jnp.logical_or(kt > thr, keep_tie), kt > NEGINF_KEY)
            bias_t = jnp.where(chosen, 0.0, NEG)
            out = jnp.where(r_t == t, bias_t, out)
        return out

    def logits(k_chunk, lo, width):
        kposf = (lo + lax.broadcasted_iota(I32, (1, width), 1)).astype(F32)
        sel = sel_bias(lo, width)
        lg = _dot_nt(q, k_chunk) - slope * (qposf - kposf)
        return lg + sel

    for ch in range(past // cw):
        lg_sc[:, ch * cw:(ch + 1) * cw] = logits(kbuf[slot, ch * cw:(ch + 1) * cw, :].astype(BF16), ch * cw, cw)
    lg_sc[:, past:total] = logits(knew_ref[0], past, LANES)

    lg = lg_sc[...]
    m = jnp.max(lg, axis=1, keepdims=True)
    p = jnp.exp(lg - m)
    l = jnp.sum(p, axis=1, keepdims=True)
    pb = p.astype(BF16)
    acc = _dot(pb[:, past:total], vnew_ref[0])
    for ch in range(past // cw):
        acc = acc + _dot(pb[:, ch * cw:(ch + 1) * cw], vbuf[slot, ch * cw:(ch + 1) * cw, :].astype(BF16))
    o_ref[0] = (acc / l).astype(BF16)


def _dsa_sample(page_table, iq32, w32, q16, iknew, knew, vnew, cache_kidx, cache_k, cache_v, n_new):
    db, n_pages = page_table.shape
    past = n_pages * PAGE
    total = past + LANES
    topk = min(TOPK_MAX, (past + n_new) // 4)
    cw = 1024 if past % 1024 == 0 else PAGE
    rows_q = n_new * H_A

    n_rows = db * n_new
    nchunk = total // LANES
    srows = _pick_tile(n_rows, 128)

    def per_b(a):
        return pl.BlockSpec((1,) + a.shape[1:], lambda b, pt: (b, 0, 0))

    hbm = pl.BlockSpec(memory_space=pl.ANY)

    keys = pl.pallas_call(
        functools.partial(_sample_scores_kernel, n_pages=n_pages, n_new=n_new, cw=cw),
        grid_spec=pltpu.PrefetchScalarGridSpec(
            num_scalar_prefetch=1,
            grid=(db,),
            in_specs=[per_b(iq32), per_b(w32), per_b(iknew), hbm],
            out_specs=pl.BlockSpec((1, n_new, total), lambda b, pt: (b, 0, 0)),
            scratch_shapes=[pltpu.VMEM((2, D_IDX, past), F32), pltpu.SemaphoreType.DMA((2,))],
        ),
        out_shape=jax.ShapeDtypeStruct((db, n_new, total), I32),
        compiler_params=_cparams(("arbitrary",)),
        name="sample_scores",
    )(page_table, iq32, w32, iknew, cache_kidx)

    keys_cm = keys.reshape(n_rows, nchunk, LANES).transpose(1, 0, 2)
    thr, cut = pl.pallas_call(
        functools.partial(_sample_search_kernel, nchunk=nchunk, rows=srows, topk=topk),
        grid=(n_rows // srows,),
        in_specs=[pl.BlockSpec((nchunk, srows, LANES), lambda i: (0, i, 0))],
        out_specs=[pl.BlockSpec((srows, LANES), lambda i: (i, 0))] * 2,
        out_shape=[jax.ShapeDtypeStruct((n_rows, LANES), I32)] * 2,
        scratch_shapes=[pltpu.VMEM((srows, LANES), I32), pltpu.VMEM((srows, LANES), F32),
                        pltpu.VMEM((srows, LANES), F32)],
        compiler_params=_cparams(("parallel",)),
        name="sample_search",
    )(keys_cm)

    thr3 = thr.reshape(db, n_new, LANES)
    cut3 = cut.reshape(db, n_new, LANES)
    return pl.pallas_call(
        functools.partial(_sample_attend_kernel, n_pages=n_pages, n_new=n_new, cw=cw),
        grid_spec=pltpu.PrefetchScalarGridSpec(
            num_scalar_prefetch=1,
            grid=(db,),
            in_specs=[per_b(q16), per_b(knew), per_b(vnew), per_b(keys), per_b(thr3), per_b(cut3), hbm, hbm],
            out_specs=pl.BlockSpec((1, rows_q, DH_A), lambda b, pt: (b, 0, 0)),
            scratch_shapes=[pltpu.VMEM((2, past, DH_A), F32), pltpu.VMEM((2, past, DH_A), F32),
                            pltpu.SemaphoreType.DMA((2, 2)), pltpu.VMEM((rows_q, total), F32)],
        ),
        out_shape=jax.ShapeDtypeStruct((db, rows_q, DH_A), BF16),
        compiler_params=_cparams(("arbitrary",)),
        name="sample_attend",
    )(page_table, q16, knew, vnew, keys, thr3, cut3, cache_k, cache_v)


def _outffn_kernel(x_ref, hm_ref, ha_ref, mod_ref, g2_ref, wom_ref, woa_ref, wg_ref, wu_ref, wd_ref, y_ref):
    mix = _dot(hm_ref[...], wom_ref[...]) + _dot(ha_ref[...], woa_ref[...])
    x1 = x_ref[...] + mod_ref[0, 0] * mix
    xn = x1 * lax.rsqrt(jnp.mean(x1 * x1, axis=-1, keepdims=True) + EPS) * g2_ref[...]
    hb = (xn * (1.0 + mod_ref[2, 0]) + mod_ref[1, 0]).astype(BF16)
    g = _dot(hb, wg_ref[...])
    u = _dot(hb, wu_ref[...])
    act = (g * _sigmoid(g) * u).astype(BF16)
    y_ref[...] = x1 + mod_ref[3, 0] * _dot(act, wd_ref[...])


def _outffn(x, hm, ha, mod_out, per_token_mod, rows_per_mod, g2, wom, woa, wg, wu, wd, tm):
    n, d = x.shape
    if per_token_mod:
        mod_spec = pl.BlockSpec((4, 1, tm, d), lambda i: (0, 0, i, 0))
    else:
        tiles = rows_per_mod // tm
        mod_spec = pl.BlockSpec((4, 1, 1, d), lambda i: (0, i // tiles, 0, 0))

    def full(a):
        return pl.BlockSpec(a.shape, lambda i: (0,) * a.ndim, pipeline_mode=pl.Buffered(1))

    def rows(w):
        return pl.BlockSpec((tm, w), lambda i: (i, 0))

    return pl.pallas_call(
        _outffn_kernel,
        grid=(n // tm,),
        in_specs=[rows(d), rows(512), rows(512), mod_spec, full(g2), full(wom), full(woa),
                  full(wg), full(wu), full(wd)],
        out_specs=rows(d),
        out_shape=jax.ShapeDtypeStruct((n, d), F32),
        compiler_params=_cparams(("parallel",)),
        name="outffn",
    )(x, hm, ha, mod_out, g2, wom, woa, wg, wu, wd)


def _pick_tile(n, pref):
    t = pref
    while n % t:
        t //= 2
    return t


def kernel(x_prompt, x_sample, cache_k, cache_v, cache_kidx, state_C, state_n, state_m, page_table,
           c_prompt, c_sample, w_ada, b_ada, g_norm1, w_in, b_igate, b_fgate, mlstm_norm_g,
           q_norm_g, k_norm_g, w_out, g_norm2, w_gate, w_up, w_down):
    bp, seq, d = x_prompt.shape
    db, t_new, _ = x_sample.shape
    n_p, n_s = bp * seq, db * t_new

    o = np.cumsum([0, 512, 512, 512, 512, H_M, H_M, 512, DH_A, DH_A, H_IDX * D_IDX, D_IDX, H_IDX])
    mq, mk, mv, mo, mi, mf, aq, ak, av, iq, ik, iw = [w_in[:, int(o[j]):int(o[j + 1])] for j in range(12)]
    zpad = lambda wdt: jnp.zeros((d, wdt), w_in.dtype)
    w_r = jnp.concatenate([mq, mk, mv, mo, aq, ak, av, iq, ik, zpad(LANES - D_IDX),
                           iw, mi, mf, zpad(LANES - 16)], axis=1).astype(BF16)
    wgt = jnp.concatenate([mi, mf], axis=1).T.astype(BF16)
    bsm = jnp.concatenate([jnp.zeros((H_IDX,), F32), b_igate, b_fgate]).reshape(1, 16)
    brow = jnp.concatenate([b_igate, b_fgate]).reshape(8, 1)
    g1 = g_norm1.reshape(1, d)
    g2 = g_norm2.reshape(1, d)
    qg = q_norm_g.reshape(1, DH_A)
    kg = k_norm_g.reshape(1, DH_A)
    ng = mlstm_norm_g.reshape(1, H_M * DH_M)
    wom = w_out[0:H_M * DH_M].astype(BF16)
    woa = w_out[H_M * DH_M:].astype(BF16)
    wg = w_gate.astype(BF16)
    wu = w_up.astype(BF16)
    wd = w_down.astype(BF16)

    mod = _ada(jnp.concatenate([c_prompt, c_sample], axis=0), w_ada.astype(BF16), b_ada)
    mod_p = mod[:bp].reshape(bp, 6, 1, d).transpose(1, 0, 2, 3)
    mod_s = jnp.repeat(mod[bp:].reshape(db, 6, d), t_new, axis=0)
    mod_s = mod_s.transpose(1, 0, 2).reshape(6, 1, n_s, d)

    tm_p = _pick_tile(seq, 512)
    (mqkv, mo_p, aq_p, k_p, v_p, kidx_p, kb, vb, ikb, iq_p, small_p, grow_p) = _inproj(
        x_prompt.reshape(n_p, d), mod_p[0:2], False, seq, g1, w_r, wgt, bsm, brow, qg, kg, tm_p)

    lc = _pick_tile(seq, 256)
    grow3 = grow_p.reshape(8, n_p // lc, lc).transpose(1, 0, 2)
    hm_p, C_p, n_pst, m_pb = _mlstm(
        mqkv, small_p, grow3, mo_p, ng,
        jnp.zeros((bp, H_M, DH_M, DH_M), F32), jnp.zeros((bp, H_M, DH_M), F32),
        jnp.zeros((bp, H_M, LANES), F32), bp, lc)

    tq = _pick_tile(seq, 512)
    ha_p = _dsa_prompt(iq_p, small_p, aq_p, ikb, kb, vb, bp, seq, tq)

    y_p = _outffn(x_prompt.reshape(n_p, d), hm_p, ha_p, mod_p[2:6], False, seq, g2,
                  wom, woa, wg, wu, wd, tm_p)

    tm_s = _pick_tile(n_s, 128)
    (mqkv_s, mo_s, aq_s, k_s, v_s, kidx_s, kb_s, vb_s, ikb_s, iq_s, small_s, grow_s) = _inproj(
        x_sample.reshape(n_s, d), mod_s[0:2], True, 0, g1, w_r, wgt, bsm, brow, qg, kg, tm_s)

    lp = 16
    pad_tok = lambda a: jnp.pad(a.reshape(db, t_new, a.shape[-1]),
                                ((0, 0), (0, lp - t_new), (0, 0))).reshape(db * lp, a.shape[-1])
    gate_pad = jnp.concatenate([jnp.zeros((H_IDX,), F32), jnp.full((H_M,), -jnp.inf, F32),
                                jnp.zeros((H_M,), F32)])
    small_pad = jnp.concatenate(
        [small_s.reshape(db, t_new, 16), jnp.broadcast_to(gate_pad, (db, lp - t_new, 16))], axis=1
    ).reshape(db * lp, 16)
    grow_pad = jnp.concatenate(
        [grow_s.reshape(8, db, t_new),
         jnp.broadcast_to(gate_pad[H_IDX:].reshape(8, 1, 1), (8, db, lp - t_new))], axis=2
    ).transpose(1, 0, 2)
    hm_s_pad, C_s, n_sst, m_sb = _mlstm(
        pad_tok(mqkv_s), small_pad, grow_pad, pad_tok(mo_s), ng,
        state_C, state_n, jnp.broadcast_to(state_m[:, :, None], (db, H_M, LANES)), db, lp)
    hm_s = hm_s_pad.reshape(db, lp, H_M * DH_M)[:, :t_new].reshape(n_s, H_M * DH_M)

    pad_rows = lambda a: jnp.pad(a.reshape(db, t_new, a.shape[-1]), ((0, 0), (0, LANES - t_new), (0, 0)))
    ha_s = _dsa_sample(
        page_table, iq_s.reshape(db, t_new * H_IDX, D_IDX),
        small_s[:, 0:H_IDX].reshape(db, t_new * H_IDX, 1),
        aq_s.reshape(db, t_new * H_A, DH_A),
        pad_rows(ikb_s), pad_rows(kb_s), pad_rows(vb_s),
        jnp.swapaxes(cache_kidx, 1, 2), cache_k, cache_v, t_new).reshape(n_s, H_A * DH_A)

    y_s = _outffn(x_sample.reshape(n_s, d), hm_s, ha_s, mod_s[2:6], True, 0, g2,
                  wom, woa, wg, wu, wd, tm_s)

    return (y_p.reshape(bp, seq, d), y_s.reshape(db, t_new, d),
            k_p.reshape(bp, seq, DH_A), v_p.reshape(bp, seq, DH_A), kidx_p.reshape(bp, seq, D_IDX),
            C_p, n_pst, m_pb[:, :, 0],
            k_s.reshape(db, t_new, DH_A), v_s.reshape(db, t_new, DH_A), kidx_s.reshape(db, t_new, D_IDX),
            C_s, n_sst, m_sb[:, :, 0])
```

```python
import functools

import jax
import jax.numpy as jnp
import numpy as np
from jax import lax
from jax.experimental import pallas as pl
from jax.experimental.pallas import tpu as pltpu

F32 = jnp.float32
BF16 = jnp.bfloat16
I32 = jnp.int32

H_M = 4
DH_M = 128
H_A = 4
DH_A = 128
H_IDX = 8
D_IDX = 64
TOPK_MAX = 256
PAGE = 128
EPS = 1e-6
INDEX_SCALE = D_IDX ** -0.5 * H_IDX ** -0.5
ALIBI_SLOPES = tuple(float(2.0 ** (-8.0 * (h + 1) / H_A)) for h in range(H_A))

LANES = 128
VMEM_LIMIT = 56 * 1024 * 1024
NEG = -1e30
I32_MIN = -2 ** 31
F32_TINY = float(np.finfo(np.float32).tiny)
NEGINF_KEY = -2139095041
INDEX_BITS = 14

C_MQ, C_MK, C_MV, C_MO, C_AQ, C_AK, C_AV, C_IQ, C_IK, C_SM, C_END = (
    0, 512, 1024, 1536, 2048, 2560, 2688, 2816, 3328, 3456, 3584)


def _cparams(sem):
    return pltpu.CompilerParams(dimension_semantics=sem, vmem_limit_bytes=VMEM_LIMIT)


def _sigmoid(x):
    return 1.0 / (1.0 + jnp.exp(-x))


def _log_sigmoid(x):
    return jnp.minimum(x, 0.0) - jnp.log1p(jnp.exp(-jnp.abs(x)))


def _dot(a, b):
    return jnp.dot(a, b, preferred_element_type=F32)


def _dot_nt(a, b):
    return lax.dot_general(a, b, (((1,), (1,)), ((), ())), preferred_element_type=F32)


def _dot_tn(a, b):
    return lax.dot_general(a, b, (((0,), (0,)), ((), ())), preferred_element_type=F32)


def _split3(x):
    hi = x.astype(BF16)
    r1 = x - hi.astype(F32)
    mid = r1.astype(BF16)
    lo = (r1 - mid.astype(F32)).astype(BF16)
    return hi, mid, lo


def _sort_key(x):
    b = pltpu.bitcast(x, I32)
    return b ^ ((b >> 31) & 0x7FFFFFFF)


def _score_key(score, pos):
    return jnp.where(jnp.abs(score) < F32_TINY, -pos, _sort_key(score))


def _ada_kernel(c_ref, w_ref, b_ref, o_ref):
    c = c_ref[...]
    s = c * _sigmoid(c)
    o_ref[...] = _dot(s.astype(BF16), w_ref[...]) + b_ref[...]


def _ada(c, w_bf, b):
    r, d = c.shape
    n = w_bf.shape[1]
    return pl.pallas_call(
        _ada_kernel,
        grid=(n // d,),
        in_specs=[pl.BlockSpec((r, d), lambda j: (0, 0)),
                  pl.BlockSpec((d, d), lambda j: (0, j)),
                  pl.BlockSpec((1, d), lambda j: (0, j))],
        out_specs=pl.BlockSpec((r, d), lambda j: (0, j)),
        out_shape=jax.ShapeDtypeStruct((r, n), F32),
        compiler_params=_cparams(("arbitrary",)),
        name="ada",
    )(c, w_bf, b.reshape(1, n))


def _inproj_kernel(x_ref, mod_ref, g1_ref, w_ref, wgt_ref, bsm_ref, brow_ref, qg_ref, kg_ref,
                   mqkv_ref, mo_ref, aq_ref, k_ref, v_ref, kidx_ref, kb_ref, vb_ref, ikb_ref,
                   iq_ref, small_ref, grow_ref):
    x = x_ref[...]
    xn = x * lax.rsqrt(jnp.mean(x * x, axis=-1, keepdims=True) + EPS) * g1_ref[...]
    h = xn * (1.0 + mod_ref[1, 0]) + mod_ref[0, 0]
    hb = h.astype(BF16)

    def sec(a, b):
        return _dot(hb, w_ref[:, a:b])

    mqkv_ref[:, 0:512] = sec(C_MQ, C_MK).astype(BF16)
    mqkv_ref[:, 512:1024] = (sec(C_MK, C_MV) * (DH_M ** -0.5)).astype(BF16)
    mqkv_ref[:, 1024:1536] = sec(C_MV, C_MO).astype(BF16)
    mo_ref[...] = sec(C_MO, C_AQ)

    aq = sec(C_AQ, C_AK)
    qg = qg_ref[...]
    for hh in range(H_A):
        a = aq[:, hh * DH_A:(hh + 1) * DH_A]
        a = a * lax.rsqrt(jnp.mean(a * a, axis=-1, keepdims=True) + EPS) * qg
        aq_ref[:, hh * DH_A:(hh + 1) * DH_A] = (a * (DH_A ** -0.5)).astype(BF16)

    ak = sec(C_AK, C_AV)
    ak = ak * lax.rsqrt(jnp.mean(ak * ak, axis=-1, keepdims=True) + EPS) * kg_ref[...]
    k_ref[...] = ak
    kb_ref[...] = ak.astype(BF16)
    av = sec(C_AV, C_IQ)
    v_ref[...] = av
    vb_ref[...] = av.astype(BF16)

    iq_ref[...] = sec(C_IQ, C_IK).astype(BF16)
    ik = sec(C_IK, C_SM)[:, 0:D_IDX]
    kidx_ref[...] = ik
    ikb_ref[...] = ik.astype(BF16)

    sm = sec(C_SM, C_END)[:, 0:16] + bsm_ref[...]
    col = lax.broadcasted_iota(I32, sm.shape, 1)
    small_ref[...] = jnp.where(col < H_IDX, sm * INDEX_SCALE,
                               jnp.where(col < H_IDX + H_M, sm, _log_sigmoid(sm)))

    gr = _dot_nt(wgt_ref[...], hb) + brow_ref[...]
    row = lax.broadcasted_iota(I32, gr.shape, 0)
    grow_ref[...] = jnp.where(row < H_M, gr, _log_sigmoid(gr))


def _inproj(x, mod_in, per_token_mod, rows_per_mod, g1, w_r, wgt, bsm, brow, qg, kg, tm):
    n, d = x.shape
    if per_token_mod:
        mod_spec = pl.BlockSpec((2, 1, tm, d), lambda i: (0, 0, i, 0))
    else:
        tiles = rows_per_mod // tm
        mod_spec = pl.BlockSpec((2, 1, 1, d), lambda i: (0, i // tiles, 0, 0))

    def full(a):
        return pl.BlockSpec(a.shape, lambda i: (0,) * a.ndim)

    def rows(w):
        return pl.BlockSpec((tm, w), lambda i: (i, 0))

    outs = [(1536, BF16), (512, F32), (512, BF16), (128, F32), (128, F32), (D_IDX, F32),
            (128, BF16), (128, BF16), (D_IDX, BF16), (512, BF16), (16, F32)]
    out_shape = [jax.ShapeDtypeStruct((n, w), dt) for w, dt in outs]
    out_specs = [rows(w) for w, _ in outs]
    out_shape.append(jax.ShapeDtypeStruct((8, n), F32))
    out_specs.append(pl.BlockSpec((8, tm), lambda i: (0, i)))
    return pl.pallas_call(
        _inproj_kernel,
        grid=(n // tm,),
        in_specs=[rows(d), mod_spec, full(g1), full(w_r), full(wgt), full(bsm), full(brow),
                  full(qg), full(kg)],
        out_specs=out_specs,
        out_shape=out_shape,
        compiler_params=_cparams(("parallel",)),
        name="inproj",
    )(x, mod_in, g1, w_r, wgt, bsm, brow, qg, kg)


def _mlstm_kernel(q_ref, k_ref, v_ref, small_ref, grow_ref, o_ref, ng_ref, c0_ref, n0_ref, m0_ref,
                  h_ref, c_out_ref, n_out_ref, m_out_ref, state_sc, m_sc, *, L, nc):
    c = pl.program_id(1)

    @pl.when(c == 0)
    def _():
        for hh in range(H_M):
            state_sc[hh, 0:DH_M, :] = c0_ref[0, hh]
            state_sc[hh, DH_M:2 * DH_M, :] = jnp.broadcast_to(n0_ref[0, hh:hh + 1, :], (DH_M, DH_M))
        m_sc[...] = jnp.zeros(m_sc.shape, F32)
        m_sc[0:H_M, :] = m0_ref[0]

    ti = lax.broadcasted_iota(I32, (L, L), 0)
    si = lax.broadcasted_iota(I32, (L, L), 1)
    causal = si <= ti
    tri = jnp.where(causal, 1.0, 0.0).astype(BF16)
    tri_t = jnp.where(ti <= si, 1.0, 0.0).astype(BF16)

    sm = small_ref[...]
    col = lax.broadcasted_iota(I32, sm.shape, 1)
    lf_cols = jnp.where(col >= H_IDX + H_M, sm, 0.0)
    hi, mid, lo = _split3(lf_cols)
    b_cols = _dot(tri, hi) + _dot(tri, mid) + _dot(tri, lo)
    gr = grow_ref[0]
    row = lax.broadcasted_iota(I32, gr.shape, 0)
    lf_rows = jnp.where(row >= H_M, gr, 0.0)
    hi, mid, lo = _split3(lf_rows)
    b_rows = _dot(hi, tri_t) + _dot(mid, tri_t) + _dot(lo, tri_t)

    ones_blk = jnp.ones((L, DH_M), BF16)

    def across(x):
        return x[:, 0:L] if L <= LANES else jnp.concatenate([x] * (L // LANES), axis=1)

    for hh in range(H_M):
        q = q_ref[:, hh * DH_M:(hh + 1) * DH_M]
        k = k_ref[:, hh * DH_M:(hh + 1) * DH_M]
        v = v_ref[:, hh * DH_M:(hh + 1) * DH_M]
        ig_c = jnp.broadcast_to(sm[:, H_IDX + hh:H_IDX + hh + 1], (L, LANES))
        b_c = jnp.broadcast_to(b_cols[:, H_IDX + H_M + hh:H_IDX + H_M + hh + 1], (L, LANES))
        ig_r = gr[hh:hh + 1, :]
        b_r = b_rows[H_M + hh:H_M + hh + 1, :]
        m_prev = m_sc[hh:hh + 1, :]

        dmat = jnp.where(causal, across(b_c) - b_r + ig_r, -jnp.inf)
        inter = b_c + m_prev
        m_t = jnp.maximum(inter, jnp.max(dmat, axis=-1, keepdims=True))
        w = jnp.exp(dmat - across(m_t))
        g = jnp.exp(inter - m_t)
        a = w * _dot_nt(q, k)
        st = state_sc[hh]
        qs = _dot_nt(q, st.astype(BF16))
        num = g * qs[:, 0:DH_M] + _dot(a.astype(BF16), v)
        den = g * qs[:, DH_M:2 * DH_M] + jnp.sum(a, axis=-1, keepdims=True)
        hv = num / jnp.maximum(jnp.abs(den), jnp.exp(-m_t))

        m_new = m_t[L - 1:L, :]
        b_last = b_c[L - 1:L, :]
        wend = jnp.exp(b_last - b_c + ig_c - m_new)
        gend = jnp.exp(b_last + m_prev - m_new)
        kw = (k.astype(F32) * wend).astype(BF16)
        v_aug = jnp.concatenate([v, ones_blk], axis=1)
        state_sc[hh] = gend * st + _dot_tn(v_aug, kw)
        m_sc[hh:hh + 1, :] = m_new

        hn = hv * lax.rsqrt(jnp.mean(hv * hv, axis=-1, keepdims=True) + EPS)
        hn = hn * ng_ref[:, hh * DH_M:(hh + 1) * DH_M]
        hn = hn * _sigmoid(o_ref[:, hh * DH_M:(hh + 1) * DH_M])
        h_ref[:, hh * DH_M:(hh + 1) * DH_M] = hn.astype(BF16)

    @pl.when(c == nc - 1)
    def _():
        for hh in range(H_M):
            c_out_ref[0, hh] = state_sc[hh, 0:DH_M, :]
            n_out_ref[0, hh:hh + 1, :] = state_sc[hh, DH_M:DH_M + 1, :]
        m_out_ref[0] = m_sc[0:H_M, :]


def _mlstm(mqkv, small, grow3, mo, ng, c0, n0, m0b, nb, L):
    n = mqkv.shape[0]
    nc = n // nb // L
    d = H_M * DH_M

    def tok(w, blk=0):
        return pl.BlockSpec((L, w), lambda b, c, blk=blk: (b * nc + c, blk))

    per_b3 = pl.BlockSpec((1, H_M, LANES), lambda b, c: (b, 0, 0))
    per_b4 = pl.BlockSpec((1, H_M, DH_M, DH_M), lambda b, c: (b, 0, 0, 0))
    return pl.pallas_call(
        functools.partial(_mlstm_kernel, L=L, nc=nc),
        grid=(nb, nc),
        in_specs=[tok(d, 0), tok(d, 1), tok(d, 2), tok(16),
                  pl.BlockSpec((1, 8, L), lambda b, c: (b * nc + c, 0, 0)),
                  tok(d), pl.BlockSpec((1, d), lambda b, c: (0, 0)),
                  per_b4, per_b3, per_b3],
        out_specs=[tok(d), per_b4, per_b3, per_b3],
        out_shape=[jax.ShapeDtypeStruct((n, d), BF16),
                   jax.ShapeDtypeStruct((nb, H_M, DH_M, DH_M), F32),
                   jax.ShapeDtypeStruct((nb, H_M, DH_M), F32),
                   jax.ShapeDtypeStruct((nb, H_M, LANES), F32)],
        scratch_shapes=[pltpu.VMEM((H_M, 2 * DH_M, DH_M), F32), pltpu.VMEM((8, LANES), F32)],
        compiler_params=_cparams(("parallel", "arbitrary")),
        name="mlstm",
    )(mqkv, mqkv, mqkv, small, grow3, mo, ng, c0, n0, m0b)


def _transpose32(words):
    a = list(words)
    j, m = 16, 0x0000FFFF
    while j:
        k = 0
        while k < 32:
            t = (a[k] ^ lax.shift_right_logical(a[k + j], jnp.int32(j))) & m
            a[k] = a[k] ^ t
            a[k + j] = a[k + j] ^ (t << j)
            k = (k + j + 1) & ~j
        j >>= 1
        m = (m ^ (m << j)) & 0xFFFFFFFF
    return a


def _topk_search(keys_sc, cand_sc, t_sc, tu_sc, aux_sc, cnt_sc, *, nchunk, rows, tk, hb, topk,
                 last_chunk_first_group=0, planes=None, nchunk_max=None):
    ncol = tk // LANES
    ngrp = rows // hb
    lane_hb = lax.broadcasted_iota(I32, (hb, LANES), 1)

    def count_rows(hit_fn):
        cnt_sc[...] = jnp.zeros(cnt_sc.shape, F32)

        def body(c, _, first_group=0):
            for g in range(first_group, ngrp):
                acc = cnt_sc[g * hb:(g + 1) * hb, :]
                for j in range(ncol):
                    kj = keys_sc[c, g * hb:(g + 1) * hb, j * LANES:(j + 1) * LANES]
                    acc = acc + jnp.where(hit_fn(c, g, j, kj), 1.0, 0.0)
                cnt_sc[g * hb:(g + 1) * hb, :] = acc
            return 0

        lax.fori_loop(0, nchunk - 1, body, 0)
        body(nchunk - 1, 0, last_chunk_first_group)
        return jnp.sum(cnt_sc[...], axis=1, keepdims=True)

    def hit_ge(c, g, j, kj):
        return kj >= cand_sc[g * hb:(g + 1) * hb, :]

    tu_sc[...] = jnp.zeros(tu_sc.shape, I32)
    if planes is None:
        def search_pass(p, _):
            bit = jnp.left_shift(jnp.int32(1), 31 - p)
            cand_sc[...] = (tu_sc[...] | bit) ^ I32_MIN
            tu_sc[...] = jnp.where(count_rows(hit_ge) >= topk, tu_sc[...] | bit, tu_sc[...])
            return 0

        lax.fori_loop(0, 32, search_pass, 0)
        t_sc[...] = tu_sc[...] ^ I32_MIN
        cand_sc[...] = t_sc[...]
        n_ge = count_rows(hit_ge)
    else:
        planes_sc, alive_sc, above_sc = planes
        nblk = nchunk_max * ncol
        nvalid = nchunk * ncol

        def tile_body(rt, _):
            r0 = pl.multiple_of(rt * 8, 8)
            words = []
            for j in range(32):
                if j < nblk:
                    kj = keys_sc[j // ncol, pl.ds(r0, 8), (j % ncol) * LANES:(j % ncol + 1) * LANES]
                    kj = jnp.where(j < nvalid, kj, NEGINF_KEY)
                else:
                    kj = jnp.full((8, LANES), NEGINF_KEY, I32)
                words.append(kj)
            for q, word in enumerate(_transpose32(words)):
                planes_sc[q, pl.ds(r0, 8), :] = ~word if q == 0 else word
            return 0

        lax.fori_loop(0, rows // 8, tile_body, 0)
        alive_sc[...] = jnp.full(alive_sc.shape, -1, I32)
        above_sc[...] = jnp.zeros(above_sc.shape, F32)

        def select_pass(q, _):
            alive = alive_sc[...]
            ones = alive & planes_sc[q]
            n_ones = jnp.sum(lax.population_count(ones).astype(F32), axis=1, keepdims=True)
            above = above_sc[...]
            take = above + n_ones >= topk
            alive_sc[...] = jnp.where(take, ones, alive ^ ones)
            above_sc[...] = jnp.where(take, above, above + n_ones)
            tu_sc[...] = jnp.where(take, tu_sc[...] | jnp.left_shift(jnp.int32(1), 31 - q), tu_sc[...])
            return 0

        lax.fori_loop(0, 32, select_pass, 0)
        t_sc[...] = tu_sc[...] ^ I32_MIN
        n_ge = above_sc[...] + jnp.sum(lax.population_count(alive_sc[...]).astype(F32), axis=1, keepdims=True)

    tied = jnp.logical_and(n_ge > topk, t_sc[...] > NEGINF_KEY)
    any_tied = jnp.max(jnp.where(tied, 1.0, 0.0)) > 0.0
    tu_sc[...] = jnp.full(tu_sc.shape, 2 ** 31 - 1, I32)

    @pl.when(any_tied)
    def _():
        cand_sc[...] = t_sc[...] + 1
        aux_sc[...] = jnp.broadcast_to(topk - count_rows(hit_ge), aux_sc.shape)

        def hit_tie_below(c, g, j, kj):
            pos = c * tk + j * LANES + lane_hb
            return jnp.where(kj == t_sc[g * hb:(g + 1) * hb, :], pos, 2 ** 30) < cand_sc[g * hb:(g + 1) * hb, :]

        tu_sc[...] = jnp.zeros(tu_sc.shape, I32)

        def cut_pass(p, _):
            bit = jnp.left_shift(jnp.int32(1), INDEX_BITS - 1 - p)
            cand_sc[...] = tu_sc[...] | bit
            tu_sc[...] = jnp.where(count_rows(hit_tie_below) < aux_sc[...], tu_sc[...] | bit, tu_sc[...])
            return 0

        lax.fori_loop(0, INDEX_BITS, cut_pass, 0)

    return any_tied


def _dsa_prompt_kernel(iq_ref, small_ref, aq_ref, qpos_ref, ik_ref, k_ref, v_ref, kpos_ref, o_ref,
                       qidx_sc, wrep_sc, qa_sc, s_sc, keys_sc, lg_sc, p_sc, alpha_sc, m_sc, l_sc, acc_sc,
                       cand_sc, t_sc, tu_sc, aux_sc, cnt_sc, planes_sc, alive_sc, above_sc,
                       *, tq, tk, topk, rb, hb, use_planes):
    i = pl.program_id(1)
    q0 = i * tq
    nchunk = (i + 1) * (tq // tk)
    ncol = tk // LANES
    nrb = tq // rb
    lane_hb = lax.broadcasted_iota(I32, (hb, LANES), 1)
    lane_rb = lax.broadcasted_iota(I32, (rb, LANES), 1)
    lane_minus_row = lane_rb - lax.broadcasted_iota(I32, (rb, LANES), 0)

    for h in range(H_IDX):
        qidx_sc[h * tq:(h + 1) * tq, :] = iq_ref[:, h * D_IDX:(h + 1) * D_IDX]
        wrep_sc[h] = jnp.broadcast_to(small_ref[:, h:h + 1], (tq, LANES))
    for hh in range(H_A):
        qa_sc[hh * tq:(hh + 1) * tq, 0:DH_A] = aq_ref[:, hh * DH_A:(hh + 1) * DH_A]
        qa_sc[hh * tq:(hh + 1) * tq, DH_A:2 * DH_A] = qpos_ref[0, hh * tq:(hh + 1) * tq, :]

    row_lo = tq - tk

    def score_body(c, _, lo=0):
        ikc = ik_ref[pl.ds(pl.multiple_of(c * tk, tk), tk), :]
        if lo:
            keys_sc[c, 0:lo, :] = jnp.full((lo, tk), NEGINF_KEY, I32)
        bound = q0 - c * tk
        for h in range(H_IDX):
            s_sc[h * tq + lo:(h + 1) * tq, :] = _dot_nt(qidx_sc[h * tq + lo:(h + 1) * tq, :], ikc)
            for r in range(lo // rb, nrb):
                wv = wrep_sc[h, r * rb:(r + 1) * rb, :]
                for j in range(ncol):
                    s = s_sc[h * tq + r * rb:h * tq + (r + 1) * rb, j * LANES:(j + 1) * LANES]
                    term = wv * jnp.maximum(s, 0.0)
                    if h:
                        term = lg_sc[r * rb:(r + 1) * rb, j * LANES:(j + 1) * LANES] + term
                    if h < H_IDX - 1:
                        lg_sc[r * rb:(r + 1) * rb, j * LANES:(j + 1) * LANES] = term
                    else:
                        acc = jnp.where(lane_minus_row <= bound + (r * rb - j * LANES), term, -jnp.inf)
                        keys_sc[c, r * rb:(r + 1) * rb, j * LANES:(j + 1) * LANES] = _score_key(
                            acc, c * tk + j * LANES + lane_rb)
        return 0

    lax.fori_loop(0, nchunk - 1, score_body, 0)
    score_body(nchunk - 1, 0, row_lo)

    any_tied = _topk_search(keys_sc, cand_sc, t_sc, tu_sc, aux_sc, cnt_sc, nchunk=nchunk, rows=tq, tk=tk, hb=hb,
                            topk=topk, last_chunk_first_group=(tq - tk) // hb,
                            planes=(planes_sc, alive_sc, above_sc) if use_planes else None,
                            nchunk_max=keys_sc.shape[0])

    @pl.when(any_tied)
    def _():
        def drop_body(c, _):
            for g in range(tq // hb):
                tb = t_sc[g * hb:(g + 1) * hb, :]
                xb = tu_sc[g * hb:(g + 1) * hb, :]
                for j in range(ncol):
                    kj = keys_sc[c, g * hb:(g + 1) * hb, j * LANES:(j + 1) * LANES]
                    pos = c * tk + j * LANES + lane_hb
                    late = jnp.where(kj == tb, pos, -1) > xb
                    keys_sc[c, g * hb:(g + 1) * hb, j * LANES:(j + 1) * LANES] = jnp.where(late, kj - 1, kj)
            return 0

        lax.fori_loop(0, nchunk, drop_body, 0)

    t_sc[...] = jnp.maximum(t_sc[...], NEGINF_KEY + 1)
    m_sc[...] = jnp.full(m_sc.shape, NEG, F32)
    l_sc[...] = jnp.zeros(l_sc.shape, F32)
    acc_sc[...] = jnp.zeros(acc_sc.shape, F32)

    def att_body(c, _, lo=0):
        start = pl.multiple_of(c * tk, tk)
        kaug = jnp.concatenate([k_ref[pl.ds(start, tk), :], kpos_ref[pl.ds(start, tk), :]], axis=1)
        head_rows = [slice(hh * tq + lo, (hh + 1) * tq) for hh in range(H_A)]
        if lo == 0:
            lg_sc[...] = _dot_nt(qa_sc[...], kaug)
        else:
            for rows in head_rows:
                lg_sc[rows, :] = _dot_nt(qa_sc[rows, :], kaug)
        for r in range(lo // rb, nrb):
            tb = t_sc[r * rb:(r + 1) * rb, :]
            selb = [jnp.where(keys_sc[c, r * rb:(r + 1) * rb, j * LANES:(j + 1) * LANES] >= tb, 0.0, NEG)
                    for j in range(ncol)]
            for hh in range(H_A):
                rows = slice(hh * tq + r * rb, hh * tq + (r + 1) * rb)
                lgs = [lg_sc[rows, j * LANES:(j + 1) * LANES] + selb[j] for j in range(ncol)]
                mx = lgs[0]
                for j in range(1, ncol):
                    mx = jnp.maximum(mx, lgs[j])
                m_old = m_sc[rows, :]
                m_new = jnp.maximum(m_old, jnp.max(mx, axis=1, keepdims=True))
                alpha = jnp.exp(m_old - m_new)
                ps = [jnp.exp(lg - m_new) for lg in lgs]
                psum = ps[0]
                for j in range(1, ncol):
                    psum = psum + ps[j]
                l_sc[rows, :] = alpha * l_sc[rows, :] + psum
                for j in range(ncol):
                    p_sc[rows, j * LANES:(j + 1) * LANES] = ps[j].astype(BF16)
                alpha_sc[rows, :] = alpha
                m_sc[rows, :] = m_new
        vc = v_ref[pl.ds(start, tk), :]
        if lo == 0:
            acc_sc[...] = alpha_sc[...] * acc_sc[...] + _dot(p_sc[...], vc)
        else:
            for rows in head_rows:
                acc_sc[rows, :] = alpha_sc[rows, :] * acc_sc[rows, :] + _dot(p_sc[rows, :], vc)
        return 0

    lax.fori_loop(0, nchunk - 1, att_body, 0)
    att_body(nchunk - 1, 0, row_lo)
    for hh in range(H_A):
        rows = slice(hh * tq, (hh + 1) * tq)
        l = jnp.sum(l_sc[rows, :], axis=1, keepdims=True)
        o_ref[:, hh * DH_A:(hh + 1) * DH_A] = (acc_sc[rows, :] / l).astype(BF16)


def _alibi_tables(seq, tq):
    pos = np.arange(seq)
    hi, lo = (pos // 64).astype(np.float32), (pos % 64).astype(np.float32)
    ktab = np.zeros((seq, DH_A), np.float32)
    ktab[:, 0], ktab[:, 1], ktab[:, 2], ktab[:, 3] = 64.0 * hi, lo, 1.0, 1.0
    nq = seq // tq
    qtab = np.zeros((nq, H_A, tq, DH_A), np.float32)
    for hh, slope in enumerate(ALIBI_SLOPES):
        qtab[:, hh, :, 0] = slope
        qtab[:, hh, :, 1] = slope
        qtab[:, hh, :, 2] = (-slope * 64.0 * hi).reshape(nq, tq)
        qtab[:, hh, :, 3] = (-slope * lo).reshape(nq, tq)
    return jnp.asarray(qtab.reshape(nq, H_A * tq, DH_A), BF16), jnp.asarray(ktab, BF16)


def _dsa_prompt(iq, small, aq, ikb, kb, vb, nb, seq, tq):
    n = iq.shape[0]
    nq = seq // tq
    topk = min(TOPK_MAX, seq // 4)
    tk = min(256, tq)
    rb = min(64, tq)
    hb = min(128, tq)
    use_planes = seq // LANES <= 32
    qtab, ktab = _alibi_tables(seq, tq)

    def tok(w):
        return pl.BlockSpec((tq, w), lambda b, i: (b * nq + i, 0))

    def per_b(w):
        return pl.BlockSpec((seq, w), lambda b, i: (b, 0))

    return pl.pallas_call(
        functools.partial(_dsa_prompt_kernel, tq=tq, tk=tk, topk=topk, rb=rb, hb=hb, use_planes=use_planes),
        grid=(nb, nq),
        in_specs=[tok(512), tok(16), tok(512),
                  pl.BlockSpec((1, H_A * tq, DH_A), lambda b, i: (i, 0, 0)),
                  per_b(D_IDX), per_b(DH_A), per_b(DH_A),
                  pl.BlockSpec((seq, DH_A), lambda b, i: (0, 0))],
        out_specs=tok(512),
        out_shape=jax.ShapeDtypeStruct((n, 512), BF16),
        scratch_shapes=[pltpu.VMEM((H_IDX * tq, D_IDX), BF16),
                        pltpu.VMEM((H_IDX, tq, LANES), F32),
                        pltpu.VMEM((H_A * tq, 2 * DH_A), BF16),
                        pltpu.VMEM((H_IDX * tq, tk), F32),
                        pltpu.VMEM((seq // tk, tq, tk), I32),
                        pltpu.VMEM((H_A * tq, tk), F32),
                        pltpu.VMEM((H_A * tq, tk), BF16),
                        pltpu.VMEM((H_A * tq, LANES), F32),
                        pltpu.VMEM((H_A * tq, LANES), F32),
                        pltpu.VMEM((H_A * tq, LANES), F32),
                        pltpu.VMEM((H_A * tq, DH_A), F32),
                        pltpu.VMEM((tq, LANES), I32),
                        pltpu.VMEM((tq, LANES), I32),
                        pltpu.VMEM((tq, LANES), I32),
                        pltpu.VMEM((tq, LANES), F32),
                        pltpu.VMEM((tq, LANES), F32),
                        pltpu.VMEM((32 if use_planes else 1, tq, LANES), I32),
                        pltpu.VMEM((tq, LANES), I32),
                        pltpu.VMEM((tq, LANES), F32)],
        compiler_params=_cparams(("parallel", "arbitrary")),
        name="dsa_prompt",
    )(iq, small, aq, qtab, ikb, kb, vb, ktab)


def _page_pipeline(pt_ref, copies_of, n_pages):
    b = pl.program_id(0)
    slot = b % 2

    def for_all(bb, sl, act):
        def body(j, _):
            for cp in copies_of(pt_ref[bb, j], sl, j):
                act(cp)
            return 0
        lax.fori_loop(0, n_pages, body, 0, unroll=8 if n_pages % 8 == 0 else 1)

    @pl.when(b == 0)
    def _():
        for_all(0, 0, lambda cp: cp.start())

    @pl.when(b + 1 < pl.num_programs(0))
    def _():
        for_all(b + 1, 1 - slot, lambda cp: cp.start())

    for_all(b, slot, lambda cp: cp.wait())
    return slot


def _sample_scores_kernel(pt_ref, iq_ref, w_ref, iknew_ref, ckidx_hbm, keys_ref, ibuf, sems,
                          *, n_pages, n_new, cw):
    past = n_pages * PAGE
    total = past + LANES

    def copies_of(pg, sl, j):
        dst = pl.ds(pl.multiple_of(j * PAGE, PAGE), PAGE)
        return (pltpu.make_async_copy(ckidx_hbm.at[pg], ibuf.at[sl, :, dst], sems.at[sl]),)

    slot = _page_pipeline(pt_ref, copies_of, n_pages)

    iq = iq_ref[0]
    w = w_ref[0]

    def scores(dots):
        s = jnp.maximum(dots, 0.0) * w
        return jnp.sum(s.reshape(n_new, H_IDX, s.shape[-1]), axis=1)

    for ch in range(past // cw):
        sc = scores(_dot(iq, ibuf[slot, :, ch * cw:(ch + 1) * cw].astype(BF16)))
        keys_ref[0, :, ch * cw:(ch + 1) * cw] = _score_key(sc, ch * cw + lax.broadcasted_iota(I32, sc.shape, 1))
    sc = scores(_dot_nt(iq, iknew_ref[0]))
    t_i = lax.broadcasted_iota(I32, (n_new, LANES), 0)
    j_i = lax.broadcasted_iota(I32, (n_new, LANES), 1)
    keys_ref[0, :, past:total] = _score_key(jnp.where(j_i <= t_i, sc, -jnp.inf), past + j_i)


def _sample_search_kernel(keys_ref, t_ref, x_ref, cand_sc, aux_sc, cnt_sc, *, nchunk, rows, topk):
    _topk_search(keys_ref, cand_sc, t_ref, x_ref, aux_sc, cnt_sc,
                 nchunk=nchunk, rows=rows, tk=LANES, hb=rows, topk=topk)


def _sample_attend_kernel(pt_ref, q_ref, knew_ref, vnew_ref, keys_ref, t_ref, x_ref, ck_hbm, cv_hbm, o_ref,
                          kbuf, vbuf, sems, lg_sc, *, n_pages, n_new, cw):
    past = n_pages * PAGE
    total = past + LANES
    rows_q = n_new * H_A

    def copies_of(pg, sl, j):
        dst = pl.ds(pl.multiple_of(j * PAGE, PAGE), PAGE)
        return (pltpu.make_async_copy(ck_hbm.at[pg], kbuf.at[sl, dst, :], sems.at[0, sl]),
                pltpu.make_async_copy(cv_hbm.at[pg], vbuf.at[sl, dst, :], sems.at[1, sl]))

    slot = _page_pipeline(pt_ref, copies_of, n_pages)

    r_i = lax.broadcasted_iota(I32, (rows_q, 1), 0)
    r_t = r_i // H_A
    r_h = r_i % H_A
    slope = jnp.zeros((rows_q, 1), F32)
    for hh in range(H_A):
        slope = jnp.where(r_h == hh, ALIBI_SLOPES[hh], slope)
    qposf = (past + r_t).astype(F32)
    q = q_ref[0]

    def sel_bias(lo, width):
        pos = lo + lax.broadcasted_iota(I32, (1, width), 1)
        out = jnp.full((rows_q, width), NEG, F32)
        for t in range(n_new):
            kt = keys_ref[0, t:t + 1, lo:lo + width]
            thr = t_ref[0, t:t + 1, 0:1]
            keep_tie = jnp.logical_and(kt == thr, pos <= x_ref[0, t:t + 1, 0:1])
            chosen = jnp.logical_and(jnp.logical_or(kt > thr, keep_tie), kt > NEGINF_KEY)
            bias_t = jnp.where(chosen, 0.0, NEG)
            out = jnp.where(r_t == t, bias_t, out)
        return out

    def logits(k_chunk, lo, width):
        kposf = (lo + lax.broadcasted_iota(I32, (1, width), 1)).astype(F32)
        sel = sel_bias(lo, width)
        lg = _dot_nt(q, k_chunk) - slope * (qposf - kposf)
        return lg + sel

    for ch in range(past // cw):
        lg_sc[:, ch * cw:(ch + 1) * cw] = logits(kbuf[slot, ch * cw:(ch + 1) * cw, :].astype(BF16), ch * cw, cw)
    lg_sc[:, past:total] = logits(knew_ref[0], past, LANES)

    lg = lg_sc[...]
    m = jnp.max(lg, axis=1, keepdims=True)
    p = jnp.exp(lg - m)
    l = jnp.sum(p, axis=1, keepdims=True)
    pb = p.astype(BF16)
    acc = _dot(pb[:, past:total], vnew_ref[0])
    for ch in range(past // cw):
        acc = acc + _dot(pb[:, ch * cw:(ch + 1) * cw], vbuf[slot, ch * cw:(ch + 1) * cw, :].astype(BF16))
    o_ref[0] = (acc / l).astype(BF16)


def _dsa_sample(page_table, iq32, w32, q16, iknew, knew, vnew, cache_kidx, cache_k, cache_v, n_new):
    db, n_pages = page_table.shape
    past = n_pages * PAGE
    total = past + LANES
    topk = min(TOPK_MAX, (past + n_new) // 4)
    cw = 1024 if past % 1024 == 0 else PAGE
    rows_q = n_new * H_A

    n_rows = db * n_new
    nchunk = total // LANES
    srows = _pick_tile(n_rows, 128)

    def per_b(a):
        return pl.BlockSpec((1,) + a.shape[1:], lambda b, pt: (b, 0, 0))

    hbm = pl.BlockSpec(memory_space=pl.ANY)

    keys = pl.pallas_call(
        functools.partial(_sample_scores_kernel, n_pages=n_pages, n_new=n_new, cw=cw),
        grid_spec=pltpu.PrefetchScalarGridSpec(
            num_scalar_prefetch=1,
            grid=(db,),
            in_specs=[per_b(iq32), per_b(w32), per_b(iknew), hbm],
            out_specs=pl.BlockSpec((1, n_new, total), lambda b, pt: (b, 0, 0)),
            scratch_shapes=[pltpu.VMEM((2, D_IDX, past), F32), pltpu.SemaphoreType.DMA((2,))],
        ),
        out_shape=jax.ShapeDtypeStruct((db, n_new, total), I32),
        compiler_params=_cparams(("arbitrary",)),
        name="sample_scores",
    )(page_table, iq32, w32, iknew, cache_kidx)

    keys_cm = keys.reshape(n_rows, nchunk, LANES).transpose(1, 0, 2)
    thr, cut = pl.pallas_call(
        functools.partial(_sample_search_kernel, nchunk=nchunk, rows=srows, topk=topk),
        grid=(n_rows // srows,),
        in_specs=[pl.BlockSpec((nchunk, srows, LANES), lambda i: (0, i, 0))],
        out_specs=[pl.BlockSpec((srows, LANES), lambda i: (i, 0))] * 2,
        out_shape=[jax.ShapeDtypeStruct((n_rows, LANES), I32)] * 2,
        scratch_shapes=[pltpu.VMEM((srows, LANES), I32), pltpu.VMEM((srows, LANES), F32),
                        pltpu.VMEM((srows, LANES), F32)],
        compiler_params=_cparams(("parallel",)),
        name="sample_search",
    )(keys_cm)

    thr3 = thr.reshape(db, n_new, LANES)
    cut3 = cut.reshape(db, n_new, LANES)
    return pl.pallas_call(
        functools.partial(_sample_attend_kernel, n_pages=n_pages, n_new=n_new, cw=cw),
        grid_spec=pltpu.PrefetchScalarGridSpec(
            num_scalar_prefetch=1,
            grid=(db,),
            in_specs=[per_b(q16), per_b(knew), per_b(vnew), per_b(keys), per_b(thr3), per_b(cut3), hbm, hbm],
            out_specs=pl.BlockSpec((1, rows_q, DH_A), lambda b, pt: (b, 0, 0)),
            scratch_shapes=[pltpu.VMEM((2, past, DH_A), F32), pltpu.VMEM((2, past, DH_A), F32),
                            pltpu.SemaphoreType.DMA((2, 2)), pltpu.VMEM((rows_q, total), F32)],
        ),
        out_shape=jax.ShapeDtypeStruct((db, rows_q, DH_A), BF16),
        compiler_params=_cparams(("arbitrary",)),
        name="sample_attend",
    )(page_table, q16, knew, vnew, keys, thr3, cut3, cache_k, cache_v)


def _outffn_kernel(x_ref, hm_ref, ha_ref, mod_ref, g2_ref, wom_ref, woa_ref, wg_ref, wu_ref, wd_ref, y_ref):
    mix = _dot(hm_ref[...], wom_ref[...]) + _dot(ha_ref[...], woa_ref[...])
    x1 = x_ref[...] + mod_ref[0, 0] * mix
    xn = x1 * lax.rsqrt(jnp.mean(x1 * x1, axis=-1, keepdims=True) + EPS) * g2_ref[...]
    hb = (xn * (1.0 + mod_ref[2, 0]) + mod_ref[1, 0]).astype(BF16)
    g = _dot(hb, wg_ref[...])
    u = _dot(hb, wu_ref[...])
    act = (g * _sigmoid(g) * u).astype(BF16)
    y_ref[...] = x1 + mod_ref[3, 0] * _dot(act, wd_ref[...])


def _outffn(x, hm, ha, mod_out, per_token_mod, rows_per_mod, g2, wom, woa, wg, wu, wd, tm):
    n, d = x.shape
    if per_token_mod:
        mod_spec = pl.BlockSpec((4, 1, tm, d), lambda i: (0, 0, i, 0))
    else:
        tiles = rows_per_mod // tm
        mod_spec = pl.BlockSpec((4, 1, 1, d), lambda i: (0, i // tiles, 0, 0))

    def full(a):
        return pl.BlockSpec(a.shape, lambda i: (0,) * a.ndim, pipeline_mode=pl.Buffered(1))

    def rows(w):
        return pl.BlockSpec((tm, w), lambda i: (i, 0))

    return pl.pallas_call(
        _outffn_kernel,
        grid=(n // tm,),
        in_specs=[rows(d), rows(512), rows(512), mod_spec, full(g2), full(wom), full(woa),
                  full(wg), full(wu), full(wd)],
        out_specs=rows(d),
        out_shape=jax.ShapeDtypeStruct((n, d), F32),
        compiler_params=_cparams(("parallel",)),
        name="outffn",
    )(x, hm, ha, mod_out, g2, wom, woa, wg, wu, wd)


def _pick_tile(n, pref):
    t = pref
    while n % t:
        t //= 2
    return t


def kernel(x_prompt, x_sample, cache_k, cache_v, cache_kidx, state_C, state_n, state_m, page_table,
           c_prompt, c_sample, w_ada, b_ada, g_norm1, w_in, b_igate, b_fgate, mlstm_norm_g,
           q_norm_g, k_norm_g, w_out, g_norm2, w_gate, w_up, w_down):
    bp, seq, d = x_prompt.shape
    db, t_new, _ = x_sample.shape
    n_p, n_s = bp * seq, db * t_new

    o = np.cumsum([0, 512, 512, 512, 512, H_M, H_M, 512, DH_A, DH_A, H_IDX * D_IDX, D_IDX, H_IDX])
    mq, mk, mv, mo, mi, mf, aq, ak, av, iq, ik, iw = [w_in[:, int(o[j]):int(o[j + 1])] for j in range(12)]
    zpad = lambda wdt: jnp.zeros((d, wdt), w_in.dtype)
    w_r = jnp.concatenate([mq, mk, mv, mo, aq, ak, av, iq, ik, zpad(LANES - D_IDX),
                           iw, mi, mf, zpad(LANES - 16)], axis=1).astype(BF16)
    wgt = jnp.concatenate([mi, mf], axis=1).T.astype(BF16)
    bsm = jnp.concatenate([jnp.zeros((H_IDX,), F32), b_igate, b_fgate]).reshape(1, 16)
    brow = jnp.concatenate([b_igate, b_fgate]).reshape(8, 1)
    g1 = g_norm1.reshape(1, d)
    g2 = g_norm2.reshape(1, d)
    qg = q_norm_g.reshape(1, DH_A)
    kg = k_norm_g.reshape(1, DH_A)
    ng = mlstm_norm_g.reshape(1, H_M * DH_M)
    wom = w_out[0:H_M * DH_M].astype(BF16)
    woa = w_out[H_M * DH_M:].astype(BF16)
    wg = w_gate.astype(BF16)
    wu = w_up.astype(BF16)
    wd = w_down.astype(BF16)

    mod = _ada(jnp.concatenate([c_prompt, c_sample], axis=0), w_ada.astype(BF16), b_ada)
    mod_p = mod[:bp].reshape(bp, 6, 1, d).transpose(1, 0, 2, 3)
    mod_s = jnp.repeat(mod[bp:].reshape(db, 6, d), t_new, axis=0)
    mod_s = mod_s.transpose(1, 0, 2).reshape(6, 1, n_s, d)

    tm_p = _pick_tile(seq, 512)
    (mqkv, mo_p, aq_p, k_p, v_p, kidx_p, kb, vb, ikb, iq_p, small_p, grow_p) = _inproj(
        x_prompt.reshape(n_p, d), mod_p[0:2], False, seq, g1, w_r, wgt, bsm, brow, qg, kg, tm_p)

    lc = _pick_tile(seq, 256)
    grow3 = grow_p.reshape(8, n_p // lc, lc).transpose(1, 0, 2)
    hm_p, C_p, n_pst, m_pb = _mlstm(
        mqkv, small_p, grow3, mo_p, ng,
        jnp.zeros((bp, H_M, DH_M, DH_M), F32), jnp.zeros((bp, H_M, DH_M), F32),
        jnp.zeros((bp, H_M, LANES), F32), bp, lc)

    tq = _pick_tile(seq, 512)
    ha_p = _dsa_prompt(iq_p, small_p, aq_p, ikb, kb, vb, bp, seq, tq)

    y_p = _outffn(x_prompt.reshape(n_p, d), hm_p, ha_p, mod_p[2:6], False, seq, g2,
                  wom, woa, wg, wu, wd, tm_p)

    tm_s = _pick_tile(n_s, 128)
    (mqkv_s, mo_s, aq_s, k_s, v_s, kidx_s, kb_s, vb_s, ikb_s, iq_s, small_s, grow_s) = _inproj(
        x_sample.reshape(n_s, d), mod_s[0:2], True, 0, g1, w_r, wgt, bsm, brow, qg, kg, tm_s)

    lp = 16
    pad_tok = lambda a: jnp.pad(a.reshape(db, t_new, a.shape[-1]),
                                ((0, 0), (0, lp - t_new), (0, 0))).reshape(db * lp, a.shape[-1])
    gate_pad = jnp.concatenate([jnp.zeros((H_IDX,), F32), jnp.full((H_M,), -jnp.inf, F32),
                                jnp.zeros((H_M,), F32)])
    small_pad = jnp.concatenate(
        [small_s.reshape(db, t_new, 16), jnp.broadcast_to(gate_pad, (db, lp - t_new, 16))], axis=1
    ).reshape(db * lp, 16)
    grow_pad = jnp.concatenate(
        [grow_s.reshape(8, db, t_new),
         jnp.broadcast_to(gate_pad[H_IDX:].reshape(8, 1, 1), (8, db, lp - t_new))], axis=2
    ).transpose(1, 0, 2)
    hm_s_pad, C_s, n_sst, m_sb = _mlstm(
        pad_tok(mqkv_s), small_pad, grow_pad, pad_tok(mo_s), ng,
        state_C, state_n, jnp.broadcast_to(state_m[:, :, None], (db, H_M, LANES)), db, lp)
    hm_s = hm_s_pad.reshape(db, lp, H_M * DH_M)[:, :t_new].reshape(n_s, H_M * DH_M)

    pad_rows = lambda a: jnp.pad(a.reshape(db, t_new, a.shape[-1]), ((0, 0), (0, LANES - t_new), (0, 0)))
    ha_s = _dsa_sample(
        page_table, iq_s.reshape(db, t_new * H_IDX, D_IDX),
        small_s[:, 0:H_IDX].reshape(db, t_new * H_IDX, 1),
        aq_s.reshape(db, t_new * H_A, DH_A),
        pad_rows(ikb_s), pad_rows(kb_s), pad_rows(vb_s),
        jnp.swapaxes(cache_kidx, 1, 2), cache_k, cache_v, t_new).reshape(n_s, H_A * DH_A)

    y_s = _outffn(x_sample.reshape(n_s, d), hm_s, ha_s, mod_s[2:6], True, 0, g2,
                  wom, woa, wg, wu, wd, tm_s)

    return (y_p.reshape(bp, seq, d), y_s.reshape(db, t_new, d),
            k_p.reshape(bp, seq, DH_A), v_p.reshape(bp, seq, DH_A), kidx_p.reshape(bp, seq, D_IDX),
            C_p, n_pst, m_pb[:, :, 0],
            k_s.reshape(db, t_new, DH_A), v_s.reshape(db, t_new, DH_A), kidx_s.reshape(db, t_new, D_IDX),
            C_s, n_sst, m_sb[:, :, 0])
```
